```python
import math
import jax, jax.numpy as jnp
from jax import lax
import numpy as np

D_MODEL = 2048
BATCH = 2
SEQ = 4096
DEPTH = 4

GRID_W = 64
CTX_LEN = 256
N_MIXERS = 3
ROPE_THETA = 10000.0
EPS = 1e-6

SSD_EXPAND = 2
SSD_D_INNER = SSD_EXPAND * D_MODEL
SSD_HEAD_DIM = 64
SSD_HEADS = SSD_D_INNER // SSD_HEAD_DIM
SSD_GROUPS = 8
SSD_STATE = 128
SSD_CONV = 5
SSD_CHUNK = 128
SSD_CONV_CH = SSD_D_INNER + 2 * SSD_GROUPS * SSD_STATE
SSD_IN = SSD_D_INNER + SSD_CONV_CH + 2 * SSD_HEADS

GQA_HEAD_DIM = 128
GQA_Q_HEADS = D_MODEL // GQA_HEAD_DIM
GQA_KV_HEADS = 4
GQA_REP = GQA_Q_HEADS // GQA_KV_HEADS
WINDOW = 128
BLOCK = 128

DIFF_HEAD_DIM = 128
DIFF_HEADS = D_MODEL // (2 * DIFF_HEAD_DIM)
DIFF_V_DIM = 2 * DIFF_HEAD_DIM
Q_BLOCK = 128

MLP_HIDDEN = 4 * D_MODEL

kernel_name = "hybrid_ssd_swa_diffattn_dit_prefix"


def rms_norm(x, g):
    xf = x.astype(jnp.float32)
    y = xf * lax.rsqrt(jnp.mean(xf * xf, axis=-1, keepdims=True) + EPS)
    return (y * g.astype(jnp.float32)).astype(x.dtype)


def grouped_rms_norm(y, g, n_groups):
    shp = y.shape
    yg = y.reshape(shp[:-1] + (n_groups, shp[-1] // n_groups))
    return rms_norm(yg, g.reshape(n_groups, -1)).reshape(shp)


def ada_modulation(cond, w, b):
    m = (jax.nn.silu(cond) @ w + b)[..., None, :]
    return jnp.split(m, 6, axis=-1)


def modulate(u, shift, scale):
    return u * (1 + scale) + shift


def mlp(u, w1, w2):
    return jnp.square(jax.nn.relu(u @ w1)) @ w2


def axial_rope_tables(n_tokens, head_dim):
    rows = n_tokens // GRID_W
    row = jnp.repeat(jnp.arange(rows, dtype=jnp.float32), GRID_W)
    col = jnp.tile(jnp.arange(GRID_W, dtype=jnp.float32), rows)
    n_freq = head_dim // 4
    inv_freq = ROPE_THETA ** (-jnp.arange(n_freq, dtype=jnp.float32) / n_freq)
    ang = jnp.concatenate([row[:, None] * inv_freq, col[:, None] * inv_freq], axis=-1)
    return jnp.cos(ang), jnp.sin(ang)


def apply_rope(x, cos, sin):
    half = x.shape[-1] // 2
    shape = (cos.shape[0],) + (1,) * (x.ndim - 3) + (half,)
    cs = cos.reshape(shape).astype(x.dtype)
    sn = sin.reshape(shape).astype(x.dtype)
    x1, x2 = x[..., :half], x[..., half:]
    return jnp.concatenate([x1 * cs - x2 * sn, x2 * cs + x1 * sn], axis=-1)


def depthwise_conv(x, w, b):
    ch = x.shape[-1]
    y = lax.conv_general_dilated(x, w[:, None, :], window_strides=(1,),
                                 padding=[(SSD_CONV // 2, SSD_CONV // 2)],
                                 dimension_numbers=('NWC', 'WIO', 'NWC'),
                                 feature_group_count=ch)
    return y + b


def segsum(a):
    t = a.shape[-1]
    a_rep = jnp.broadcast_to(a[..., :, None], a.shape + (t,))
    strict = jnp.tril(jnp.ones((t, t), dtype=bool), -1)
    seg = jnp.cumsum(jnp.where(strict, a_rep, 0.0), axis=-2)
    return jnp.where(jnp.tril(jnp.ones((t, t), dtype=bool)), seg, -jnp.inf)


def ssd_scan(x, log_a, bm, cm, h0):
    bsz, l, h, p = x.shape
    g, n = bm.shape[2], bm.shape[3]
    r = h // g
    q = SSD_CHUNK
    nc = l // q
    xc = x.reshape(bsz, nc, q, g, r, p)
    bc = bm.reshape(bsz, nc, q, g, n)
    cc = cm.reshape(bsz, nc, q, g, n)
    a = log_a.astype(jnp.float32).reshape(bsz, nc, q, g, r).transpose(0, 3, 4, 1, 2)
    a_cum = jnp.cumsum(a, axis=-1)
    decay_in = jnp.exp(segsum(a)).astype(x.dtype)
    cb = jnp.einsum('bclgn,bcsgn->bgcls', cc, bc)
    y_diag = jnp.einsum('bgcls,bgrcls,bcsgrp->bclgrp', cb, decay_in, xc)
    decay_to_end = jnp.exp(a_cum[..., -1:] - a_cum).astype(x.dtype)
    chunk_states = jnp.einsum('bcsgn,bgrcs,bcsgrp->bcgrpn', bc, decay_to_end, xc)
    states = jnp.concatenate([h0.reshape(bsz, 1, g, r, p, n).astype(x.dtype), chunk_states], axis=1)
    chunk_total = jnp.pad(a_cum[..., -1], ((0, 0), (0, 0), (0, 0), (1, 0)))
    decay_chunk = jnp.exp(segsum(chunk_total)).astype(x.dtype)
    states = jnp.einsum('bgrzc,bcgrpn->bzgrpn', decay_chunk, states)
    prev_states, final = states[:, :-1], states[:, -1]
    decay_out = jnp.exp(a_cum).astype(x.dtype)
    y_off = jnp.einsum('bclgn,bcgrpn,bgrcl->bclgrp', cc, prev_states, decay_out)
    y = (y_diag + y_off).reshape(bsz, l, h, p)
    return y, final.reshape(bsz, h, p, n)


def ssd_mixer(u_ctx, u_lat, w_in, conv_w, conv_b, dt_bias, a_log, d_skip, norm_g, w_out, need_ctx_out):
    a = -jnp.exp(a_log)

    def project(u):
        bsz, l = u.shape[:2]
        z, xbc, dt = jnp.split(u @ w_in, [SSD_D_INNER, SSD_D_INNER + SSD_CONV_CH], axis=-1)
        xbc = jax.nn.silu(depthwise_conv(xbc, conv_w, conv_b))
        xs, bm, cm = jnp.split(xbc, [SSD_D_INNER, SSD_D_INNER + SSD_GROUPS * SSD_STATE], axis=-1)
        xs = xs.reshape(bsz, l, SSD_HEADS, SSD_HEAD_DIM)
        bm = bm.reshape(bsz, l, SSD_GROUPS, SSD_STATE)
        cm = cm.reshape(bsz, l, SSD_GROUPS, SSD_STATE)
        dt = jax.nn.softplus(dt.reshape(bsz, l, 2, SSD_HEADS) + dt_bias)
        return z, xs, bm, cm, dt

    def flip(t):
        return jnp.flip(t, axis=1)

    def bidirectional(z, xs, bm, cm, dt, h0_f, h0_b):
        y_f, hf = ssd_scan(xs * dt[:, :, 0, :, None], dt[:, :, 0] * a[0], bm, cm, h0_f)
        y_b, hb = ssd_scan(flip(xs * dt[:, :, 1, :, None]), flip(dt[:, :, 1] * a[1]), flip(bm), flip(cm), h0_b)
        y = y_f + flip(y_b) + xs * d_skip[:, None]
        y = y.reshape(z.shape) * jax.nn.silu(z)
        return grouped_rms_norm(y, norm_g, SSD_GROUPS), hf, hb

    zc, xc, bc, cc, dtc = project(u_ctx)
    h0 = jnp.zeros((u_ctx.shape[0], SSD_HEADS, SSD_HEAD_DIM, SSD_STATE), xc.dtype)
    y_ctx, hf, hb = bidirectional(zc, xc, bc, cc, dtc, h0, h0)
    zl, xl, bl, cl, dtl = project(u_lat)
    y_lat, _, _ = bidirectional(zl, xl, bl, cl, dtl, hf, hb)
    out_ctx = y_ctx @ w_out if need_ctx_out else None
    return out_ctx, y_lat @ w_out


def window_gqa_mixer(u_ctx, u_lat, cos, sin, w_qkv, sink, w_out, need_ctx_out):
    scale = GQA_HEAD_DIM ** -0.5
    split = [GQA_Q_HEADS * GQA_HEAD_DIM, (GQA_Q_HEADS + GQA_KV_HEADS) * GQA_HEAD_DIM]

    def project(u):
        bsz, l = u.shape[:2]
        q, k, v = jnp.split(u @ w_qkv, split, axis=-1)
        return (q.reshape(bsz, l, GQA_KV_HEADS, GQA_REP, GQA_HEAD_DIM),
                k.reshape(bsz, l, GQA_KV_HEADS, GQA_HEAD_DIM),
                v.reshape(bsz, l, GQA_KV_HEADS, GQA_HEAD_DIM))

    qc, kc, vc = project(u_ctx)
    ql, kl, vl = project(u_lat)
    ql, kl = apply_rope(ql, cos, sin), apply_rope(kl, cos, sin)
    sink_hr = sink.reshape(GQA_KV_HEADS, GQA_REP).astype(jnp.float32)
    bsz, s = u_lat.shape[:2]
    n_ctx = u_ctx.shape[1]
    nb = s // BLOCK
    qb = ql.reshape(bsz, nb, BLOCK, GQA_KV_HEADS, GQA_REP, GQA_HEAD_DIM)

    def neighbours(t):
        tp = jnp.pad(t, ((0, 0), (BLOCK, BLOCK), (0, 0), (0, 0))).reshape(bsz, nb + 2, BLOCK, GQA_KV_HEADS, GQA_HEAD_DIM)
        return jnp.concatenate([tp[:, :-2], tp[:, 1:-1], tp[:, 2:]], axis=2)

    kw, vw = neighbours(kl), neighbours(vl)
    blk = jnp.arange(nb)[:, None, None] * BLOCK
    q_pos = blk + jnp.arange(BLOCK)[None, :, None]
    k_pos = blk + jnp.arange(3 * BLOCK)[None, None, :] - BLOCK
    valid = (jnp.abs(k_pos - q_pos) <= WINDOW) & (k_pos >= 0) & (k_pos < s)
    s_win = jnp.einsum('bnqhrd,bnkhd->bnhrqk', qb, kw).astype(jnp.float32) * scale
    s_win = jnp.where(valid[None, :, None, None], s_win, -jnp.inf)
    s_ctx = jnp.einsum('bnqhrd,bchd->bnhrqc', qb, kc).astype(jnp.float32) * scale
    s_sink = jnp.broadcast_to(sink_hr[:, :, None, None], s_win.shape[:-1] + (1,))
    p = jax.nn.softmax(jnp.concatenate([s_win, s_ctx, s_sink], axis=-1), axis=-1).astype(ql.dtype)
    o = (jnp.einsum('bnhrqk,bnkhd->bnqhrd', p[..., :3 * BLOCK], vw)
         + jnp.einsum('bnhrqc,bchd->bnqhrd', p[..., 3 * BLOCK:3 * BLOCK + n_ctx], vc))
    out_lat = o.reshape(bsz, s, GQA_Q_HEADS * GQA_HEAD_DIM) @ w_out
    out_ctx = None
    if need_ctx_out:
        sc = jnp.einsum('bqhrd,bkhd->bhrqk', qc, kc).astype(jnp.float32) * scale
        sc_sink = jnp.broadcast_to(sink_hr[:, :, None, None], sc.shape[:-1] + (1,))
        pc = jax.nn.softmax(jnp.concatenate([sc, sc_sink], axis=-1), axis=-1).astype(qc.dtype)
        oc = jnp.einsum('bhrqk,bkhd->bqhrd', pc[..., :n_ctx], vc)
        out_ctx = oc.reshape(bsz, n_ctx, GQA_Q_HEADS * GQA_HEAD_DIM) @ w_out
    return out_ctx, out_lat


def diff_attn_mixer(u_ctx, u_lat, cos, sin, w_qkv, lam_q1, lam_k1, lam_q2, lam_k2, subln_g, w_out,
                    lambda_init, need_ctx_out):
    scale = DIFF_HEAD_DIM ** -0.5

    def project(u):
        bsz, l = u.shape[:2]
        q, k, v = jnp.split(u @ w_qkv, [2 * DIFF_HEADS * DIFF_HEAD_DIM, 4 * DIFF_HEADS * DIFF_HEAD_DIM], axis=-1)
        return (q.reshape(bsz, l, DIFF_HEADS, 2, DIFF_HEAD_DIM),
                k.reshape(bsz, l, DIFF_HEADS, 2, DIFF_HEAD_DIM),
                v.reshape(bsz, l, DIFF_HEADS, DIFF_V_DIM))

    lam = (jnp.exp(jnp.sum((lam_q1 * lam_k1).astype(jnp.float32)))
           - jnp.exp(jnp.sum((lam_q2 * lam_k2).astype(jnp.float32))) + lambda_init)

    def attend(qb, k, v):
        sc = jnp.einsum('bqhtd,bkhtd->bhtqk', qb, k).astype(jnp.float32) * scale
        p = jax.nn.softmax(sc, axis=-1)
        a = (p[:, :, 0] - lam * p[:, :, 1]).astype(v.dtype)
        return jnp.einsum('bhqk,bkhe->bqhe', a, v)

    def finish(o):
        o = rms_norm(o, subln_g) * (1 - lambda_init)
        return o.reshape(o.shape[0], o.shape[1], DIFF_HEADS * DIFF_V_DIM) @ w_out

    qc, kc, vc = project(u_ctx)
    ql, kl, vl = project(u_lat)
    ql, kl = apply_rope(ql, cos, sin), apply_rope(kl, cos, sin)
    bsz, s = u_lat.shape[:2]
    nb = s // Q_BLOCK
    k_all = jnp.concatenate([kc, kl], axis=1)
    v_all = jnp.concatenate([vc, vl], axis=1)
    q_blocks = ql.reshape(bsz, nb, Q_BLOCK, DIFF_HEADS, 2, DIFF_HEAD_DIM).transpose(1, 0, 2, 3, 4, 5)
    o = lax.map(lambda qb: attend(qb, k_all, v_all), q_blocks)
    o = o.transpose(1, 0, 2, 3, 4).reshape(bsz, s, DIFF_HEADS, DIFF_V_DIM)
    out_lat = finish(o)
    out_ctx = finish(attend(qc, kc, vc)) if need_ctx_out else None
    return out_ctx, out_lat


def _n_of_kind(kind):
    return len(range(kind, DEPTH, N_MIXERS))


def setup_inputs(seed: int = 0) -> dict:
    key = jax.random.key(seed)
    keys = iter(jax.random.split(key, 48))

    def normal(shape, scale):
        return jax.random.normal(next(keys), shape, jnp.float32) * scale

    def gain(shape):
        return 1.0 + normal(shape, 0.02)

    n_a, n_b, n_c = _n_of_kind(0), _n_of_kind(1), _n_of_kind(2)
    dt0 = jnp.exp(jax.random.uniform(next(keys), (n_a, 2, SSD_HEADS), jnp.float32,
                                     minval=math.log(1e-3), maxval=math.log(1e-1)))
    return {
        'x': normal((BATCH, SEQ, D_MODEL), 1.0),
        'c': normal((BATCH, D_MODEL), 1.0),
        'ctx': normal((BATCH, CTX_LEN, D_MODEL), 1.0),
        'c_ctx': normal((D_MODEL,), 1.0),
        'ada_w': normal((DEPTH, D_MODEL, 6 * D_MODEL), 0.5 * D_MODEL ** -0.5),
        'ada_b': normal((DEPTH, 6 * D_MODEL), 0.02),
        'norm_mix_g': gain((DEPTH, D_MODEL)),
        'norm_mlp_g': gain((DEPTH, D_MODEL)),
        'mlp_w1': normal((DEPTH, D_MODEL, MLP_HIDDEN), D_MODEL ** -0.5),
        'mlp_w2': normal((DEPTH, MLP_HIDDEN, D_MODEL), MLP_HIDDEN ** -0.5),
        'ssd_w_in': normal((n_a, D_MODEL, SSD_IN), D_MODEL ** -0.5),
        'ssd_conv_w': normal((n_a, SSD_CONV, SSD_CONV_CH), SSD_CONV ** -0.5),
        'ssd_conv_b': normal((n_a, SSD_CONV_CH), 0.02),
        'ssd_dt_bias': dt0 + jnp.log(-jnp.expm1(-dt0)),
        'ssd_a_log': jnp.log(jax.random.uniform(next(keys), (n_a, 2, SSD_HEADS), jnp.float32, minval=1.0, maxval=16.0)),
        'ssd_d': 1.0 + normal((n_a, SSD_HEADS), 0.1),
        'ssd_norm_g': gain((n_a, SSD_D_INNER)),
        'ssd_w_out': normal((n_a, SSD_D_INNER, D_MODEL), SSD_D_INNER ** -0.5),
        'gqa_w_qkv': normal((n_b, D_MODEL, (GQA_Q_HEADS + 2 * GQA_KV_HEADS) * GQA_HEAD_DIM), D_MODEL ** -0.5),
        'gqa_sink': normal((n_b, GQA_Q_HEADS), 0.5),
        'gqa_w_out': normal((n_b, GQA_Q_HEADS * GQA_HEAD_DIM, D_MODEL), (GQA_Q_HEADS * GQA_HEAD_DIM) ** -0.5),
        'diff_w_qkv': normal((n_c, D_MODEL, 4 * DIFF_HEADS * DIFF_HEAD_DIM + DIFF_HEADS * DIFF_V_DIM), D_MODEL ** -0.5),
        'diff_lam_q1': normal((n_c, DIFF_HEAD_DIM), 0.1),
        'diff_lam_k1': normal((n_c, DIFF_HEAD_DIM), 0.1),
        'diff_lam_q2': normal((n_c, DIFF_HEAD_DIM), 0.1),
        'diff_lam_k2': normal((n_c, DIFF_HEAD_DIM), 0.1),
        'diff_subln_g': gain((n_c, DIFF_V_DIM)),
        'diff_w_out': normal((n_c, DIFF_HEADS * DIFF_V_DIM, D_MODEL), (DIFF_HEADS * DIFF_V_DIM) ** -0.5),
        'final_norm_g': gain((D_MODEL,)),
    }


def reference(x, c, ctx, c_ctx, ada_w, ada_b, norm_mix_g, norm_mlp_g, mlp_w1, mlp_w2,
              ssd_w_in, ssd_conv_w, ssd_conv_b, ssd_dt_bias, ssd_a_log, ssd_d, ssd_norm_g, ssd_w_out,
              gqa_w_qkv, gqa_sink, gqa_w_out,
              diff_w_qkv, diff_lam_q1, diff_lam_k1, diff_lam_q2, diff_lam_k2, diff_subln_g, diff_w_out,
              final_norm_g):
    cos, sin = axial_rope_tables(x.shape[1], GQA_HEAD_DIM)
    h, hc = x, ctx
    for i in range(DEPTH):
        kind, j = i % N_MIXERS, i // N_MIXERS
        need_ctx = i < DEPTH - 1
        sh_a, sc_a, g_a, sh_m, sc_m, g_m = ada_modulation(c, ada_w[i], ada_b[i])
        csh_a, csc_a, cg_a, csh_m, csc_m, cg_m = ada_modulation(c_ctx, ada_w[i], ada_b[i])
        u_lat = modulate(rms_norm(h, norm_mix_g[i]), sh_a, sc_a)
        u_ctx = modulate(rms_norm(hc, norm_mix_g[i]), csh_a, csc_a)
        if kind == 0:
            o_ctx, o_lat = ssd_mixer(u_ctx, u_lat, ssd_w_in[j], ssd_conv_w[j], ssd_conv_b[j], ssd_dt_bias[j],
                                     ssd_a_log[j], ssd_d[j], ssd_norm_g[j], ssd_w_out[j], need_ctx)
        elif kind == 1:
            o_ctx, o_lat = window_gqa_mixer(u_ctx, u_lat, cos, sin, gqa_w_qkv[j], gqa_sink[j], gqa_w_out[j], need_ctx)
        else:
            lambda_init = 0.8 - 0.6 * math.exp(-0.3 * i)
            o_ctx, o_lat = diff_attn_mixer(u_ctx, u_lat, cos, sin, diff_w_qkv[j], diff_lam_q1[j], diff_lam_k1[j],
                                           diff_lam_q2[j], diff_lam_k2[j], diff_subln_g[j], diff_w_out[j],
                                           lambda_init, need_ctx)
        h = h + g_a * o_lat
        h = h + g_m * mlp(modulate(rms_norm(h, norm_mlp_g[i]), sh_m, sc_m), mlp_w1[i], mlp_w2[i])
        if need_ctx:
            hc = hc + cg_a * o_ctx
            hc = hc + cg_m * mlp(modulate(rms_norm(hc, norm_mlp_g[i]), csh_m, csc_m), mlp_w1[i], mlp_w2[i])
    return rms_norm(h, final_norm_g)
```

```python
import functools
import math

import jax
import jax.numpy as jnp
from jax import lax
from jax.experimental import pallas as pl
from jax.experimental.pallas import tpu as pltpu

F32 = jnp.float32
BF16 = jnp.bfloat16

EPS = 1e-6
GRID_W = 64
ROPE_THETA = 10000.0
N_MIXERS = 3

SSD_HEAD_DIM = 64
SSD_GROUPS = 8
SSD_STATE = 128
SSD_CONV = 5
SSD_CHUNK = 128

GQA_HEAD_DIM = 128
GQA_KV_HEADS = 4
WINDOW = 128

DIFF_HEAD_DIM = 128

LANES = 128
SUBLANES = 8
VMEM_CAP = 56 * 1024 * 1024


def _cparams(sem, vmem_mb):
    return pltpu.CompilerParams(dimension_semantics=sem,
                                vmem_limit_bytes=min(int(vmem_mb * 1024 * 1024), VMEM_CAP))


def _nt_dot(a, b):
    return lax.dot_general(a, b, (((1,), (1,)), ((), ())), preferred_element_type=F32)


def _pick_tile(n, cap):
    best = LANES
    t = LANES
    while t <= cap:
        if n % t == 0:
            best = t
        t += LANES
    return best


def _norm_mod(x, g, shift, scale):
    ms = jnp.mean(x * x, axis=-1, keepdims=True)
    y = x * lax.rsqrt(ms + EPS) * g
    return y * (1.0 + scale) + shift


def _ada_kernel(cond_ref, w_ref, b_ref, o_ref):
    a = cond_ref[...]
    s = a / (1.0 + jnp.exp(-a))
    o_ref[0] = jnp.dot(s.astype(BF16), w_ref[0].astype(BF16), preferred_element_type=F32) + b_ref[0]


def _ada_all(cond8, ada_w, ada_b):
    depth, d, n = ada_w.shape
    tn = _pick_tile(n, 1024)
    return pl.pallas_call(
        _ada_kernel,
        grid=(depth, n // tn),
        in_specs=[pl.BlockSpec((SUBLANES, d), lambda l, j: (0, 0)),
                  pl.BlockSpec((1, d, tn), lambda l, j: (l, 0, j)),
                  pl.BlockSpec((1, 1, tn), lambda l, j: (l, 0, j))],
        out_specs=pl.BlockSpec((1, SUBLANES, tn), lambda l, j: (l, 0, j)),
        out_shape=jax.ShapeDtypeStruct((depth, SUBLANES, n), F32),
        compiler_params=_cparams(("arbitrary", "arbitrary"), 40),
        name="ada_mod",
    )(cond8, ada_w, ada_b.reshape(depth, 1, n))


def _proj_kernel(h_ref, g_ref, mod_ref, w_ref, o_ref, u_sc, *, shift_idx, scale_idx):
    @pl.when(pl.program_id(1) == 0)
    def _():
        m = mod_ref[0]
        u = _norm_mod(h_ref[...], g_ref[...], m[shift_idx:shift_idx + 1], m[scale_idx:scale_idx + 1])
        u_sc[...] = u.astype(BF16)

    o_ref[...] = jnp.dot(u_sc[...], w_ref[...], preferred_element_type=F32).astype(o_ref.dtype)


def _proj(h, g, mods, w, seq, nbatch, out_dtype, tm=512):
    t, d = h.shape
    n = w.shape[1]
    tn = _pick_tile(n, 1536)
    return pl.pallas_call(
        functools.partial(_proj_kernel, shift_idx=0, scale_idx=1),
        grid=(t // tm, n // tn),
        in_specs=[pl.BlockSpec((tm, d), lambda i, j: (i, 0)),
                  pl.BlockSpec((1, d), lambda i, j: (0, 0)),
                  pl.BlockSpec((1, 6, d), lambda i, j: (jnp.minimum((i * tm) // seq, nbatch), 0, 0)),
                  pl.BlockSpec((d, tn), lambda i, j: (0, j))],
        out_specs=pl.BlockSpec((tm, tn), lambda i, j: (i, j)),
        out_shape=jax.ShapeDtypeStruct((t, n), out_dtype),
        scratch_shapes=[pltpu.VMEM((tm, d), BF16)],
        compiler_params=_cparams(("arbitrary", "arbitrary"), 48),
        name="norm_mod_proj",
    )(h, g.reshape(1, d), mods, w)


def _mlp_kernel(h_ref, g_ref, mod_ref, w1_ref, w2_ref, fg_ref, o_ref, u_sc, acc_sc, *, nk, final_norm):
    k = pl.program_id(1)

    @pl.when(k == 0)
    def _():
        m = mod_ref[0]
        u = _norm_mod(h_ref[...], g_ref[...], m[3:4], m[4:5])
        u_sc[...] = u.astype(BF16)
        acc_sc[...] = jnp.zeros_like(acc_sc)

    hk = jnp.dot(u_sc[...], w1_ref[...], preferred_element_type=F32)
    hk = jnp.square(jnp.maximum(hk, 0.0)).astype(BF16)
    acc_sc[...] += jnp.dot(hk, w2_ref[...], preferred_element_type=F32)

    @pl.when(k == nk - 1)
    def _():
        m = mod_ref[0]
        out = h_ref[...] + m[5:6] * acc_sc[...]
        if final_norm:
            ms = jnp.mean(out * out, axis=-1, keepdims=True)
            out = out * lax.rsqrt(ms + EPS) * fg_ref[...]
        o_ref[...] = out


def _mlp(h, g, mods, w1, w2, final_g, rows, seq, nbatch, final_norm, tm=512, tk=512):
    d = h.shape[1]
    hid = w1.shape[1]
    nk = hid // tk
    return pl.pallas_call(
        functools.partial(_mlp_kernel, nk=nk, final_norm=final_norm),
        grid=(rows // tm, nk),
        in_specs=[pl.BlockSpec((tm, d), lambda i, k: (i, 0)),
                  pl.BlockSpec((1, d), lambda i, k: (0, 0)),
                  pl.BlockSpec((1, 6, d), lambda i, k: (jnp.minimum((i * tm) // seq, nbatch), 0, 0)),
                  pl.BlockSpec((d, tk), lambda i, k: (0, k)),
                  pl.BlockSpec((tk, d), lambda i, k: (k, 0)),
                  pl.BlockSpec((1, d), lambda i, k: (0, 0))],
        out_specs=pl.BlockSpec((tm, d), lambda i, k: (i, 0)),
        out_shape=jax.ShapeDtypeStruct((rows, d), F32),
        scratch_shapes=[pltpu.VMEM((tm, d), BF16), pltpu.VMEM((tm, d), F32)],
        compiler_params=_cparams(("arbitrary", "arbitrary"), 48),
        name="mlp",
    )(h, g.reshape(1, d), mods, w1, w2, final_g.reshape(1, d))


def _outproj_kernel(y_ref, w_ref, h_ref, mod_ref, o_ref):
    acc = jnp.dot(y_ref[...], w_ref[...], preferred_element_type=F32)
    o_ref[...] = h_ref[...] + mod_ref[0][2:3] * acc


def _outproj(y, w, h, mods, rows, seq, nbatch, tm=512, tn=512):
    kdim = y.shape[1]
    d = w.shape[1]
    return pl.pallas_call(
        _outproj_kernel,
        grid=(rows // tm, d // tn),
        in_specs=[pl.BlockSpec((tm, kdim), lambda i, j: (i, 0)),
                  pl.BlockSpec((kdim, tn), lambda i, j: (0, j)),
                  pl.BlockSpec((tm, tn), lambda i, j: (i, j)),
                  pl.BlockSpec((1, 6, tn), lambda i, j: (jnp.minimum((i * tm) // seq, nbatch), 0, j))],
        out_specs=pl.BlockSpec((tm, tn), lambda i, j: (i, j)),
        out_shape=jax.ShapeDtypeStruct((rows, d), F32),
        compiler_params=_cparams(("arbitrary", "arbitrary"), 40),
        name="out_proj",
    )(y, w, h, mods)


def _rope_tables(n_tokens, head_dim):
    rows = n_tokens // GRID_W
    row = jnp.repeat(jnp.arange(rows, dtype=F32), GRID_W)
    col = jnp.tile(jnp.arange(GRID_W, dtype=F32), rows)
    n_freq = head_dim // 4
    inv_freq = ROPE_THETA ** (-jnp.arange(n_freq, dtype=F32) / n_freq)
    ang = jnp.concatenate([row[:, None] * inv_freq, col[:, None] * inv_freq], axis=-1)
    cos, sin = jnp.cos(ang), jnp.sin(ang)
    return jnp.concatenate([cos, cos], axis=-1), jnp.concatenate([-sin, sin], axis=-1)


def _rope(x, cos2, sin_s):
    return x * cos2 + pltpu.roll(x, x.shape[-1] // 2, 1) * sin_s


def _gqa_attend(qs, keys, vals, masks, sink_col, scale):
    ss = []
    for k_, m_ in zip(keys, masks):
        s = _nt_dot(qs, k_) * scale
        if m_ is not None:
            s = jnp.where(m_, s, -jnp.inf)
        ss.append(s)
    m = sink_col
    for s in ss:
        m = jnp.maximum(m, jnp.max(s, axis=-1, keepdims=True))
    l = jnp.exp(sink_col - m)
    acc = None
    for s, v_ in zip(ss, vals):
        p = jnp.exp(s - m)
        l = l + jnp.sum(p, axis=-1, keepdims=True)
        pv = jnp.dot(p.astype(BF16), v_, preferred_element_type=F32)
        acc = pv if acc is None else acc + pv
    return acc / l


def _gqa_kernel(sink_ref, q_ref, kp_ref, ko_ref, kn_ref, vp_ref, vo_ref, vn_ref, kc_ref, vc_ref,
                cos_ref, sin_ref, o_ref, *, nb, nctxb, rep):
    t = pl.program_id(1)
    hd = GQA_HEAD_DIM
    blk = WINDOW
    scale = hd ** -0.5

    def sink_col(h):
        return jnp.concatenate([jnp.full((blk, 1), sink_ref[h * rep + r], F32) for r in range(rep)], axis=0)

    def write(h, o):
        for r in range(rep):
            c0 = (h * rep + r) * hd
            o_ref[:, c0:c0 + hd] = o[r * blk:(r + 1) * blk].astype(o_ref.dtype)

    @pl.when(t < nctxb)
    def _ctx():
        for h in range(GQA_KV_HEADS):
            qs = jnp.concatenate([q_ref[:, (h * rep + r) * hd:(h * rep + r + 1) * hd] for r in range(rep)], axis=0)
            kc = kc_ref[:, h * hd:(h + 1) * hd].astype(BF16)
            vc = vc_ref[:, h * hd:(h + 1) * hd].astype(BF16)
            write(h, _gqa_attend(qs.astype(BF16), [kc], [vc], [None], sink_col(h), scale))

    @pl.when(t >= nctxb)
    def _lat():
        n = t - nctxb
        pq = pl.multiple_of(n * blk, blk)
        pp = pl.multiple_of(jnp.maximum(n - 1, 0) * blk, blk)
        pn = pl.multiple_of(jnp.minimum(n + 1, nb - 1) * blk, blk)
        cq, sq = cos_ref[pl.ds(pq, blk), :], sin_ref[pl.ds(pq, blk), :]
        cp, sp = cos_ref[pl.ds(pp, blk), :], sin_ref[pl.ds(pp, blk), :]
        cn, sn = cos_ref[pl.ds(pn, blk), :], sin_ref[pl.ds(pn, blk), :]
        i = lax.broadcasted_iota(jnp.int32, (rep * blk, 3 * blk), 0) & (blk - 1)
        j = lax.broadcasted_iota(jnp.int32, (rep * blk, 3 * blk), 1)
        rel = j - blk - i
        valid = (rel <= WINDOW) & (rel >= -WINDOW)
        valid = valid & ((j >= blk) | (n >= 1)) & ((j < 2 * blk) | (n <= nb - 2))
        for h in range(GQA_KV_HEADS):
            qs = jnp.concatenate(
                [_rope(q_ref[:, (h * rep + r) * hd:(h * rep + r + 1) * hd], cq, sq) for r in range(rep)], axis=0)
            sl = slice(h * hd, (h + 1) * hd)
            kw = jnp.concatenate([_rope(kp_ref[:, sl], cp, sp), _rope(ko_ref[:, sl], cq, sq),
                                  _rope(kn_ref[:, sl], cn, sn)], axis=0).astype(BF16)
            vw = jnp.concatenate([vp_ref[:, sl], vo_ref[:, sl], vn_ref[:, sl]], axis=0).astype(BF16)
            kc = kc_ref[:, sl].astype(BF16)
            vc = vc_ref[:, sl].astype(BF16)
            write(h, _gqa_attend(qs.astype(BF16), [kw, kc], [vw, vc], [valid, None], sink_col(h), scale))


def _gqa(p, sink, cos2, sin_s, nbatch, seq, nctx):
    t = p.shape[0]
    hd = GQA_HEAD_DIM
    nkv = GQA_KV_HEADS
    dq = p.shape[1] - 2 * nkv * hd
    rep = dq // hd // nkv
    blk = WINDOW
    nb = seq // blk
    nctxb = nctx // blk
    latb = nbatch * seq // blk
    kvw = nkv * hd
    kcol = dq // kvw
    vcol = kcol + 1

    def qrow(b, s):
        return jnp.where(s < nctxb, latb + b * nctxb + s, b * nb + s - nctxb)

    def nidx(s):
        return jnp.maximum(s - nctxb, 0)

    def kvspec(off, col):
        return pl.BlockSpec((blk, kvw), lambda b, s: (b * nb + jnp.clip(nidx(s) + off, 0, nb - 1), col))

    ctx_blk = nbatch * seq // nctx
    return pl.pallas_call(
        functools.partial(_gqa_kernel, nb=nb, nctxb=nctxb, rep=rep),
        grid=(nbatch, nctxb + nb),
        in_specs=[pl.BlockSpec(memory_space=pltpu.SMEM),
                  pl.BlockSpec((blk, dq), lambda b, s: (qrow(b, s), 0)),
                  kvspec(-1, kcol), kvspec(0, kcol), kvspec(1, kcol),
                  kvspec(-1, vcol), kvspec(0, vcol), kvspec(1, vcol),
                  pl.BlockSpec((nctx, kvw), lambda b, s: (ctx_blk + b, kcol)),
                  pl.BlockSpec((nctx, kvw), lambda b, s: (ctx_blk + b, vcol)),
                  pl.BlockSpec((seq, hd), lambda b, s: (0, 0)),
                  pl.BlockSpec((seq, hd), lambda b, s: (0, 0))],
        out_specs=pl.BlockSpec((blk, dq), lambda b, s: (qrow(b, s), 0)),
        out_shape=jax.ShapeDtypeStruct((t, dq), BF16),
        compiler_params=_cparams(("arbitrary", "arbitrary"), 40),
        name="gqa_window_attn",
    )(sink, p, p, p, p, p, p, p, p, p, cos2, sin_s)


def _diff_kernel(lq1_ref, lk1_ref, lq2_ref, lk2_ref, g_ref, q_ref, kl_ref, kc_ref, vl_ref, vc_ref,
                 cos_ref, sin_ref, o_ref, k_sc, v_sc, s_sc, *, nctx, seq, tq, kchunk, lambda_init):
    t = pl.program_id(2)
    hd = DIFF_HEAD_DIM
    scale = hd ** -0.5

    @pl.when(t == 0)
    def _prep():
        k_sc[0:nctx, :] = kc_ref[...].astype(BF16)
        v_sc[0:nctx, :] = vc_ref[...].astype(BF16)
        for r0 in range(0, seq, kchunk):
            c, s = cos_ref[r0:r0 + kchunk, :], sin_ref[r0:r0 + kchunk, :]
            for tt in range(2):
                k_sc[nctx + r0:nctx + r0 + kchunk, tt * hd:(tt + 1) * hd] = _rope(
                    kl_ref[r0:r0 + kchunk, tt * hd:(tt + 1) * hd], c, s).astype(BF16)
            v_sc[nctx + r0:nctx + r0 + kchunk, :] = vl_ref[r0:r0 + kchunk, :].astype(BF16)

    lam = (jnp.exp(jnp.sum(lq1_ref[...] * lk1_ref[...], axis=-1, keepdims=True))
           - jnp.exp(jnp.sum(lq2_ref[...] * lk2_ref[...], axis=-1, keepdims=True)) + lambda_init)

    def run(qs, chunks):
        outs = []
        for tt in range(2):
            q = qs[tt].astype(BF16)
            m = jnp.full((tq, 1), -jnp.inf, F32)
            for (c0, cs) in chunks:
                s = _nt_dot(q, k_sc[c0:c0 + cs, tt * hd:(tt + 1) * hd]) * scale
                s_sc[tt, :, c0:c0 + cs] = s
                m = jnp.maximum(m, jnp.max(s, axis=-1, keepdims=True))
            l = jnp.zeros((tq, 1), F32)
            acc = jnp.zeros((tq, 2 * hd), F32)
            for (c0, cs) in chunks:
                e = jnp.exp(s_sc[tt, :, c0:c0 + cs] - m)
                l = l + jnp.sum(e, axis=-1, keepdims=True)
                acc = acc + jnp.dot(e.astype(BF16), v_sc[c0:c0 + cs, :], preferred_element_type=F32)
            outs.append(acc / l)
        o = outs[0] - lam * outs[1]
        ms = jnp.mean(o * o, axis=-1, keepdims=True)
        o = o * lax.rsqrt(ms + EPS) * g_ref[...] * (1.0 - lambda_init)
        o_ref[...] = o.astype(o_ref.dtype)

    ctx_chunks = [(c0, min(kchunk, nctx - c0)) for c0 in range(0, nctx, kchunk)]
    lat_chunks = [(nctx + c0, kchunk) for c0 in range(0, seq, kchunk)]

    @pl.when(t == 0)
    def _ctx():
        run([q_ref[:, 0:hd], q_ref[:, hd:2 * hd]], ctx_chunks)

    @pl.when(t > 0)
    def _lat():
        p0 = pl.multiple_of((t - 1) * tq, tq)
        c, s = cos_ref[pl.ds(p0, tq), :], sin_ref[pl.ds(p0, tq), :]
        run([_rope(q_ref[:, 0:hd], c, s), _rope(q_ref[:, hd:2 * hd], c, s)], ctx_chunks + lat_chunks)


def _diff_attn(p, lq1, lk1, lq2, lk2, subln_g, cos2, sin_s, nbatch, seq, nctx, lambda_init):
    t = p.shape[0]
    hd = DIFF_HEAD_DIM
    vd = 2 * hd
    nh = p.shape[1] // (3 * vd)
    tq = nctx
    nq = seq // tq
    ctx_blk = nbatch * seq // nctx
    kchunk = 512

    def qrow(b, s):
        return jnp.where(s == 0, ctx_blk + b, b * nq + s - 1)

    vec = lambda a: a.reshape(1, -1)
    return pl.pallas_call(
        functools.partial(_diff_kernel, nctx=nctx, seq=seq, tq=tq, kchunk=kchunk, lambda_init=lambda_init),
        grid=(nbatch, nh, nq + 1),
        in_specs=[pl.BlockSpec((1, hd), lambda b, h, s: (0, 0))] * 4
        + [pl.BlockSpec((1, vd), lambda b, h, s: (0, 0)),
           pl.BlockSpec((tq, vd), lambda b, h, s: (qrow(b, s), h)),
           pl.BlockSpec((seq, vd), lambda b, h, s: (b, nh + h)),
           pl.BlockSpec((nctx, vd), lambda b, h, s: (ctx_blk + b, nh + h)),
           pl.BlockSpec((seq, vd), lambda b, h, s: (b, 2 * nh + h)),
           pl.BlockSpec((nctx, vd), lambda b, h, s: (ctx_blk + b, 2 * nh + h)),
           pl.BlockSpec((seq, hd), lambda b, h, s: (0, 0)),
           pl.BlockSpec((seq, hd), lambda b, h, s: (0, 0))],
        out_specs=pl.BlockSpec((tq, vd), lambda b, h, s: (qrow(b, s), h)),
        out_shape=jax.ShapeDtypeStruct((t, nh * vd), BF16),
        scratch_shapes=[pltpu.VMEM((nctx + seq, vd), BF16), pltpu.VMEM((nctx + seq, vd), BF16),
                        pltpu.VMEM((2, tq, nctx + seq), F32)],
        compiler_params=_cparams(("arbitrary", "arbitrary", "arbitrary"), 56),
        name="diff_attn",
    )(vec(lq1), vec(lk1), vec(lq2), vec(lk2), vec(subln_g), p, p, p, p, p, cos2, sin_s)


def _ssd_conv_kernel(xm_ref, xp_ref, xn_ref, bcm_ref, bcp_ref, bcn_ref, dt_ref, wx_ref, wbc_ref, bx_ref,
                     bbc_ref, dtb_ref, xs_ref, bm_ref, cm_ref, dto_ref, ext_sc, *, rows, seq, nctx, nlat_rows,
                     slab):
    r = pl.program_id(0)
    row0 = r * rows
    in_lat = row0 < nlat_rows
    seg = jnp.where(in_lat, seq, nctx)
    off = jnp.where(in_lat, row0, row0 - nlat_rows)
    is_first = (off % seg) == 0
    is_last = ((off + rows) % seg) == 0
    halo = SUBLANES
    pad = SSD_CONV // 2
    ngrp = SSD_GROUPS
    nst = SSD_STATE

    def conv_slab(m_ref, p_ref, n_ref, w_ref, b_ref, c0):
        prev = jnp.where(is_first, 0.0, p_ref[:, c0:c0 + slab])
        nxt = jnp.where(is_last, 0.0, n_ref[:, c0:c0 + slab])
        ext_sc[0:halo, :] = prev
        ext_sc[halo:halo + rows, :] = m_ref[:, c0:c0 + slab]
        ext_sc[halo + rows:2 * halo + rows, :] = nxt
        acc = jnp.broadcast_to(b_ref[:, c0:c0 + slab], (rows, slab))
        for k in range(SSD_CONV):
            acc = acc + w_ref[k:k + 1, c0:c0 + slab] * ext_sc[pl.ds(halo - pad + k, rows), :]
        return acc / (1.0 + jnp.exp(-acc))

    for g in range(ngrp):
        xs_ref[g] = conv_slab(xm_ref, xp_ref, xn_ref, wx_ref, bx_ref, g * slab)
    nbc = 2 * ngrp * nst // slab
    for sidx in range(nbc):
        y = conv_slab(bcm_ref, bcp_ref, bcn_ref, wbc_ref, bbc_ref, sidx * slab)
        for q in range(slab // nst):
            gi = sidx * (slab // nst) + q
            if gi < ngrp:
                bm_ref[gi] = y[:, q * nst:(q + 1) * nst]
            else:
                cm_ref[gi - ngrp] = y[:, q * nst:(q + 1) * nst]
    v = dt_ref[...] + dtb_ref[...]
    dto_ref[...] = jnp.maximum(v, 0.0) + jnp.log1p(jnp.exp(-jnp.abs(v)))


def _ssd_conv(zx, conv_w, conv_b, dt_bias, d_inner, nbatch, seq, nctx):
    t = zx.shape[0]
    rows = 256
    slab = 512
    ngrp, nst = SSD_GROUPS, SSD_STATE
    bcw = 2 * ngrp * nst
    nh2 = dt_bias.size
    halo = SUBLANES
    rb = rows // halo
    nhalo = t // halo
    xcol = d_inner // d_inner
    bccol = 2 * d_inner // bcw
    dtcol = (2 * d_inner + bcw) // nh2
    wx, wbc = conv_w[:, :d_inner], conv_w[:, d_inner:]
    bx, bbc = conv_b[:d_inner].reshape(1, -1), conv_b[d_inner:].reshape(1, -1)
    prev = lambda r: jnp.maximum(r * rb - 1, 0)
    nxt = lambda r: jnp.minimum(r * rb + rb, nhalo - 1)
    return pl.pallas_call(
        functools.partial(_ssd_conv_kernel, rows=rows, seq=seq, nctx=nctx, nlat_rows=nbatch * seq, slab=slab),
        grid=(t // rows,),
        in_specs=[pl.BlockSpec((rows, d_inner), lambda r: (r, xcol)),
                  pl.BlockSpec((halo, d_inner), lambda r: (prev(r), xcol)),
                  pl.BlockSpec((halo, d_inner), lambda r: (nxt(r), xcol)),
                  pl.BlockSpec((rows, bcw), lambda r: (r, bccol)),
                  pl.BlockSpec((halo, bcw), lambda r: (prev(r), bccol)),
                  pl.BlockSpec((halo, bcw), lambda r: (nxt(r), bccol)),
                  pl.BlockSpec((rows, nh2), lambda r: (r, dtcol)),
                  pl.BlockSpec((SSD_CONV, d_inner), lambda r: (0, 0)),
                  pl.BlockSpec((SSD_CONV, bcw), lambda r: (0, 0)),
                  pl.BlockSpec((1, d_inner), lambda r: (0, 0)),
                  pl.BlockSpec((1, bcw), lambda r: (0, 0)),
                  pl.BlockSpec((1, nh2), lambda r: (0, 0))],
        out_specs=[pl.BlockSpec((ngrp, rows, d_inner // ngrp), lambda r: (0, r, 0)),
                   pl.BlockSpec((ngrp, rows, nst), lambda r: (0, r, 0)),
                   pl.BlockSpec((ngrp, rows, nst), lambda r: (0, r, 0)),
                   pl.BlockSpec((rows, nh2), lambda r: (r, 0))],
        out_shape=[jax.ShapeDtypeStruct((ngrp, t, d_inner // ngrp), F32),
                   jax.ShapeDtypeStruct((ngrp, t, nst), F32),
                   jax.ShapeDtypeStruct((ngrp, t, nst), F32),
                   jax.ShapeDtypeStruct((t, nh2), F32)],
        scratch_shapes=[pltpu.VMEM((rows + 2 * halo, slab), F32)],
        compiler_params=_cparams(("arbitrary",), 48),
        name="ssd_conv",
    )(zx, zx, zx, zx, zx, zx, zx, wx, wbc, bx, bbc, dt_bias.reshape(1, nh2))


def _ssd_scan_kernel(alog_ref, tri_ref, xf_ref, bf_ref, cf_ref, dtf_ref, xb_ref, bb_ref, cb_ref, dtb_ref,
                     yf_ref, yb_ref, st_sc, e_sc, dec_sc, et_sc, dtt_sc, wt_sc, ext_sc, *, nheads):
    q = SSD_CHUNK
    ngrp = SSD_GROUPS
    hpg = nheads // ngrp
    p = SSD_HEAD_DIM

    @pl.when(pl.program_id(1) == 0)
    def _():
        st_sc[...] = jnp.zeros_like(st_sc)

    a = -jnp.exp(alog_ref[...])
    tri = tri_ref[...]
    es, decs, dts, ws = [], [], [], []
    for d, dt_ref in enumerate((dtf_ref, dtb_ref)):
        dt = dt_ref[:, d * nheads:(d + 1) * nheads]
        la = dt * a[:, d * nheads:(d + 1) * nheads]
        acum = jnp.dot(tri, la, preferred_element_type=F32, precision=lax.Precision.HIGHEST)
        tot = acum[q - 1:q, :]
        e = acum if d == 0 else tot - acum + la
        es.append(e)
        decs.append(jnp.exp(e))
        dts.append(dt)
        ws.append(dt * jnp.exp(tot - e))
    e2 = jnp.concatenate(es, axis=1)
    e_sc[...] = e2
    dec_sc[...] = jnp.concatenate(decs, axis=1)
    e2t = e2.T
    et_sc[...] = e2t.reshape(2 * ngrp, hpg, q)
    dtt_sc[...] = jnp.concatenate(dts, axis=1).T.reshape(2 * ngrp, hpg, q)
    wt_sc[...] = jnp.concatenate(ws, axis=1).T.reshape(2 * ngrp, hpg, q)
    tot_col = jnp.concatenate([e2t[0:nheads, q - 1:q], e2t[nheads:2 * nheads, 0:1]], axis=0)
    ext_sc[...] = jnp.broadcast_to(jnp.exp(tot_col), (2 * nheads, q)).reshape(2 * ngrp, hpg, q)

    lane = lax.broadcasted_iota(jnp.int32, (q, 2 * nheads), 1)
    li = lax.broadcasted_iota(jnp.int32, (q, q), 0)
    si = lax.broadcasted_iota(jnp.int32, (q, q), 1)
    lo = lax.broadcasted_iota(jnp.int32, (q, 2 * p), 1) < p
    causal = (si <= li, si >= li)

    def group(g, carry):
        for d, (x_ref, b_ref, c_ref, y_ref) in enumerate(((xf_ref, bf_ref, cf_ref, yf_ref),
                                                          (xb_ref, bb_ref, cb_ref, yb_ref))):
            bg = b_ref[g]
            cg = c_ref[g].astype(BF16)
            cbm = _nt_dot(cg, bg.astype(BF16))
            bgt = bg.T
            st = st_sc[d, g]
            z = jnp.dot(cg, st.astype(BF16), preferred_element_type=F32)
            et = et_sc[d * ngrp + g]
            dtt = dtt_sc[d * ngrp + g]
            wt = wt_sc[d * ngrp + g]
            ext = ext_sc[d * ngrp + g]
            e_all = e_sc[...]
            dec_all = dec_sc[...]
            for jp in range(hpg // 2):
                ms, decc, bws, exts = [], [], [], []
                for j in (2 * jp, 2 * jp + 1):
                    col = d * nheads + g * hpg + j
                    sel = lane == col
                    ecol = jnp.sum(jnp.where(sel, e_all, 0.0), axis=1, keepdims=True)
                    decc.append(jnp.sum(jnp.where(sel, dec_all, 0.0), axis=1, keepdims=True))
                    lm = jnp.exp(jnp.where(causal[d], ecol - et[j:j + 1, :], -jnp.inf))
                    ms.append((cbm * lm * dtt[j:j + 1, :]).astype(BF16))
                    bws.append((bgt * wt[j:j + 1, :]).astype(BF16))
                    exts.append(ext[j:j + 1, :])
                xp = x_ref[g, :, jp * 2 * p:(jp + 1) * 2 * p]
                rhs = jnp.concatenate([jnp.where(lo, xp, 0.0), jnp.where(lo, 0.0, xp)], axis=0).astype(BF16)
                y = jnp.dot(jnp.concatenate(ms, axis=1), rhs, preferred_element_type=F32)
                y = y + jnp.where(lo, decc[0], decc[1]) * z[:, jp * 2 * p:(jp + 1) * 2 * p]
                y_ref[g, :, jp * 2 * p:(jp + 1) * 2 * p] = y
                contrib = jnp.dot(jnp.concatenate(bws, axis=1), rhs, preferred_element_type=F32)
                decay = jnp.where(lo[0:1, :], exts[0], exts[1])
                st_sc[d, g, :, jp * 2 * p:(jp + 1) * 2 * p] = st[:, jp * 2 * p:(jp + 1) * 2 * p] * decay + contrib
        return carry

    lax.fori_loop(0, ngrp, group, 0)


def _ssd_scan(xs_g, bm_g, cm_g, dt, a_log, nbatch, seq, nctx):
    ngrp, t, gw = xs_g.shape
    nst = bm_g.shape[2]
    q = SSD_CHUNK
    nheads = a_log.shape[1]
    hpg = nheads // ngrp
    nlat = seq // q
    nctxc = nctx // q
    latc = nbatch * nlat

    def fidx(b, s):
        return jnp.where(s < nctxc, latc + b * nctxc + s, b * nlat + s - nctxc)

    def bidx(b, s):
        return jnp.where(s < nctxc, latc + b * nctxc + nctxc - 1 - s, b * nlat + nlat - 1 - (s - nctxc))

    def specs(idx):
        return [pl.BlockSpec((ngrp, q, gw), lambda b, s: (0, idx(b, s), 0)),
                pl.BlockSpec((ngrp, q, nst), lambda b, s: (0, idx(b, s), 0)),
                pl.BlockSpec((ngrp, q, nst), lambda b, s: (0, idx(b, s), 0)),
                pl.BlockSpec((q, 2 * nheads), lambda b, s: (idx(b, s), 0))]

    tri = jnp.tril(jnp.ones((q, q), F32))
    yshape = jax.ShapeDtypeStruct((ngrp, t, gw), F32)
    small = pltpu.VMEM((2 * ngrp, hpg, q), F32)
    return pl.pallas_call(
        functools.partial(_ssd_scan_kernel, nheads=nheads),
        grid=(nbatch, nctxc + nlat),
        in_specs=[pl.BlockSpec((1, 2 * nheads), lambda b, s: (0, 0)),
                  pl.BlockSpec((q, q), lambda b, s: (0, 0))] + specs(fidx) + specs(bidx),
        out_specs=[pl.BlockSpec((ngrp, q, gw), lambda b, s: (0, fidx(b, s), 0)),
                   pl.BlockSpec((ngrp, q, gw), lambda b, s: (0, bidx(b, s), 0))],
        out_shape=[yshape, yshape],
        scratch_shapes=[pltpu.VMEM((2, ngrp, nst, gw), F32),
                        pltpu.VMEM((q, 2 * nheads), F32), pltpu.VMEM((q, 2 * nheads), F32),
                        small, small, small, small],
        compiler_params=_cparams(("arbitrary", "arbitrary"), 48),
        name="ssd_scan",
    )(a_log.reshape(1, 2 * nheads), tri, xs_g, bm_g, cm_g, dt, xs_g, bm_g, cm_g, dt)


def _ssd_out_kernel(yf_ref, yb_ref, xs_ref, z_ref, dx_ref, ng_ref, w_ref, h_ref, mod_ref, o_ref, y_sc):
    ngrp = SSD_GROUPS
    gw = y_sc.shape[1] // ngrp

    @pl.when(pl.program_id(1) == 0)
    def _():
        for g in range(ngrp):
            sl = slice(g * gw, (g + 1) * gw)
            z = z_ref[:, sl]
            y = (yf_ref[g] + yb_ref[g] + xs_ref[g] * dx_ref[:, sl]) * (z / (1.0 + jnp.exp(-z)))
            ms = jnp.mean(y * y, axis=-1, keepdims=True)
            y_sc[:, sl] = (y * lax.rsqrt(ms + EPS) * ng_ref[:, sl]).astype(BF16)

    acc = jnp.dot(y_sc[...], w_ref[...], preferred_element_type=F32)
    o_ref[...] = h_ref[...] + mod_ref[0][2:3] * acc


def _ssd_out(yf, yb, xs_g, zx, d_skip, norm_g, w_out, h, mods, rows, seq, nbatch, tm=256, tn=512):
    ngrp, _, gw = xs_g.shape
    d_inner = ngrp * gw
    d = w_out.shape[1]
    dx = jnp.repeat(d_skip, SSD_HEAD_DIM).reshape(1, d_inner)
    gspec = pl.BlockSpec((ngrp, tm, gw), lambda i, j: (0, i, 0))
    return pl.pallas_call(
        _ssd_out_kernel,
        grid=(rows // tm, d // tn),
        in_specs=[gspec, gspec, gspec,
                  pl.BlockSpec((tm, d_inner), lambda i, j: (i, 0)),
                  pl.BlockSpec((1, d_inner), lambda i, j: (0, 0)),
                  pl.BlockSpec((1, d_inner), lambda i, j: (0, 0)),
                  pl.BlockSpec((d_inner, tn), lambda i, j: (0, j)),
                  pl.BlockSpec((tm, tn), lambda i, j: (i, j)),
                  pl.BlockSpec((1, 6, tn), lambda i, j: (jnp.minimum((i * tm) // seq, nbatch), 0, j))],
        out_specs=pl.BlockSpec((tm, tn), lambda i, j: (i, j)),
        out_shape=jax.ShapeDtypeStruct((rows, d), F32),
        scratch_shapes=[pltpu.VMEM((tm, d_inner), BF16)],
        compiler_params=_cparams(("arbitrary", "arbitrary"), 56),
        name="ssd_gate_norm_out_proj",
    )(yf, yb, xs_g, zx, dx, norm_g.reshape(1, d_inner), w_out, h, mods)


def kernel(x, c, ctx, c_ctx, ada_w, ada_b, norm_mix_g, norm_mlp_g, mlp_w1, mlp_w2, ssd_w_in, ssd_conv_w, ssd_conv_b, ssd_dt_bias, ssd_a_log, ssd_d, ssd_norm_g, ssd_w_out, gqa_w_qkv, gqa_sink, gqa_w_out, diff_w_qkv, diff_lam_q1, diff_lam_k1, diff_lam_q2, diff_lam_k2, diff_subln_g, diff_w_out, final_norm_g):
    nbatch, seq, d = x.shape
    nctx = ctx.shape[1]
    depth = ada_w.shape[0]
    nlat_rows = nbatch * seq
    t = nlat_rows + nbatch * nctx

    cond8 = jnp.zeros((SUBLANES, d), F32).at[:nbatch].set(c).at[nbatch].set(c_ctx)
    mods_all = _ada_all(cond8, ada_w, ada_b)[:, :nbatch + 1].reshape(depth, nbatch + 1, 6, d)
    cos2, sin_s = _rope_tables(seq, GQA_HEAD_DIM)

    h = jnp.concatenate([x.reshape(nlat_rows, d), ctx.reshape(nbatch * nctx, d)], axis=0)
    for i in range(depth):
        kind, j = i % N_MIXERS, i // N_MIXERS
        need_ctx = i < depth - 1
        rows = t if need_ctx else nlat_rows
        mods = mods_all[i]
        if kind == 0:
            d_inner = ssd_w_out.shape[1]
            zx = _proj(h, norm_mix_g[i], mods, ssd_w_in[j].astype(BF16), seq, nbatch, F32)
            xs_g, bm_g, cm_g, dt = _ssd_conv(zx, ssd_conv_w[j], ssd_conv_b[j], ssd_dt_bias[j], d_inner,
                                             nbatch, seq, nctx)
            yf, yb = _ssd_scan(xs_g, bm_g, cm_g, dt, ssd_a_log[j], nbatch, seq, nctx)
            h = _ssd_out(yf, yb, xs_g, zx, ssd_d[j], ssd_norm_g[j], ssd_w_out[j].astype(BF16), h, mods,
                         rows, seq, nbatch)
        elif kind == 1:
            p = _proj(h, norm_mix_g[i], mods, gqa_w_qkv[j].astype(BF16), seq, nbatch, F32)
            o = _gqa(p, gqa_sink[j], cos2, sin_s, nbatch, seq, nctx)
            h = _outproj(o, gqa_w_out[j].astype(BF16), h, mods, rows, seq, nbatch)
        else:
            lambda_init = 0.8 - 0.6 * math.exp(-0.3 * i)
            p = _proj(h, norm_mix_g[i], mods, diff_w_qkv[j].astype(BF16), seq, nbatch, F32)
            o = _diff_attn(p, diff_lam_q1[j], diff_lam_k1[j], diff_lam_q2[j], diff_lam_k2[j], diff_subln_g[j],
                           cos2, sin_s, nbatch, seq, nctx, lambda_init)
            h = _outproj(o, diff_w_out[j].astype(BF16), h, mods, rows, seq, nbatch)
        h = _mlp(h, norm_mlp_g[i], mods, mlp_w1[i].astype(BF16), mlp_w2[i].astype(BF16), final_norm_g,
                 rows, seq, nbatch, final_norm=not need_ctx)
    return h[:nlat_rows].reshape(nbatch, seq, d)
```

```python
import functools
import math

import jax
import jax.numpy as jnp
from jax import lax
from jax.experimental import pallas as pl
from jax.experimental.pallas import tpu as pltpu

F32 = jnp.float32
BF16 = jnp.bfloat16

EPS = 1e-6
GRID_W = 64
ROPE_THETA = 10000.0
N_MIXERS = 3

SSD_HEAD_DIM = 64
SSD_GROUPS = 8
SSD_STATE = 128
SSD_CONV = 5
SSD_CHUNK = 128

GQA_HEAD_DIM = 128
GQA_KV_HEADS = 4
WINDOW = 128

DIFF_HEAD_DIM = 128

LANES = 128
SUBLANES = 8
CONV_HALO = 16
VMEM_CAP = 56 * 1024 * 1024


def _cparams(sem, vmem_mb):
    return pltpu.CompilerParams(dimension_semantics=sem,
                                vmem_limit_bytes=min(int(vmem_mb * 1024 * 1024), VMEM_CAP))


def _nt_dot(a, b):
    return lax.dot_general(a, b, (((1,), (1,)), ((), ())), preferred_element_type=F32)


def _pick_tile(n, cap):
    best = LANES
    t = LANES
    while t <= cap:
        if n % t == 0:
            best = t
        t += LANES
    return best


def _norm_mod(x, g, shift, scale):
    ms = jnp.mean(x * x, axis=-1, keepdims=True)
    y = x * lax.rsqrt(ms + EPS) * g
    return y * (1.0 + scale) + shift


def _ada_kernel(cond_ref, w_ref, b_ref, o_ref):
    a = cond_ref[...]
    s = a / (1.0 + jnp.exp(-a))
    o_ref[0] = jnp.dot(s.astype(BF16), w_ref[0].astype(BF16), preferred_element_type=F32) + b_ref[0]


def _ada_all(cond8, ada_w, ada_b):
    depth, d, n = ada_w.shape
    tn = _pick_tile(n, 1024)
    return pl.pallas_call(
        _ada_kernel,
        grid=(depth, n // tn),
        in_specs=[pl.BlockSpec((SUBLANES, d), lambda l, j: (0, 0)),
                  pl.BlockSpec((1, d, tn), lambda l, j: (l, 0, j)),
                  pl.BlockSpec((1, 1, tn), lambda l, j: (l, 0, j))],
        out_specs=pl.BlockSpec((1, SUBLANES, tn), lambda l, j: (l, 0, j)),
        out_shape=jax.ShapeDtypeStruct((depth, SUBLANES, n), F32),
        compiler_params=_cparams(("arbitrary", "arbitrary"), 40),
        name="ada_mod",
    )(cond8, ada_w, ada_b.reshape(depth, 1, n))


def _proj_kernel(h_ref, g_ref, mod_ref, w_ref, o_ref, u_sc, *, shift_idx, scale_idx):
    @pl.when(pl.program_id(1) == 0)
    def _():
        m = mod_ref[0]
        u = _norm_mod(h_ref[...], g_ref[...], m[shift_idx:shift_idx + 1], m[scale_idx:scale_idx + 1])
        u_sc[...] = u.astype(BF16)

    o_ref[...] = jnp.dot(u_sc[...], w_ref[0], preferred_element_type=F32).astype(o_ref.dtype)


def _proj(h, g, mods, w, layer, col0, ncols, seq, nbatch, out_dtype, tm=512):
    t, d = h.shape
    tn = _pick_tile(math.gcd(ncols, col0) if col0 else ncols, 3072)
    cb = col0 // tn
    return pl.pallas_call(
        functools.partial(_proj_kernel, shift_idx=0, scale_idx=1),
        grid=(t // tm, ncols // tn),
        in_specs=[pl.BlockSpec((tm, d), lambda i, j: (i, 0)),
                  pl.BlockSpec((1, d), lambda i, j: (0, 0)),
                  pl.BlockSpec((1, 6, d), lambda i, j: (jnp.minimum((i * tm) // seq, nbatch), 0, 0)),
                  pl.BlockSpec((1, d, tn), lambda i, j: (layer, 0, cb + j))],
        out_specs=pl.BlockSpec((tm, tn), lambda i, j: (i, j)),
        out_shape=jax.ShapeDtypeStruct((t, ncols), out_dtype),
        scratch_shapes=[pltpu.VMEM((tm, d), BF16)],
        compiler_params=_cparams(("arbitrary", "arbitrary"), 48),
        name="norm_mod_proj",
    )(h, g.reshape(1, d), mods, w)


def _mlp_kernel(h_ref, g_ref, mod_ref, w1_ref, w2_ref, fg_ref, o_ref, u_sc, acc_sc, *, nk, final_norm):
    k = pl.program_id(1)

    @pl.when(k == 0)
    def _():
        m = mod_ref[0]
        u = _norm_mod(h_ref[...], g_ref[...], m[3:4], m[4:5])
        u_sc[...] = u.astype(BF16)
        acc_sc[...] = jnp.zeros_like(acc_sc)

    hk = jnp.dot(u_sc[...], w1_ref[0], preferred_element_type=F32)
    hk = jnp.square(jnp.maximum(hk, 0.0)).astype(BF16)
    acc_sc[...] += jnp.dot(hk, w2_ref[0], preferred_element_type=F32)

    @pl.when(k == nk - 1)
    def _():
        m = mod_ref[0]
        out = h_ref[...] + m[5:6] * acc_sc[...]
        if final_norm:
            ms = jnp.mean(out * out, axis=-1, keepdims=True)
            out = out * lax.rsqrt(ms + EPS) * fg_ref[...]
        o_ref[...] = out


def _mlp(h, g, mods, w1, w2, layer, final_g, rows, seq, nbatch, final_norm, tm=512, tk=1024):
    d = h.shape[1]
    hid = w1.shape[2]
    nk = hid // tk
    return pl.pallas_call(
        functools.partial(_mlp_kernel, nk=nk, final_norm=final_norm),
        grid=(rows // tm, nk),
        in_specs=[pl.BlockSpec((tm, d), lambda i, k: (i, 0)),
                  pl.BlockSpec((1, d), lambda i, k: (0, 0)),
                  pl.BlockSpec((1, 6, d), lambda i, k: (jnp.minimum((i * tm) // seq, nbatch), 0, 0)),
                  pl.BlockSpec((1, d, tk), lambda i, k: (layer, 0, k)),
                  pl.BlockSpec((1, tk, d), lambda i, k: (layer, k, 0)),
                  pl.BlockSpec((1, d), lambda i, k: (0, 0))],
        out_specs=pl.BlockSpec((tm, d), lambda i, k: (i, 0)),
        out_shape=jax.ShapeDtypeStruct((rows, d), F32),
        scratch_shapes=[pltpu.VMEM((tm, d), BF16), pltpu.VMEM((tm, d), F32)],
        compiler_params=_cparams(("arbitrary", "arbitrary"), 52),
        name="mlp",
    )(h, g.reshape(1, d), mods, w1, w2, final_g.reshape(1, d))


def _outproj_kernel(y_ref, w_ref, h_ref, mod_ref, o_ref):
    acc = jnp.dot(y_ref[...], w_ref[0], preferred_element_type=F32)
    o_ref[...] = h_ref[...] + mod_ref[0][2:3] * acc


def _outproj(y, w, layer, h, mods, rows, seq, nbatch, tm=512, tn=512):
    kdim = y.shape[1]
    d = w.shape[2]
    return pl.pallas_call(
        _outproj_kernel,
        grid=(rows // tm, d // tn),
        in_specs=[pl.BlockSpec((tm, kdim), lambda i, j: (i, 0)),
                  pl.BlockSpec((1, kdim, tn), lambda i, j: (layer, 0, j)),
                  pl.BlockSpec((tm, tn), lambda i, j: (i, j)),
                  pl.BlockSpec((1, 6, tn), lambda i, j: (jnp.minimum((i * tm) // seq, nbatch), 0, j))],
        out_specs=pl.BlockSpec((tm, tn), lambda i, j: (i, j)),
        out_shape=jax.ShapeDtypeStruct((rows, d), F32),
        compiler_params=_cparams(("arbitrary", "arbitrary"), 40),
        name="out_proj",
    )(y, w, h, mods)


def _rope_tables(n_tokens, head_dim):
    rows = n_tokens // GRID_W
    row = jnp.repeat(jnp.arange(rows, dtype=F32), GRID_W)
    col = jnp.tile(jnp.arange(GRID_W, dtype=F32), rows)
    n_freq = head_dim // 4
    inv_freq = ROPE_THETA ** (-jnp.arange(n_freq, dtype=F32) / n_freq)
    ang = jnp.concatenate([row[:, None] * inv_freq, col[:, None] * inv_freq], axis=-1)
    cos, sin = jnp.cos(ang), jnp.sin(ang)
    return jnp.concatenate([cos, cos], axis=-1), jnp.concatenate([-sin, sin], axis=-1)


def _rope(x, cos2, sin_s):
    return x * cos2 + pltpu.roll(x, x.shape[-1] // 2, 1) * sin_s


def _gqa_attend(qs, keys, vals, masks, sink_col, scale):
    ss = []
    for k_, m_ in zip(keys, masks):
        s = _nt_dot(qs, k_) * scale
        if m_ is not None:
            s = jnp.where(m_, s, -jnp.inf)
        ss.append(s)
    m = sink_col
    for s in ss:
        m = jnp.maximum(m, jnp.max(s, axis=-1, keepdims=True))
    l = jnp.exp(sink_col - m)
    acc = None
    for s, v_ in zip(ss, vals):
        p = jnp.exp(s - m)
        l = l + jnp.sum(p, axis=-1, keepdims=True)
        pv = jnp.dot(p.astype(BF16), v_, preferred_element_type=F32)
        acc = pv if acc is None else acc + pv
    return acc / l


def _gqa_kernel(sink_ref, q_ref, kp_ref, ko_ref, kn_ref, vp_ref, vo_ref, vn_ref, kc_ref, vc_ref,
                cos_ref, sin_ref, o_ref, *, nb, nctxb, rep):
    t = pl.program_id(1)
    hd = GQA_HEAD_DIM
    blk = WINDOW
    scale = hd ** -0.5

    def sink_col(h):
        return jnp.concatenate([jnp.full((blk, 1), sink_ref[h * rep + r], F32) for r in range(rep)], axis=0)

    def write(h, o):
        for r in range(rep):
            c0 = (h * rep + r) * hd
            o_ref[:, c0:c0 + hd] = o[r * blk:(r + 1) * blk].astype(o_ref.dtype)

    @pl.when(t < nctxb)
    def _ctx():
        for h in range(GQA_KV_HEADS):
            qs = jnp.concatenate([q_ref[:, (h * rep + r) * hd:(h * rep + r + 1) * hd] for r in range(rep)], axis=0)
            kc = kc_ref[:, h * hd:(h + 1) * hd]
            vc = vc_ref[:, h * hd:(h + 1) * hd]
            write(h, _gqa_attend(qs, [kc], [vc], [None], sink_col(h), scale))

    @pl.when(t >= nctxb)
    def _lat():
        n = t - nctxb
        pq = pl.multiple_of(n * blk, blk)
        pp = pl.multiple_of(jnp.maximum(n - 1, 0) * blk, blk)
        pn = pl.multiple_of(jnp.minimum(n + 1, nb - 1) * blk, blk)
        cq, sq = cos_ref[pl.ds(pq, blk), :], sin_ref[pl.ds(pq, blk), :]
        cp, sp = cos_ref[pl.ds(pp, blk), :], sin_ref[pl.ds(pp, blk), :]
        cn, sn = cos_ref[pl.ds(pn, blk), :], sin_ref[pl.ds(pn, blk), :]
        i = lax.broadcasted_iota(jnp.int32, (rep * blk, 3 * blk), 0) & (blk - 1)
        j = lax.broadcasted_iota(jnp.int32, (rep * blk, 3 * blk), 1)
        rel = j - blk - i
        valid = (rel <= WINDOW) & (rel >= -WINDOW)
        valid = valid & ((j >= blk) | (n >= 1)) & ((j < 2 * blk) | (n <= nb - 2))
        for h in range(GQA_KV_HEADS):
            qs = jnp.concatenate(
                [_rope(q_ref[:, (h * rep + r) * hd:(h * rep + r + 1) * hd].astype(F32), cq, sq)
                 for r in range(rep)], axis=0)
            sl = slice(h * hd, (h + 1) * hd)
            kw = jnp.concatenate([_rope(kp_ref[:, sl].astype(F32), cp, sp), _rope(ko_ref[:, sl].astype(F32), cq, sq),
                                  _rope(kn_ref[:, sl].astype(F32), cn, sn)], axis=0).astype(BF16)
            vw = jnp.concatenate([vp_ref[:, sl], vo_ref[:, sl], vn_ref[:, sl]], axis=0)
            kc = kc_ref[:, sl]
            vc = vc_ref[:, sl]
            write(h, _gqa_attend(qs.astype(BF16), [kw, kc], [vw, vc], [valid, None], sink_col(h), scale))


def _gqa(p, sink, cos2, sin_s, nbatch, seq, nctx):
    t = p.shape[0]
    hd = GQA_HEAD_DIM
    nkv = GQA_KV_HEADS
    dq = p.shape[1] - 2 * nkv * hd
    rep = dq // hd // nkv
    blk = WINDOW
    nb = seq // blk
    nctxb = nctx // blk
    latb = nbatch * seq // blk
    kvw = nkv * hd
    kcol = dq // kvw
    vcol = kcol + 1

    def qrow(b, s):
        return jnp.where(s < nctxb, latb + b * nctxb + s, b * nb + s - nctxb)

    def nidx(s):
        return jnp.maximum(s - nctxb, 0)

    def kvspec(off, col):
        return pl.BlockSpec((blk, kvw), lambda b, s: (b * nb + jnp.clip(nidx(s) + off, 0, nb - 1), col))

    ctx_blk = nbatch * seq // nctx
    return pl.pallas_call(
        functools.partial(_gqa_kernel, nb=nb, nctxb=nctxb, rep=rep),
        grid=(nbatch, nctxb + nb),
        in_specs=[pl.BlockSpec(memory_space=pltpu.SMEM),
                  pl.BlockSpec((blk, dq), lambda b, s: (qrow(b, s), 0)),
                  kvspec(-1, kcol), kvspec(0, kcol), kvspec(1, kcol),
                  kvspec(-1, vcol), kvspec(0, vcol), kvspec(1, vcol),
                  pl.BlockSpec((nctx, kvw), lambda b, s: (ctx_blk + b, kcol)),
                  pl.BlockSpec((nctx, kvw), lambda b, s: (ctx_blk + b, vcol)),
                  pl.BlockSpec((seq, hd), lambda b, s: (0, 0)),
                  pl.BlockSpec((seq, hd), lambda b, s: (0, 0))],
        out_specs=pl.BlockSpec((blk, dq), lambda b, s: (qrow(b, s), 0)),
        out_shape=jax.ShapeDtypeStruct((t, dq), BF16),
        compiler_params=_cparams(("arbitrary", "arbitrary"), 40),
        name="gqa_window_attn",
    )(sink, p, p, p, p, p, p, p, p, p, cos2, sin_s)


def _diff_kernel(lq1_ref, lk1_ref, lq2_ref, lk2_ref, g_ref, q_ref, kl_ref, kc_ref, vl_ref, vc_ref,
                 cos_ref, sin_ref, o_ref, k_sc, s_sc, *, nctx, seq, tq, kchunk, lambda_init):
    t = pl.program_id(2)
    hd = DIFF_HEAD_DIM
    scale = hd ** -0.5

    @pl.when(t == 0)
    def _prep():
        for r0 in range(0, seq, kchunk):
            c, s = cos_ref[r0:r0 + kchunk, :], sin_ref[r0:r0 + kchunk, :]
            for tt in range(2):
                k_sc[r0:r0 + kchunk, tt * hd:(tt + 1) * hd] = _rope(
                    kl_ref[r0:r0 + kchunk, tt * hd:(tt + 1) * hd].astype(F32), c, s).astype(BF16)

    lam = (jnp.exp(jnp.sum(lq1_ref[...] * lk1_ref[...], axis=-1, keepdims=True))
           - jnp.exp(jnp.sum(lq2_ref[...] * lk2_ref[...], axis=-1, keepdims=True)) + lambda_init)

    ctx_chunks = [(c0, min(kchunk, nctx - c0), kc_ref, vc_ref, c0) for c0 in range(0, nctx, kchunk)]
    lat_chunks = [(nctx + c0, kchunk, k_sc, vl_ref, c0) for c0 in range(0, seq, kchunk)]

    def run(qs, chunks):
        outs = []
        for tt in range(2):
            q = qs[tt].astype(BF16)
            m = jnp.full((tq, 1), -jnp.inf, F32)
            for (c0, cs, kr, _, r0) in chunks:
                s = _nt_dot(q, kr[r0:r0 + cs, tt * hd:(tt + 1) * hd]) * scale
                s_sc[tt, :, c0:c0 + cs] = s
                m = jnp.maximum(m, jnp.max(s, axis=-1, keepdims=True))
            l = jnp.zeros((tq, 1), F32)
            acc = jnp.zeros((tq, 2 * hd), F32)
            for (c0, cs, _, vr, r0) in chunks:
                e = jnp.exp(s_sc[tt, :, c0:c0 + cs] - m)
                l = l + jnp.sum(e, axis=-1, keepdims=True)
                acc = acc + jnp.dot(e.astype(BF16), vr[r0:r0 + cs, :], preferred_element_type=F32)
            outs.append(acc / l)
        o = outs[0] - lam * outs[1]
        ms = jnp.mean(o * o, axis=-1, keepdims=True)
        o = o * lax.rsqrt(ms + EPS) * g_ref[...] * (1.0 - lambda_init)
        o_ref[...] = o.astype(o_ref.dtype)

    @pl.when(t == 0)
    def _ctx():
        run([q_ref[:, 0:hd], q_ref[:, hd:2 * hd]], ctx_chunks)

    @pl.when(t > 0)
    def _lat():
        p0 = pl.multiple_of((t - 1) * tq, tq)
        c, s = cos_ref[pl.ds(p0, tq), :], sin_ref[pl.ds(p0, tq), :]
        run([_rope(q_ref[:, 0:hd].astype(F32), c, s), _rope(q_ref[:, hd:2 * hd].astype(F32), c, s)],
            ctx_chunks + lat_chunks)


def _diff_attn(p, lq1, lk1, lq2, lk2, subln_g, cos2, sin_s, nbatch, seq, nctx, lambda_init):
    t = p.shape[0]
    hd = DIFF_HEAD_DIM
    vd = 2 * hd
    nh = p.shape[1] // (3 * vd)
    tq = nctx
    nq = seq // tq
    ctx_blk = nbatch * seq // nctx
    kchunk = 512

    def qrow(b, s):
        return jnp.where(s == 0, ctx_blk + b, b * nq + s - 1)

    vec = lambda a: a.reshape(1, -1)
    return pl.pallas_call(
        functools.partial(_diff_kernel, nctx=nctx, seq=seq, tq=tq, kchunk=kchunk, lambda_init=lambda_init),
        grid=(nbatch, nh, nq + 1),
        in_specs=[pl.BlockSpec((1, hd), lambda b, h, s: (0, 0))] * 4
        + [pl.BlockSpec((1, vd), lambda b, h, s: (0, 0)),
           pl.BlockSpec((tq, vd), lambda b, h, s: (qrow(b, s), h)),
           pl.BlockSpec((seq, vd), lambda b, h, s: (b, nh + h)),
           pl.BlockSpec((nctx, vd), lambda b, h, s: (ctx_blk + b, nh + h)),
           pl.BlockSpec((seq, vd), lambda b, h, s: (b, 2 * nh + h)),
           pl.BlockSpec((nctx, vd), lambda b, h, s: (ctx_blk + b, 2 * nh + h)),
           pl.BlockSpec((seq, hd), lambda b, h, s: (0, 0)),
           pl.BlockSpec((seq, hd), lambda b, h, s: (0, 0))],
        out_specs=pl.BlockSpec((tq, vd), lambda b, h, s: (qrow(b, s), h)),
        out_shape=jax.ShapeDtypeStruct((t, nh * vd), BF16),
        scratch_shapes=[pltpu.VMEM((seq, vd), BF16), pltpu.VMEM((2, tq, nctx + seq), F32)],
        compiler_params=_cparams(("arbitrary", "arbitrary", "arbitrary"), 48),
        name="diff_attn",
    )(vec(lq1), vec(lk1), vec(lq2), vec(lk2), vec(subln_g), p, p, p, p, p, cos2, sin_s)


def _ssd_conv_kernel(xm_ref, xp_ref, xn_ref, bcm_ref, bcp_ref, bcn_ref, dt_ref, wx_ref, wbc_ref, bx_ref,
                     bbc_ref, dtb_ref, xs_ref, bm_ref, cm_ref, dto_ref, ext_sc, *, rows, seq, nctx, nlat_rows,
                     slab):
    r = pl.program_id(0)
    row0 = r * rows
    in_lat = row0 < nlat_rows
    seg = jnp.where(in_lat, seq, nctx)
    off = jnp.where(in_lat, row0, row0 - nlat_rows)
    is_first = (off % seg) == 0
    is_last = ((off + rows) % seg) == 0
    halo = CONV_HALO
    pad = SSD_CONV // 2
    ngrp = SSD_GROUPS
    nst = SSD_STATE

    def conv_slab(m_ref, p_ref, n_ref, w_ref, b_ref, c0):
        prev = jnp.where(is_first, 0.0, p_ref[:, c0:c0 + slab].astype(F32))
        nxt = jnp.where(is_last, 0.0, n_ref[:, c0:c0 + slab].astype(F32))
        ext_sc[0:halo, :] = prev
        ext_sc[halo:halo + rows, :] = m_ref[:, c0:c0 + slab].astype(F32)
        ext_sc[halo + rows:2 * halo + rows, :] = nxt
        acc = jnp.broadcast_to(b_ref[:, c0:c0 + slab], (rows, slab))
        for k in range(SSD_CONV):
            acc = acc + w_ref[k:k + 1, c0:c0 + slab] * ext_sc[pl.ds(halo - pad + k, rows), :]
        return acc / (1.0 + jnp.exp(-acc))

    for g in range(ngrp):
        xs_ref[g] = conv_slab(xm_ref, xp_ref, xn_ref, wx_ref, bx_ref, g * slab).astype(xs_ref.dtype)
    nbc = 2 * ngrp * nst // slab
    for sidx in range(nbc):
        y = conv_slab(bcm_ref, bcp_ref, bcn_ref, wbc_ref, bbc_ref, sidx * slab).astype(bm_ref.dtype)
        for q in range(slab // nst):
            gi = sidx * (slab // nst) + q
            if gi < ngrp:
                bm_ref[gi] = y[:, q * nst:(q + 1) * nst]
            else:
                cm_ref[gi - ngrp] = y[:, q * nst:(q + 1) * nst]
    v = dt_ref[...] + dtb_ref[...]
    dto_ref[...] = jnp.maximum(v, 0.0) + jnp.log1p(jnp.exp(-jnp.abs(v)))


def _ssd_conv(zx, dt_raw, conv_w, conv_b, dt_bias, d_inner, nbatch, seq, nctx):
    t = zx.shape[0]
    rows = 256
    slab = 512
    ngrp, nst = SSD_GROUPS, SSD_STATE
    bcw = 2 * ngrp * nst
    nh2 = dt_bias.size
    halo = CONV_HALO
    rb = rows // halo
    nhalo = t // halo
    xcol = 1
    bccol = 2 * d_inner // bcw
    wx, wbc = conv_w[:, :d_inner], conv_w[:, d_inner:]
    bx, bbc = conv_b[:d_inner].reshape(1, -1), conv_b[d_inner:].reshape(1, -1)
    prev = lambda r: jnp.maximum(r * rb - 1, 0)
    nxt = lambda r: jnp.minimum(r * rb + rb, nhalo - 1)
    return pl.pallas_call(
        functools.partial(_ssd_conv_kernel, rows=rows, seq=seq, nctx=nctx, nlat_rows=nbatch * seq, slab=slab),
        grid=(t // rows,),
        in_specs=[pl.BlockSpec((rows, d_inner), lambda r: (r, xcol)),
                  pl.BlockSpec((halo, d_inner), lambda r: (prev(r), xcol)),
                  pl.BlockSpec((halo, d_inner), lambda r: (nxt(r), xcol)),
                  pl.BlockSpec((rows, bcw), lambda r: (r, bccol)),
                  pl.BlockSpec((halo, bcw), lambda r: (prev(r), bccol)),
                  pl.BlockSpec((halo, bcw), lambda r: (nxt(r), bccol)),
                  pl.BlockSpec((rows, nh2), lambda r: (r, 0)),
                  pl.BlockSpec((SSD_CONV, d_inner), lambda r: (0, 0)),
                  pl.BlockSpec((SSD_CONV, bcw), lambda r: (0, 0)),
                  pl.BlockSpec((1, d_inner), lambda r: (0, 0)),
                  pl.BlockSpec((1, bcw), lambda r: (0, 0)),
                  pl.BlockSpec((1, nh2), lambda r: (0, 0))],
        out_specs=[pl.BlockSpec((ngrp, rows, d_inner // ngrp), lambda r: (0, r, 0)),
                   pl.BlockSpec((ngrp, rows, nst), lambda r: (0, r, 0)),
                   pl.BlockSpec((ngrp, rows, nst), lambda r: (0, r, 0)),
                   pl.BlockSpec((rows, nh2), lambda r: (r, 0))],
        out_shape=[jax.ShapeDtypeStruct((ngrp, t, d_inner // ngrp), BF16),
                   jax.ShapeDtypeStruct((ngrp, t, nst), BF16),
                   jax.ShapeDtypeStruct((ngrp, t, nst), BF16),
                   jax.ShapeDtypeStruct((t, nh2), F32)],
        scratch_shapes=[pltpu.VMEM((rows + 2 * halo, slab), F32)],
        compiler_params=_cparams(("arbitrary",), 40),
        name="ssd_conv",
    )(zx, zx, zx, zx, zx, zx, dt_raw, wx, wbc, bx, bbc, dt_bias.reshape(1, nh2))


def _ssd_scan_kernel(alog_ref, tri_ref, xf_ref, bf_ref, cf_ref, dtf_ref, xb_ref, bb_ref, cb_ref, dtb_ref,
                     yf_ref, yb_ref, st_sc, e_sc, dec_sc, et_sc, dtt_sc, wt_sc, ext_sc, *, nheads):
    q = SSD_CHUNK
    ngrp = SSD_GROUPS
    hpg = nheads // ngrp
    p = SSD_HEAD_DIM

    @pl.when(pl.program_id(1) == 0)
    def _():
        st_sc[...] = jnp.zeros_like(st_sc)

    a = -jnp.exp(alog_ref[...])
    tri = tri_ref[...]
    es, decs, dts, ws = [], [], [], []
    for d, dt_ref in enumerate((dtf_ref, dtb_ref)):
        dt = dt_ref[:, d * nheads:(d + 1) * nheads]
        la = dt * a[:, d * nheads:(d + 1) * nheads]
        acum = jnp.dot(tri, la, preferred_element_type=F32, precision=lax.Precision.HIGHEST)
        tot = acum[q - 1:q, :]
        e = acum if d == 0 else tot - acum + la
        es.append(e)
        decs.append(jnp.exp(e))
        dts.append(dt)
        ws.append(dt * jnp.exp(tot - e))
    e2 = jnp.concatenate(es, axis=1)
    e_sc[...] = e2
    dec_sc[...] = jnp.concatenate(decs, axis=1)
    e2t = e2.T
    et_sc[...] = e2t.reshape(2 * ngrp, hpg, q)
    dtt_sc[...] = jnp.concatenate(dts, axis=1).T.reshape(2 * ngrp, hpg, q)
    wt_sc[...] = jnp.concatenate(ws, axis=1).T.reshape(2 * ngrp, hpg, q)
    tot_col = jnp.concatenate([e2t[0:nheads, q - 1:q], e2t[nheads:2 * nheads, 0:1]], axis=0)
    ext_sc[...] = jnp.broadcast_to(jnp.exp(tot_col), (2 * nheads, q)).reshape(2 * ngrp, hpg, q)

    lane = lax.broadcasted_iota(jnp.int32, (q, 2 * nheads), 1)
    li = lax.broadcasted_iota(jnp.int32, (q, q), 0)
    si = lax.broadcasted_iota(jnp.int32, (q, q), 1)
    lo = lax.broadcasted_iota(jnp.int32, (q, 2 * p), 1) < p
    causal = (si <= li, si >= li)

    def group(g, carry):
        for d, (x_ref, b_ref, c_ref, y_ref) in enumerate(((xf_ref, bf_ref, cf_ref, yf_ref),
                                                          (xb_ref, bb_ref, cb_ref, yb_ref))):
            bg = b_ref[g]
            cg = c_ref[g]
            cbm = _nt_dot(cg, bg)
            bgt = bg.astype(F32).T
            st = st_sc[d, g]
            z = jnp.dot(cg, st.astype(BF16), preferred_element_type=F32)
            et = et_sc[d * ngrp + g]
            dtt = dtt_sc[d * ngrp + g]
            wt = wt_sc[d * ngrp + g]
            ext = ext_sc[d * ngrp + g]
            e_all = e_sc[...]
            dec_all = dec_sc[...]
            for jp in range(hpg // 2):
                ms, decc, bws, exts = [], [], [], []
                for j in (2 * jp, 2 * jp + 1):
                    col = d * nheads + g * hpg + j
                    sel = lane == col
                    ecol = jnp.sum(jnp.where(sel, e_all, 0.0), axis=1, keepdims=True)
                    decc.append(jnp.sum(jnp.where(sel, dec_all, 0.0), axis=1, keepdims=True))
                    lm = jnp.exp(jnp.where(causal[d], ecol - et[j:j + 1, :], -jnp.inf))
                    ms.append((cbm * lm * dtt[j:j + 1, :]).astype(BF16))
                    bws.append((bgt * wt[j:j + 1, :]).astype(BF16))
                    exts.append(ext[j:j + 1, :])
                xp = x_ref[g, :, jp * 2 * p:(jp + 1) * 2 * p]
                zero = jnp.zeros_like(xp)
                rhs = jnp.concatenate([jnp.where(lo, xp, zero), jnp.where(lo, zero, xp)], axis=0)
                y = jnp.dot(jnp.concatenate(ms, axis=1), rhs, preferred_element_type=F32)
                y = y + jnp.where(lo, decc[0], decc[1]) * z[:, jp * 2 * p:(jp + 1) * 2 * p]
                y_ref[g, :, jp * 2 * p:(jp + 1) * 2 * p] = y.astype(y_ref.dtype)
                contrib = jnp.dot(jnp.concatenate(bws, axis=1), rhs, preferred_element_type=F32)
                decay = jnp.where(lo[0:1, :], exts[0], exts[1])
                st_sc[d, g, :, jp * 2 * p:(jp + 1) * 2 * p] = st[:, jp * 2 * p:(jp + 1) * 2 * p] * decay + contrib
        return carry

    lax.fori_loop(0, ngrp, group, 0)


def _ssd_scan(xs_g, bm_g, cm_g, dt, a_log, nbatch, seq, nctx):
    ngrp, t, gw = xs_g.shape
    nst = bm_g.shape[2]
    q = SSD_CHUNK
    nheads = a_log.shape[1]
    hpg = nheads // ngrp
    nlat = seq // q
    nctxc = nctx // q
    latc = nbatch * nlat

    def fidx(b, s):
        return jnp.where(s < nctxc, latc + b * nctxc + s, b * nlat + s - nctxc)

    def bidx(b, s):
        return jnp.where(s < nctxc, latc + b * nctxc + nctxc - 1 - s, b * nlat + nlat - 1 - (s - nctxc))

    def specs(idx):
        return [pl.BlockSpec((ngrp, q, gw), lambda b, s: (0, idx(b, s), 0)),
                pl.BlockSpec((ngrp, q, nst), lambda b, s: (0, idx(b, s), 0)),
                pl.BlockSpec((ngrp, q, nst), lambda b, s: (0, idx(b, s), 0)),
                pl.BlockSpec((q, 2 * nheads), lambda b, s: (idx(b, s), 0))]

    tri = jnp.tril(jnp.ones((q, q), F32))
    yshape = jax.ShapeDtypeStruct((ngrp, t, gw), BF16)
    small = pltpu.VMEM((2 * ngrp, hpg, q), F32)
    return pl.pallas_call(
        functools.partial(_ssd_scan_kernel, nheads=nheads),
        grid=(nbatch, nctxc + nlat),
        in_specs=[pl.BlockSpec((1, 2 * nheads), lambda b, s: (0, 0)),
                  pl.BlockSpec((q, q), lambda b, s: (0, 0))] + specs(fidx) + specs(bidx),
        out_specs=[pl.BlockSpec((ngrp, q, gw), lambda b, s: (0, fidx(b, s), 0)),
                   pl.BlockSpec((ngrp, q, gw), lambda b, s: (0, bidx(b, s), 0))],
        out_shape=[yshape, yshape],
        scratch_shapes=[pltpu.VMEM((2, ngrp, nst, gw), F32),
                        pltpu.VMEM((q, 2 * nheads), F32), pltpu.VMEM((q, 2 * nheads), F32),
                        small, small, small, small],
        compiler_params=_cparams(("arbitrary", "arbitrary"), 48),
        name="ssd_scan",
    )(a_log.reshape(1, 2 * nheads), tri, xs_g, bm_g, cm_g, dt, xs_g, bm_g, cm_g, dt)


def _ssd_out_kernel(yf_ref, yb_ref, xs_ref, z_ref, dx_ref, ng_ref, w_ref, h_ref, mod_ref, o_ref, y_sc):
    ngrp = SSD_GROUPS
    gw = y_sc.shape[1] // ngrp

    @pl.when(pl.program_id(1) == 0)
    def _():
        for g in range(ngrp):
            sl = slice(g * gw, (g + 1) * gw)
            z = z_ref[:, sl].astype(F32)
            y = yf_ref[g].astype(F32) + yb_ref[g].astype(F32) + xs_ref[g].astype(F32) * dx_ref[:, sl]
            y = y * (z / (1.0 + jnp.exp(-z)))
            ms = jnp.mean(y * y, axis=-1, keepdims=True)
            y_sc[:, sl] = (y * lax.rsqrt(ms + EPS) * ng_ref[:, sl]).astype(BF16)

    acc = jnp.dot(y_sc[...], w_ref[0], preferred_element_type=F32)
    o_ref[...] = h_ref[...] + mod_ref[0][2:3] * acc


def _ssd_out(yf, yb, xs_g, zx, d_skip, norm_g, w_out, layer, h, mods, rows, seq, nbatch, tm=512, tn=512):
    ngrp, _, gw = xs_g.shape
    d_inner = ngrp * gw
    d = w_out.shape[2]
    dx = jnp.repeat(d_skip, SSD_HEAD_DIM).reshape(1, d_inner)
    gspec = pl.BlockSpec((ngrp, tm, gw), lambda i, j: (0, i, 0))
    return pl.pallas_call(
        _ssd_out_kernel,
        grid=(rows // tm, d // tn),
        in_specs=[gspec, gspec, gspec,
                  pl.BlockSpec((tm, d_inner), lambda i, j: (i, 0)),
                  pl.BlockSpec((1, d_inner), lambda i, j: (0, 0)),
                  pl.BlockSpec((1, d_inner), lambda i, j: (0, 0)),
                  pl.BlockSpec((1, d_inner, tn), lambda i, j: (layer, 0, j)),
                  pl.BlockSpec((tm, tn), lambda i, j: (i, j)),
                  pl.BlockSpec((1, 6, tn), lambda i, j: (jnp.minimum((i * tm) // seq, nbatch), 0, j))],
        out_specs=pl.BlockSpec((tm, tn), lambda i, j: (i, j)),
        out_shape=jax.ShapeDtypeStruct((rows, d), F32),
        scratch_shapes=[pltpu.VMEM((tm, d_inner), BF16)],
        compiler_params=_cparams(("arbitrary", "arbitrary"), 56),
        name="ssd_gate_norm_out_proj",
    )(yf, yb, xs_g, zx, dx, norm_g.reshape(1, d_inner), w_out, h, mods)


def kernel(x, c, ctx, c_ctx, ada_w, ada_b, norm_mix_g, norm_mlp_g, mlp_w1, mlp_w2, ssd_w_in, ssd_conv_w, ssd_conv_b, ssd_dt_bias, ssd_a_log, ssd_d, ssd_norm_g, ssd_w_out, gqa_w_qkv, gqa_sink, gqa_w_out, diff_w_qkv, diff_lam_q1, diff_lam_k1, diff_lam_q2, diff_lam_k2, diff_subln_g, diff_w_out, final_norm_g):
    nbatch, seq, d = x.shape
    nctx = ctx.shape[1]
    depth = ada_w.shape[0]
    nlat_rows = nbatch * seq
    t = nlat_rows + nbatch * nctx

    cond8 = jnp.zeros((SUBLANES, d), F32).at[:nbatch].set(c).at[nbatch].set(c_ctx)
    mods_all = _ada_all(cond8, ada_w, ada_b)[:, :nbatch + 1].reshape(depth, nbatch + 1, 6, d)
    cos2, sin_s = _rope_tables(seq, GQA_HEAD_DIM)

    h = jnp.concatenate([x.reshape(nlat_rows, d), ctx.reshape(nbatch * nctx, d)], axis=0)
    w1_b, w2_b = mlp_w1.astype(BF16), mlp_w2.astype(BF16)
    ssd_in_b, ssd_out_b = ssd_w_in.astype(BF16), ssd_w_out.astype(BF16)
    gqa_in_b, gqa_out_b = gqa_w_qkv.astype(BF16), gqa_w_out.astype(BF16)
    diff_in_b, diff_out_b = diff_w_qkv.astype(BF16), diff_w_out.astype(BF16)
    for i in range(depth):
        kind, j = i % N_MIXERS, i // N_MIXERS
        need_ctx = i < depth - 1
        rows = t if need_ctx else nlat_rows
        mods = mods_all[i]
        if kind == 0:
            d_inner = ssd_w_out.shape[1]
            nmain = ssd_w_in.shape[2] - 2 * ssd_a_log.shape[2]
            zx = _proj(h, norm_mix_g[i], mods, ssd_in_b, j, 0, nmain, seq, nbatch, BF16)
            dt_raw = _proj(h, norm_mix_g[i], mods, ssd_in_b, j, nmain, 2 * ssd_a_log.shape[2], seq, nbatch, F32)
            xs_g, bm_g, cm_g, dt = _ssd_conv(zx, dt_raw, ssd_conv_w[j], ssd_conv_b[j], ssd_dt_bias[j], d_inner,
                                             nbatch, seq, nctx)
            yf, yb = _ssd_scan(xs_g, bm_g, cm_g, dt, ssd_a_log[j], nbatch, seq, nctx)
            h = _ssd_out(yf, yb, xs_g, zx, ssd_d[j], ssd_norm_g[j], ssd_out_b, j, h, mods, rows, seq, nbatch)
        elif kind == 1:
            p = _proj(h, norm_mix_g[i], mods, gqa_in_b, j, 0, gqa_w_qkv.shape[2], seq, nbatch, BF16)
            o = _gqa(p, gqa_sink[j], cos2, sin_s, nbatch, seq, nctx)
            h = _outproj(o, gqa_out_b, j, h, mods, rows, seq, nbatch)
        else:
            lambda_init = 0.8 - 0.6 * math.exp(-0.3 * i)
            p = _proj(h, norm_mix_g[i], mods, diff_in_b, j, 0, diff_w_qkv.shape[2], seq, nbatch, BF16)
            o = _diff_attn(p, diff_lam_q1[j], diff_lam_k1[j], diff_lam_q2[j], diff_lam_k2[j], diff_subln_g[j],
                           cos2, sin_s, nbatch, seq, nctx, lambda_init)
            h = _outproj(o, diff_out_b, j, h, mods, rows, seq, nbatch)
        h = _mlp(h, norm_mlp_g[i], mods, w1_b, w2_b, i, final_norm_g, rows, seq, nbatch, final_norm=not need_ctx)
    return h[:nlat_rows].reshape(nbatch, seq, d)
```

```python
import functools
import math

import jax
import jax.numpy as jnp
from jax import lax
from jax.experimental import pallas as pl
from jax.experimental.pallas import tpu as pltpu

F32 = jnp.float32
BF16 = jnp.bfloat16

EPS = 1e-6
LOG2E = math.log2(math.e)
GRID_W = 64
ROPE_THETA = 10000.0
N_MIXERS = 3

SSD_HEAD_DIM = 64
SSD_GROUPS = 8
SSD_STATE = 128
SSD_CONV = 5
SSD_CHUNK = 128

GQA_HEAD_DIM = 128
GQA_KV_HEADS = 4
WINDOW = 128

DIFF_HEAD_DIM = 128
SM_ROWS = 128

LANES = 128
SUBLANES = 8
CONV_HALO = 16
VMEM_CAP = 56 * 1024 * 1024


def _cparams(sem, vmem_mb):
    return pltpu.CompilerParams(dimension_semantics=sem,
                                vmem_limit_bytes=min(int(vmem_mb * 1024 * 1024), VMEM_CAP))


def _nt_dot(a, b):
    return lax.dot_general(a, b, (((1,), (1,)), ((), ())), preferred_element_type=F32)


def _pick_tile(n, cap):
    best = LANES
    t = LANES
    while t <= cap:
        if n % t == 0:
            best = t
        t += LANES
    return best


def _norm_mod(x, g, shift, scale):
    ms = jnp.mean(x * x, axis=-1, keepdims=True)
    y = x * lax.rsqrt(ms + EPS) * g
    return y * (1.0 + scale) + shift


def _ada_kernel(cond_ref, w_ref, b_ref, o_ref):
    a = cond_ref[...]
    s = a / (1.0 + jnp.exp(-a))
    o_ref[0] = jnp.dot(s.astype(BF16), w_ref[0].astype(BF16), preferred_element_type=F32) + b_ref[0]


def _ada_all(cond8, ada_w, ada_b):
    depth, d, n = ada_w.shape
    tn = _pick_tile(n, 1024)
    return pl.pallas_call(
        _ada_kernel,
        grid=(depth, n // tn),
        in_specs=[pl.BlockSpec((SUBLANES, d), lambda l, j: (0, 0)),
                  pl.BlockSpec((1, d, tn), lambda l, j: (l, 0, j)),
                  pl.BlockSpec((1, 1, tn), lambda l, j: (l, 0, j))],
        out_specs=pl.BlockSpec((1, SUBLANES, tn), lambda l, j: (l, 0, j)),
        out_shape=jax.ShapeDtypeStruct((depth, SUBLANES, n), F32),
        compiler_params=_cparams(("arbitrary", "arbitrary"), 40),
        name="ada_mod",
    )(cond8, ada_w, ada_b.reshape(depth, 1, n))


def _proj_kernel(h_ref, g_ref, mod_ref, w_ref, *rest, n_tail):
    if n_tail:
        wt_ref, o_ref, ot_ref, u_sc = rest
    else:
        o_ref, u_sc = rest

    @pl.when(pl.program_id(1) == 0)
    def _():
        m = mod_ref[0]
        u = _norm_mod(h_ref[...], g_ref[...], m[0:1], m[1:2]).astype(BF16)
        u_sc[...] = u
        if n_tail:
            ot_ref[...] = jnp.dot(u, wt_ref[0], preferred_element_type=F32)

    o_ref[...] = jnp.dot(u_sc[...], w_ref[0], preferred_element_type=F32).astype(o_ref.dtype)


def _proj(h, g, mods, w, layer, ncols, n_tail, seq, nbatch, tm=512):
    t, d = h.shape
    tn = _pick_tile(ncols, 3072)
    in_specs = [pl.BlockSpec((tm, d), lambda i, j: (i, 0)),
                pl.BlockSpec((1, d), lambda i, j: (0, 0)),
                pl.BlockSpec((1, 6, d), lambda i, j: (jnp.minimum((i * tm) // seq, nbatch), 0, 0)),
                pl.BlockSpec((1, d, tn), lambda i, j: (layer, 0, j))]
    out_specs = [pl.BlockSpec((tm, tn), lambda i, j: (i, j))]
    out_shape = [jax.ShapeDtypeStruct((t, ncols), BF16)]
    args = [h, g.reshape(1, d), mods, w]
    if n_tail:
        assert ncols % n_tail == 0
        tail_blk = ncols // n_tail
        in_specs.append(pl.BlockSpec((1, d, n_tail), lambda i, j: (layer, 0, tail_blk)))
        out_specs.append(pl.BlockSpec((tm, n_tail), lambda i, j: (i, 0)))
        out_shape.append(jax.ShapeDtypeStruct((t, n_tail), F32))
        args.append(w)
    outs = pl.pallas_call(
        functools.partial(_proj_kernel, n_tail=n_tail),
        grid=(t // tm, ncols // tn),
        in_specs=in_specs,
        out_specs=out_specs,
        out_shape=out_shape,
        scratch_shapes=[pltpu.VMEM((tm, d), BF16)],
        compiler_params=_cparams(("arbitrary", "arbitrary"), 48),
        name="norm_mod_proj",
    )(*args)
    return outs if n_tail else outs[0]


def _mlp_kernel(h_ref, g_ref, mod_ref, w1_ref, w2_ref, fg_ref, o_ref, u_sc, acc_sc, *, nk, final_norm):
    k = pl.program_id(1)

    @pl.when(k == 0)
    def _():
        m = mod_ref[0]
        u = _norm_mod(h_ref[...], g_ref[...], m[3:4], m[4:5])
        u_sc[...] = u.astype(BF16)
        acc_sc[...] = jnp.zeros_like(acc_sc)

    hk = jnp.dot(u_sc[...], w1_ref[0], preferred_element_type=F32)
    hk = jnp.square(jnp.maximum(hk, 0.0)).astype(BF16)
    acc_sc[...] += jnp.dot(hk, w2_ref[0], preferred_element_type=F32)

    @pl.when(k == nk - 1)
    def _():
        m = mod_ref[0]
        out = h_ref[...] + m[5:6] * acc_sc[...]
        if final_norm:
            ms = jnp.mean(out * out, axis=-1, keepdims=True)
            out = out * lax.rsqrt(ms + EPS) * fg_ref[...]
        o_ref[...] = out


def _mlp(h, g, mods, w1, w2, layer, final_g, rows, seq, nbatch, final_norm, tm=512, tk=1024):
    d = h.shape[1]
    hid = w1.shape[2]
    nk = hid // tk
    return pl.pallas_call(
        functools.partial(_mlp_kernel, nk=nk, final_norm=final_norm),
        grid=(rows // tm, nk),
        in_specs=[pl.BlockSpec((tm, d), lambda i, k: (i, 0)),
                  pl.BlockSpec((1, d), lambda i, k: (0, 0)),
                  pl.BlockSpec((1, 6, d), lambda i, k: (jnp.minimum((i * tm) // seq, nbatch), 0, 0)),
                  pl.BlockSpec((1, d, tk), lambda i, k: (layer, 0, k)),
                  pl.BlockSpec((1, tk, d), lambda i, k: (layer, k, 0)),
                  pl.BlockSpec((1, d), lambda i, k: (0, 0))],
        out_specs=pl.BlockSpec((tm, d), lambda i, k: (i, 0)),
        out_shape=jax.ShapeDtypeStruct((rows, d), F32),
        scratch_shapes=[pltpu.VMEM((tm, d), BF16), pltpu.VMEM((tm, d), F32)],
        compiler_params=_cparams(("arbitrary", "arbitrary"), 52),
        name="mlp",
    )(h, g.reshape(1, d), mods, w1, w2, final_g.reshape(1, d))


def _outproj_kernel(y_ref, w_ref, h_ref, mod_ref, o_ref):
    acc = jnp.dot(y_ref[...], w_ref[0], preferred_element_type=F32)
    o_ref[...] = h_ref[...] + mod_ref[0][2:3] * acc


def _outproj(y, w, layer, h, mods, rows, seq, nbatch, tm=512):
    kdim = y.shape[1]
    d = w.shape[2]
    tn = d
    return pl.pallas_call(
        _outproj_kernel,
        grid=(rows // tm, d // tn),
        in_specs=[pl.BlockSpec((tm, kdim), lambda i, j: (i, 0)),
                  pl.BlockSpec((1, kdim, tn), lambda i, j: (layer, 0, j)),
                  pl.BlockSpec((tm, tn), lambda i, j: (i, j)),
                  pl.BlockSpec((1, 6, tn), lambda i, j: (jnp.minimum((i * tm) // seq, nbatch), 0, j))],
        out_specs=pl.BlockSpec((tm, tn), lambda i, j: (i, j)),
        out_shape=jax.ShapeDtypeStruct((rows, d), F32),
        compiler_params=_cparams(("arbitrary", "arbitrary"), 44),
        name="out_proj",
    )(y, w, h, mods)


def _rope_tables(n_tokens, head_dim):
    rows = n_tokens // GRID_W
    row = jnp.repeat(jnp.arange(rows, dtype=F32), GRID_W)
    col = jnp.tile(jnp.arange(GRID_W, dtype=F32), rows)
    n_freq = head_dim // 4
    inv_freq = ROPE_THETA ** (-jnp.arange(n_freq, dtype=F32) / n_freq)
    ang = jnp.concatenate([row[:, None] * inv_freq, col[:, None] * inv_freq], axis=-1)
    cos, sin = jnp.cos(ang), jnp.sin(ang)
    return jnp.concatenate([cos, cos], axis=-1), jnp.concatenate([-sin, sin], axis=-1)


def _rope(x, cos2, sin_s):
    return x * cos2 + pltpu.roll(x, x.shape[-1] // 2, 1) * sin_s


def _gqa_head(h, qs, segs, sink_ref, s_sc, p_sc, rep):
    blk = WINDOW
    for (c0, w, k_, _, _) in segs:
        s_sc[h, :, c0:c0 + w] = _nt_dot(qs, k_)
    linv = []
    for r in range(rep):
        rows = slice(r * blk, (r + 1) * blk)
        sink = sink_ref[h * rep + r] * LOG2E
        m = jnp.full((blk, 1), sink, F32)
        svals = []
        for (c0, w, _, _, msk) in segs:
            s = s_sc[h, rows, c0:c0 + w]
            if msk is not None:
                s = jnp.where(msk, s, -jnp.inf)
            svals.append(s)
            m = jnp.maximum(m, jnp.max(s, axis=-1, keepdims=True))
        l = jnp.exp2(sink - m)
        for (c0, w, _, _, _), s in zip(segs, svals):
            p = jnp.exp2(s - m)
            l = l + jnp.sum(p, axis=-1, keepdims=True)
            p_sc[h, rows, c0:c0 + w] = p.astype(BF16)
        linv.append(1.0 / l)
    acc = None
    for (c0, w, _, v_, _) in segs:
        pv = jnp.dot(p_sc[h, :, c0:c0 + w], v_, preferred_element_type=F32)
        acc = pv if acc is None else acc + pv
    return acc * jnp.concatenate(linv, axis=0)


def _gqa_kernel(sink_ref, q_ref, kp_ref, ko_ref, kn_ref, vp_ref, vo_ref, vn_ref, kc_ref, vc_ref,
                cos_ref, sin_ref, o_ref, s_sc, p_sc, *, nb, nctxb, rep):
    t = pl.program_id(1)
    hd = GQA_HEAD_DIM
    blk = WINDOW
    nctx = kc_ref.shape[0]
    qscale = hd ** -0.5 * LOG2E

    def write(h, o):
        for r in range(rep):
            c0 = (h * rep + r) * hd
            o_ref[:, c0:c0 + hd] = o[r * blk:(r + 1) * blk].astype(o_ref.dtype)

    @pl.when(t < nctxb)
    def _ctx():
        for h in range(GQA_KV_HEADS):
            qs = jnp.concatenate([q_ref[:, (h * rep + r) * hd:(h * rep + r + 1) * hd] for r in range(rep)], axis=0)
            qs = (qs.astype(F32) * qscale).astype(BF16)
            kc = kc_ref[:, h * hd:(h + 1) * hd]
            vc = vc_ref[:, h * hd:(h + 1) * hd]
            write(h, _gqa_head(h, qs, [(0, nctx, kc, vc, None)], sink_ref, s_sc, p_sc, rep))

    @pl.when(t >= nctxb)
    def _lat():
        n = t - nctxb
        pq = pl.multiple_of(n * blk, blk)
        pp = pl.multiple_of(jnp.maximum(n - 1, 0) * blk, blk)
        pn = pl.multiple_of(jnp.minimum(n + 1, nb - 1) * blk, blk)
        cq, sq = cos_ref[pl.ds(pq, blk), :], sin_ref[pl.ds(pq, blk), :]
        cp, sp = cos_ref[pl.ds(pp, blk), :], sin_ref[pl.ds(pp, blk), :]
        cn, sn = cos_ref[pl.ds(pn, blk), :], sin_ref[pl.ds(pn, blk), :]
        i = lax.broadcasted_iota(jnp.int32, (blk, 3 * blk), 0)
        j = lax.broadcasted_iota(jnp.int32, (blk, 3 * blk), 1)
        rel = j - blk - i
        valid = (rel <= WINDOW) & (rel >= -WINDOW)
        valid = valid & ((j >= blk) | (n >= 1)) & ((j < 2 * blk) | (n <= nb - 2))
        for h in range(GQA_KV_HEADS):
            qs = jnp.concatenate(
                [_rope(q_ref[:, (h * rep + r) * hd:(h * rep + r + 1) * hd].astype(F32), cq, sq)
                 for r in range(rep)], axis=0)
            sl = slice(h * hd, (h + 1) * hd)
            kw = jnp.concatenate([_rope(kp_ref[:, sl].astype(F32), cp, sp), _rope(ko_ref[:, sl].astype(F32), cq, sq),
                                  _rope(kn_ref[:, sl].astype(F32), cn, sn)], axis=0).astype(BF16)
            vw = jnp.concatenate([vp_ref[:, sl], vo_ref[:, sl], vn_ref[:, sl]], axis=0)
            kc = kc_ref[:, sl]
            vc = vc_ref[:, sl]
            segs = [(0, 3 * blk, kw, vw, valid), (3 * blk, nctx, kc, vc, None)]
            write(h, _gqa_head(h, (qs * qscale).astype(BF16), segs, sink_ref, s_sc, p_sc, rep))


def _gqa(p, sink, cos2, sin_s, nbatch, seq, nctx):
    t = p.shape[0]
    hd = GQA_HEAD_DIM
    nkv = GQA_KV_HEADS
    dq = p.shape[1] - 2 * nkv * hd
    rep = dq // hd // nkv
    blk = WINDOW
    nb = seq // blk
    nctxb = nctx // blk
    latb = nbatch * seq // blk
    kvw = nkv * hd
    kcol = dq // kvw
    vcol = kcol + 1

    def qrow(b, s):
        return jnp.where(s < nctxb, latb + b * nctxb + s, b * nb + s - nctxb)

    def nidx(s):
        return jnp.maximum(s - nctxb, 0)

    def kvspec(off, col):
        return pl.BlockSpec((blk, kvw), lambda b, s: (b * nb + jnp.clip(nidx(s) + off, 0, nb - 1), col))

    ctx_blk = nbatch * seq // nctx
    return pl.pallas_call(
        functools.partial(_gqa_kernel, nb=nb, nctxb=nctxb, rep=rep),
        grid=(nbatch, nctxb + nb),
        in_specs=[pl.BlockSpec(memory_space=pltpu.SMEM),
                  pl.BlockSpec((blk, dq), lambda b, s: (qrow(b, s), 0)),
                  kvspec(-1, kcol), kvspec(0, kcol), kvspec(1, kcol),
                  kvspec(-1, vcol), kvspec(0, vcol), kvspec(1, vcol),
                  pl.BlockSpec((nctx, kvw), lambda b, s: (ctx_blk + b, kcol)),
                  pl.BlockSpec((nctx, kvw), lambda b, s: (ctx_blk + b, vcol)),
                  pl.BlockSpec((seq, hd), lambda b, s: (0, 0)),
                  pl.BlockSpec((seq, hd), lambda b, s: (0, 0))],
        out_specs=pl.BlockSpec((blk, dq), lambda b, s: (qrow(b, s), 0)),
        out_shape=jax.ShapeDtypeStruct((t, dq), BF16),
        scratch_shapes=[pltpu.VMEM((nkv, rep * blk, 3 * blk + nctx), F32),
                        pltpu.VMEM((nkv, rep * blk, 3 * blk + nctx), BF16)],
        compiler_params=_cparams(("arbitrary", "arbitrary"), 40),
        name="gqa_window_attn",
    )(sink, p, p, p, p, p, p, p, p, p, cos2, sin_s)


def _diff_kernel(lq1_ref, lk1_ref, lq2_ref, lk2_ref, g_ref, ql_ref, qc_ref, kl_ref, kc_ref, vl_ref, vc_ref,
                 cos_ref, sin_ref, ol_ref, oc_ref, k_sc, s_sc, e_sc, *, nctx, seq, tq, kchunk, lambda_init):
    t = pl.program_id(2)
    hd = DIFF_HEAD_DIM
    qscale = hd ** -0.5 * LOG2E

    @pl.when(t == 0)
    def _prep():
        for r0 in range(0, seq, kchunk):
            c, s = cos_ref[r0:r0 + kchunk, :], sin_ref[r0:r0 + kchunk, :]
            for tt in range(2):
                k_sc[r0:r0 + kchunk, tt * hd:(tt + 1) * hd] = _rope(
                    kl_ref[r0:r0 + kchunk, tt * hd:(tt + 1) * hd].astype(F32), c, s).astype(BF16)

    lam = (jnp.exp(jnp.sum(lq1_ref[...] * lk1_ref[...], axis=-1, keepdims=True))
           - jnp.exp(jnp.sum(lq2_ref[...] * lk2_ref[...], axis=-1, keepdims=True)) + lambda_init)

    ctx_chunks = [(c0, min(kchunk, nctx - c0), kc_ref, c0) for c0 in range(0, nctx, kchunk)]
    lat_chunks = [(nctx + c0, kchunk, k_sc, c0) for c0 in range(0, seq, kchunk)]

    def run(qs, chunks, nrows, o_ref, with_lat):
        for tt in range(2):
            q = (qs[tt] * qscale).astype(BF16)
            for (c0, cs, kr, r0) in chunks:
                s_sc[tt, 0:nrows, c0:c0 + cs] = _nt_dot(q, kr[r0:r0 + cs, tt * hd:(tt + 1) * hd])
        outs = []
        for tt in range(2):
            linv = []
            for r in range(0, nrows, SM_ROWS):
                rows = slice(r, r + SM_ROWS)
                m = jnp.full((SM_ROWS, 1), -jnp.inf, F32)
                for (c0, cs, _, _) in chunks:
                    m = jnp.maximum(m, jnp.max(s_sc[tt, rows, c0:c0 + cs], axis=-1, keepdims=True))
                l = jnp.zeros((SM_ROWS, 1), F32)
                for (c0, cs, _, _) in chunks:
                    e = jnp.exp2(s_sc[tt, rows, c0:c0 + cs] - m)
                    l = l + jnp.sum(e, axis=-1, keepdims=True)
                    e_sc[tt, rows, c0:c0 + cs] = e.astype(BF16)
                linv.append(1.0 / l)
            acc = jnp.dot(e_sc[tt, 0:nrows, 0:nctx], vc_ref[...], preferred_element_type=F32)
            if with_lat:
                acc = acc + jnp.dot(e_sc[tt, 0:nrows, nctx:nctx + seq], vl_ref[...], preferred_element_type=F32)
            outs.append(acc * jnp.concatenate(linv, axis=0))
        o = outs[0] - lam * outs[1]
        ms = jnp.mean(o * o, axis=-1, keepdims=True)
        o = o * lax.rsqrt(ms + EPS) * g_ref[...] * (1.0 - lambda_init)
        o_ref[...] = o.astype(o_ref.dtype)

    @pl.when(t == 0)
    def _ctx():
        run([qc_ref[:, 0:hd].astype(F32), qc_ref[:, hd:2 * hd].astype(F32)], ctx_chunks, nctx, oc_ref, False)

    @pl.when(t > 0)
    def _lat():
        p0 = pl.multiple_of((t - 1) * tq, tq)
        c, s = cos_ref[pl.ds(p0, tq), :], sin_ref[pl.ds(p0, tq), :]
        run([_rope(ql_ref[:, 0:hd].astype(F32), c, s), _rope(ql_ref[:, hd:2 * hd].astype(F32), c, s)],
            ctx_chunks + lat_chunks, tq, ol_ref, True)


def _diff_attn(p, lq1, lk1, lq2, lk2, subln_g, cos2, sin_s, nbatch, seq, nctx, lambda_init):
    hd = DIFF_HEAD_DIM
    vd = 2 * hd
    nh = p.shape[1] // (3 * vd)
    tq = 512
    nq = seq // tq
    ctx_blk = nbatch * seq // nctx
    kchunk = 512

    def lrow(b, s):
        return b * nq + jnp.maximum(s - 1, 0)

    vec = lambda a: a.reshape(1, -1)
    o_lat, o_ctx = pl.pallas_call(
        functools.partial(_diff_kernel, nctx=nctx, seq=seq, tq=tq, kchunk=kchunk, lambda_init=lambda_init),
        grid=(nbatch, nh, nq + 1),
        in_specs=[pl.BlockSpec((1, hd), lambda b, h, s: (0, 0))] * 4
        + [pl.BlockSpec((1, vd), lambda b, h, s: (0, 0)),
           pl.BlockSpec((tq, vd), lambda b, h, s: (lrow(b, s), h)),
           pl.BlockSpec((nctx, vd), lambda b, h, s: (ctx_blk + b, h)),
           pl.BlockSpec((seq, vd), lambda b, h, s: (b, nh + h)),
           pl.BlockSpec((nctx, vd), lambda b, h, s: (ctx_blk + b, nh + h)),
           pl.BlockSpec((seq, vd), lambda b, h, s: (b, 2 * nh + h)),
           pl.BlockSpec((nctx, vd), lambda b, h, s: (ctx_blk + b, 2 * nh + h)),
           pl.BlockSpec((seq, hd), lambda b, h, s: (0, 0)),
           pl.BlockSpec((seq, hd), lambda b, h, s: (0, 0))],
        out_specs=[pl.BlockSpec((tq, vd), lambda b, h, s: (lrow(b, s), h)),
                   pl.BlockSpec((nctx, vd), lambda b, h, s: (b, h))],
        out_shape=[jax.ShapeDtypeStruct((nbatch * seq, nh * vd), BF16),
                   jax.ShapeDtypeStruct((nbatch * nctx, nh * vd), BF16)],
        scratch_shapes=[pltpu.VMEM((seq, vd), BF16), pltpu.VMEM((2, tq, nctx + seq), F32),
                        pltpu.VMEM((2, tq, nctx + seq), BF16)],
        compiler_params=_cparams(("arbitrary", "arbitrary", "arbitrary"), 56),
        name="diff_attn",
    )(vec(lq1), vec(lk1), vec(lq2), vec(lk2), vec(subln_g), p, p, p, p, p, p, cos2, sin_s)
    return jnp.concatenate([o_lat, o_ctx], axis=0)


def _ssd_conv_kernel(xm_ref, xp_ref, xn_ref, bcm_ref, bcp_ref, bcn_ref, dt_ref, wx_ref, wbc_ref, bx_ref,
                     bbc_ref, dtb_ref, xs_ref, bm_ref, cm_ref, dto_ref, ext_sc, *, rows, seq, nctx, nlat_rows,
                     slab):
    r = pl.program_id(0)
    row0 = r * rows
    in_lat = row0 < nlat_rows
    seg = jnp.where(in_lat, seq, nctx)
    off = jnp.where(in_lat, row0, row0 - nlat_rows)
    is_first = (off % seg) == 0
    is_last = ((off + rows) % seg) == 0
    halo = CONV_HALO
    pad = SSD_CONV // 2
    ngrp = SSD_GROUPS
    nst = SSD_STATE

    def conv_slab(m_ref, p_ref, n_ref, w_ref, b_ref, c0):
        prev = jnp.where(is_first, 0.0, p_ref[:, c0:c0 + slab].astype(F32))
        nxt = jnp.where(is_last, 0.0, n_ref[:, c0:c0 + slab].astype(F32))
        ext_sc[0:halo, :] = prev
        ext_sc[halo:halo + rows, :] = m_ref[:, c0:c0 + slab].astype(F32)
        ext_sc[halo + rows:2 * halo + rows, :] = nxt
        acc = jnp.broadcast_to(b_ref[:, c0:c0 + slab], (rows, slab))
        for k in range(SSD_CONV):
            acc = acc + w_ref[k:k + 1, c0:c0 + slab] * ext_sc[pl.ds(halo - pad + k, rows), :]
        return acc / (1.0 + jnp.exp(-acc))

    for g in range(ngrp):
        xs_ref[g] = conv_slab(xm_ref, xp_ref, xn_ref, wx_ref, bx_ref, g * slab).astype(xs_ref.dtype)
    nbc = 2 * ngrp * nst // slab
    for sidx in range(nbc):
        y = conv_slab(bcm_ref, bcp_ref, bcn_ref, wbc_ref, bbc_ref, sidx * slab).astype(bm_ref.dtype)
        for q in range(slab // nst):
            gi = sidx * (slab // nst) + q
            if gi < ngrp:
                bm_ref[gi] = y[:, q * nst:(q + 1) * nst]
            else:
                cm_ref[gi - ngrp] = y[:, q * nst:(q + 1) * nst]
    v = dt_ref[...] + dtb_ref[...]
    dto_ref[...] = jnp.maximum(v, 0.0) + jnp.log1p(jnp.exp(-jnp.abs(v)))


def _ssd_conv(zx, dt_raw, conv_w, conv_b, dt_bias, d_inner, nbatch, seq, nctx):
    t = zx.shape[0]
    rows = 256
    slab = 512
    ngrp, nst = SSD_GROUPS, SSD_STATE
    bcw = 2 * ngrp * nst
    nh2 = dt_bias.size
    halo = CONV_HALO
    rb = rows // halo
    nhalo = t // halo
    xcol = 1
    bccol = 2 * d_inner // bcw
    wx, wbc = conv_w[:, :d_inner], conv_w[:, d_inner:]
    bx, bbc = conv_b[:d_inner].reshape(1, -1), conv_b[d_inner:].reshape(1, -1)
    prev = lambda r: jnp.maximum(r * rb - 1, 0)
    nxt = lambda r: jnp.minimum(r * rb + rb, nhalo - 1)
    return pl.pallas_call(
        functools.partial(_ssd_conv_kernel, rows=rows, seq=seq, nctx=nctx, nlat_rows=nbatch * seq, slab=slab),
        grid=(t // rows,),
        in_specs=[pl.BlockSpec((rows, d_inner), lambda r: (r, xcol)),
                  pl.BlockSpec((halo, d_inner), lambda r: (prev(r), xcol)),
                  pl.BlockSpec((halo, d_inner), lambda r: (nxt(r), xcol)),
                  pl.BlockSpec((rows, bcw), lambda r: (r, bccol)),
                  pl.BlockSpec((halo, bcw), lambda r: (prev(r), bccol)),
                  pl.BlockSpec((halo, bcw), lambda r: (nxt(r), bccol)),
                  pl.BlockSpec((rows, nh2), lambda r: (r, 0)),
                  pl.BlockSpec((SSD_CONV, d_inner), lambda r: (0, 0)),
                  pl.BlockSpec((SSD_CONV, bcw), lambda r: (0, 0)),
                  pl.BlockSpec((1, d_inner), lambda r: (0, 0)),
                  pl.BlockSpec((1, bcw), lambda r: (0, 0)),
                  pl.BlockSpec((1, nh2), lambda r: (0, 0))],
        out_specs=[pl.BlockSpec((ngrp, rows, d_inner // ngrp), lambda r: (0, r, 0)),
                   pl.BlockSpec((ngrp, rows, nst), lambda r: (0, r, 0)),
                   pl.BlockSpec((ngrp, rows, nst), lambda r: (0, r, 0)),
                   pl.BlockSpec((rows, nh2), lambda r: (r, 0))],
        out_shape=[jax.ShapeDtypeStruct((ngrp, t, d_inner // ngrp), BF16),
                   jax.ShapeDtypeStruct((ngrp, t, nst), BF16),
                   jax.ShapeDtypeStruct((ngrp, t, nst), BF16),
                   jax.ShapeDtypeStruct((t, nh2), F32)],
        scratch_shapes=[pltpu.VMEM((rows + 2 * halo, slab), F32)],
        compiler_params=_cparams(("arbitrary",), 40),
        name="ssd_conv",
    )(zx, zx, zx, zx, zx, zx, dt_raw, wx, wbc, bx, bbc, dt_bias.reshape(1, nh2))


def _ssd_scan_kernel(alog_ref, tri_ref, xf_ref, bf_ref, cf_ref, dtf_ref, xb_ref, bb_ref, cb_ref, dtb_ref,
                     yf_ref, yb_ref, st_sc, e_sc, et_sc, wt_sc, ext_sc, *, nheads):
    q = SSD_CHUNK
    ngrp = SSD_GROUPS
    hpg = nheads // ngrp
    p = SSD_HEAD_DIM

    @pl.when(pl.program_id(1) == 0)
    def _():
        st_sc[...] = jnp.zeros_like(st_sc)

    a = -jnp.exp(alog_ref[...]) * LOG2E
    tri = tri_ref[...]
    es, dts, ws = [], [], []
    for d, dt_ref in enumerate((dtf_ref, dtb_ref)):
        dt = dt_ref[:, d * nheads:(d + 1) * nheads]
        la = dt * a[:, d * nheads:(d + 1) * nheads]
        acum = jnp.dot(tri, la, preferred_element_type=F32, precision=lax.Precision.HIGHEST)
        tot = acum[q - 1:q, :]
        e = acum if d == 0 else tot - acum + la
        es.append(e)
        dts.append(dt)
        ws.append(dt * jnp.exp2(tot - e))
    e2 = jnp.concatenate(es, axis=1)
    e_sc[...] = e2
    e2t = e2.T
    et_sc[...] = (e2t - jnp.log2(jnp.concatenate(dts, axis=1).T)).reshape(2 * ngrp, hpg, q)
    wt_sc[...] = jnp.concatenate(ws, axis=1).T.reshape(2 * ngrp, hpg, q)
    tot_col = jnp.concatenate([e2t[0:nheads, q - 1:q], e2t[nheads:2 * nheads, 0:1]], axis=0)
    ext_sc[...] = jnp.broadcast_to(jnp.exp2(tot_col), (2 * nheads, q)).reshape(2 * ngrp, hpg, q)

    lane = lax.broadcasted_iota(jnp.int32, (q, 2 * nheads), 1)
    li = lax.broadcasted_iota(jnp.int32, (q, q), 0)
    si = lax.broadcasted_iota(jnp.int32, (q, q), 1)
    lo = lax.broadcasted_iota(jnp.int32, (q, 2 * p), 1) < p
    causal = (si <= li, si >= li)

    dirs = ((xf_ref, bf_ref, cf_ref, yf_ref), (xb_ref, bb_ref, cb_ref, yb_ref))

    def group(g, carry):
        pre = []
        for d, (x_ref, b_ref, c_ref, y_ref) in enumerate(dirs):
            bg = b_ref[g]
            cg = c_ref[g]
            st = st_sc[d, g]
            pre.append((_nt_dot(cg, bg),
                        bg.astype(F32).T,
                        st,
                        jnp.dot(cg, st.astype(BF16), preferred_element_type=F32),
                        et_sc[d * ngrp + g], wt_sc[d * ngrp + g], ext_sc[d * ngrp + g]))
        e_all = e_sc[...]
        for d, (x_ref, b_ref, c_ref, y_ref) in enumerate(dirs):
            cbm, bgt, st, z, et, wt, ext = pre[d]
            for jp in range(hpg // 2):
                ms, decc, bws, exts = [], [], [], []
                for j in (2 * jp, 2 * jp + 1):
                    col = d * nheads + g * hpg + j
                    sel = lane == col
                    ecol = jnp.sum(jnp.where(sel, e_all, 0.0), axis=1, keepdims=True)
                    decc.append(jnp.exp2(ecol))
                    lm = jnp.exp2(jnp.where(causal[d], ecol - et[j:j + 1, :], -jnp.inf))
                    ms.append((cbm * lm).astype(BF16))
                    bws.append((bgt * wt[j:j + 1, :]).astype(BF16))
                    exts.append(ext[j:j + 1, :])
                cols = slice(jp * 2 * p, (jp + 1) * 2 * p)
                xp = x_ref[g, :, cols]
                zero = jnp.zeros_like(xp)
                rhs = jnp.concatenate([jnp.where(lo, xp, zero), jnp.where(lo, zero, xp)], axis=0)
                y = jnp.dot(jnp.concatenate(ms, axis=1), rhs, preferred_element_type=F32)
                y = y + jnp.where(lo, decc[0], decc[1]) * z[:, cols]
                y_ref[g, :, cols] = y.astype(y_ref.dtype)
                contrib = jnp.dot(jnp.concatenate(bws, axis=1), rhs, preferred_element_type=F32)
                decay = jnp.where(lo[0:1, :], exts[0], exts[1])
                st_sc[d, g, :, cols] = st[:, cols] * decay + contrib
        return carry

    lax.fori_loop(0, ngrp, group, 0)


def _ssd_scan(xs_g, bm_g, cm_g, dt, a_log, nbatch, seq, nctx):
    ngrp, t, gw = xs_g.shape
    nst = bm_g.shape[2]
    q = SSD_CHUNK
    nheads = a_log.shape[1]
    hpg = nheads // ngrp
    nlat = seq // q
    nctxc = nctx // q
    latc = nbatch * nlat

    def fidx(b, s):
        return jnp.where(s < nctxc, latc + b * nctxc + s, b * nlat + s - nctxc)

    def bidx(b, s):
        return jnp.where(s < nctxc, latc + b * nctxc + nctxc - 1 - s, b * nlat + nlat - 1 - (s - nctxc))

    def specs(idx):
        return [pl.BlockSpec((ngrp, q, gw), lambda b, s: (0, idx(b, s), 0)),
                pl.BlockSpec((ngrp, q, nst), lambda b, s: (0, idx(b, s), 0)),
                pl.BlockSpec((ngrp, q, nst), lambda b, s: (0, idx(b, s), 0)),
                pl.BlockSpec((q, 2 * nheads), lambda b, s: (idx(b, s), 0))]

    tri = jnp.tril(jnp.ones((q, q), F32))
    yshape = jax.ShapeDtypeStruct((ngrp, t, gw), BF16)
    small = pltpu.VMEM((2 * ngrp, hpg, q), F32)
    return pl.pallas_call(
        functools.partial(_ssd_scan_kernel, nheads=nheads),
        grid=(nbatch, nctxc + nlat),
        in_specs=[pl.BlockSpec((1, 2 * nheads), lambda b, s: (0, 0)),
                  pl.BlockSpec((q, q), lambda b, s: (0, 0))] + specs(fidx) + specs(bidx),
        out_specs=[pl.BlockSpec((ngrp, q, gw), lambda b, s: (0, fidx(b, s), 0)),
                   pl.BlockSpec((ngrp, q, gw), lambda b, s: (0, bidx(b, s), 0))],
        out_shape=[yshape, yshape],
        scratch_shapes=[pltpu.VMEM((2, ngrp, nst, gw), F32), pltpu.VMEM((q, 2 * nheads), F32),
                        small, small, small],
        compiler_params=_cparams(("arbitrary", "arbitrary"), 48),
        name="ssd_scan",
    )(a_log.reshape(1, 2 * nheads), tri, xs_g, bm_g, cm_g, dt, xs_g, bm_g, cm_g, dt)


def _ssd_out_kernel(yf_ref, yb_ref, xs_ref, z_ref, dx_ref, ng_ref, w_ref, h_ref, mod_ref, o_ref, acc_sc, *,
                    nsteps, gps, gw):
    s = pl.program_id(1)

    @pl.when(s == 0)
    def _():
        acc_sc[...] = jnp.zeros_like(acc_sc)

    acc = acc_sc[...]
    for g in range(gps):
        sl = slice(g * gw, (g + 1) * gw)
        z = z_ref[:, sl].astype(F32)
        y = yf_ref[g].astype(F32) + yb_ref[g].astype(F32) + xs_ref[g].astype(F32) * dx_ref[:, sl]
        y = y * (z / (1.0 + jnp.exp(-z)))
        ms = jnp.mean(y * y, axis=-1, keepdims=True)
        yn = (y * lax.rsqrt(ms + EPS) * ng_ref[:, sl]).astype(BF16)
        acc = acc + jnp.dot(yn, w_ref[0, sl, :], preferred_element_type=F32)

    @pl.when(s < nsteps - 1)
    def _():
        acc_sc[...] = acc

    @pl.when(s == nsteps - 1)
    def _():
        o_ref[...] = h_ref[...] + mod_ref[0][2:3] * acc


def _ssd_out(yf, yb, xs_g, zx, d_skip, norm_g, w_out, layer, h, mods, rows, seq, nbatch, tm=512, gps=2):
    ngrp, _, gw = xs_g.shape
    d_inner = ngrp * gw
    d = w_out.shape[2]
    nsteps = ngrp // gps
    dx = jnp.repeat(d_skip, SSD_HEAD_DIM).reshape(1, d_inner)
    gspec = pl.BlockSpec((gps, tm, gw), lambda i, g: (g, i, 0))
    return pl.pallas_call(
        functools.partial(_ssd_out_kernel, nsteps=nsteps, gps=gps, gw=gw),
        grid=(rows // tm, nsteps),
        in_specs=[gspec, gspec, gspec,
                  pl.BlockSpec((tm, gps * gw), lambda i, g: (i, g)),
                  pl.BlockSpec((1, gps * gw), lambda i, g: (0, g)),
                  pl.BlockSpec((1, gps * gw), lambda i, g: (0, g)),
                  pl.BlockSpec((1, gps * gw, d), lambda i, g: (layer, g, 0)),
                  pl.BlockSpec((tm, d), lambda i, g: (i, 0)),
                  pl.BlockSpec((1, 6, d), lambda i, g: (jnp.minimum((i * tm) // seq, nbatch), 0, 0))],
        out_specs=pl.BlockSpec((tm, d), lambda i, g: (i, 0)),
        out_shape=jax.ShapeDtypeStruct((rows, d), F32),
        scratch_shapes=[pltpu.VMEM((tm, d), F32)],
        compiler_params=_cparams(("arbitrary", "arbitrary"), 48),
        name="ssd_gate_norm_out_proj",
    )(yf, yb, xs_g, zx, dx, norm_g.reshape(1, d_inner), w_out, h, mods)


def kernel(x, c, ctx, c_ctx, ada_w, ada_b, norm_mix_g, norm_mlp_g, mlp_w1, mlp_w2, ssd_w_in, ssd_conv_w, ssd_conv_b, ssd_dt_bias, ssd_a_log, ssd_d, ssd_norm_g, ssd_w_out, gqa_w_qkv, gqa_sink, gqa_w_out, diff_w_qkv, diff_lam_q1, diff_lam_k1, diff_lam_q2, diff_lam_k2, diff_subln_g, diff_w_out, final_norm_g):
    nbatch, seq, d = x.shape
    nctx = ctx.shape[1]
    depth = ada_w.shape[0]
    nlat_rows = nbatch * seq
    t = nlat_rows + nbatch * nctx

    cond8 = jnp.zeros((SUBLANES, d), F32).at[:nbatch].set(c).at[nbatch].set(c_ctx)
    mods_all = _ada_all(cond8, ada_w, ada_b)[:, :nbatch + 1].reshape(depth, nbatch + 1, 6, d)
    cos2, sin_s = _rope_tables(seq, GQA_HEAD_DIM)

    h = jnp.concatenate([x.reshape(nlat_rows, d), ctx.reshape(nbatch * nctx, d)], axis=0)
    w1_b, w2_b = mlp_w1.astype(BF16), mlp_w2.astype(BF16)
    ssd_in_b, ssd_out_b = ssd_w_in.astype(BF16), ssd_w_out.astype(BF16)
    gqa_in_b, gqa_out_b = gqa_w_qkv.astype(BF16), gqa_w_out.astype(BF16)
    diff_in_b, diff_out_b = diff_w_qkv.astype(BF16), diff_w_out.astype(BF16)
    for i in range(depth):
        kind, j = i % N_MIXERS, i // N_MIXERS
        need_ctx = i < depth - 1
        rows = t if need_ctx else nlat_rows
        mods = mods_all[i]
        if kind == 0:
            d_inner = ssd_w_out.shape[1]
            nmain = ssd_w_in.shape[2] - 2 * ssd_a_log.shape[2]
            zx, dt_raw = _proj(h, norm_mix_g[i], mods, ssd_in_b, j, nmain, 2 * ssd_a_log.shape[2], seq, nbatch)
            xs_g, bm_g, cm_g, dt = _ssd_conv(zx, dt_raw, ssd_conv_w[j], ssd_conv_b[j], ssd_dt_bias[j], d_inner,
                                             nbatch, seq, nctx)
            yf, yb = _ssd_scan(xs_g, bm_g, cm_g, dt, ssd_a_log[j], nbatch, seq, nctx)
            h = _ssd_out(yf, yb, xs_g, zx, ssd_d[j], ssd_norm_g[j], ssd_out_b, j, h, mods, rows, seq, nbatch)
        elif kind == 1:
            p = _proj(h, norm_mix_g[i], mods, gqa_in_b, j, gqa_w_qkv.shape[2], 0, seq, nbatch)
            o = _gqa(p, gqa_sink[j], cos2, sin_s, nbatch, seq, nctx)
            h = _outproj(o, gqa_out_b, j, h, mods, rows, seq, nbatch)
        else:
            lambda_init = 0.8 - 0.6 * math.exp(-0.3 * i)
            p = _proj(h, norm_mix_g[i], mods, diff_in_b, j, diff_w_qkv.shape[2], 0, seq, nbatch)
            o = _diff_attn(p, diff_lam_q1[j], diff_lam_k1[j], diff_lam_q2[j], diff_lam_k2[j], diff_subln_g[j],
                           cos2, sin_s, nbatch, seq, nctx, lambda_init)
            h = _outproj(o, diff_out_b, j, h, mods, rows, seq, nbatch)
        h = _mlp(h, norm_mlp_g[i], mods, w1_b, w2_b, i, final_norm_g, rows, seq, nbatch, final_norm=not need_ctx)
    return h[:nlat_rows].reshape(nbatch, seq, d)
```

```python
import functools
import math

import jax
import jax.numpy as jnp
from jax import lax
from jax.experimental import pallas as pl
from jax.experimental.pallas import tpu as pltpu

F32 = jnp.float32
BF16 = jnp.bfloat16

EPS = 1e-6
LOG2E = math.log2(math.e)
GRID_W = 64
ROPE_THETA = 10000.0
N_MIXERS = 3

SSD_HEAD_DIM = 64
SSD_GROUPS = 8
SSD_STATE = 128
SSD_CONV = 5
SSD_CHUNK = 128

GQA_HEAD_DIM = 128
GQA_KV_HEADS = 4
WINDOW = 128

DIFF_HEAD_DIM = 128
SM_ROWS = 128

LANES = 128
SUBLANES = 8
CONV_HALO = 16
VMEM_CAP = 56 * 1024 * 1024


def _cparams(sem, vmem_mb):
    return pltpu.CompilerParams(dimension_semantics=sem,
                                vmem_limit_bytes=min(int(vmem_mb * 1024 * 1024), VMEM_CAP))


def _nt_dot(a, b):
    return lax.dot_general(a, b, (((1,), (1,)), ((), ())), preferred_element_type=F32)


def _pick_tile(n, cap):
    best = LANES
    t = LANES
    while t <= cap:
        if n % t == 0:
            best = t
        t += LANES
    return best


def _norm_mod(x, g, shift, scale):
    ms = jnp.mean(x * x, axis=-1, keepdims=True)
    y = x * lax.rsqrt(ms + EPS) * g
    return y * (1.0 + scale) + shift


def _ada_kernel(cond_ref, w_ref, b_ref, o_ref):
    a = cond_ref[...]
    s = a / (1.0 + jnp.exp(-a))
    o_ref[0] = jnp.dot(s.astype(BF16), w_ref[0].astype(BF16), preferred_element_type=F32) + b_ref[0]


def _ada_all(cond8, ada_w, ada_b):
    depth, d, n = ada_w.shape
    tn = _pick_tile(n, 1024)
    return pl.pallas_call(
        _ada_kernel,
        grid=(depth, n // tn),
        in_specs=[pl.BlockSpec((SUBLANES, d), lambda l, j: (0, 0)),
                  pl.BlockSpec((1, d, tn), lambda l, j: (l, 0, j)),
                  pl.BlockSpec((1, 1, tn), lambda l, j: (l, 0, j))],
        out_specs=pl.BlockSpec((1, SUBLANES, tn), lambda l, j: (l, 0, j)),
        out_shape=jax.ShapeDtypeStruct((depth, SUBLANES, n), F32),
        compiler_params=_cparams(("arbitrary", "arbitrary"), 40),
        name="ada_mod",
    )(cond8, ada_w, ada_b.reshape(depth, 1, n))


def _proj_kernel(h_ref, g_ref, mod_ref, w_ref, *rest, n_tail):
    if n_tail:
        wt_ref, o_ref, ot_ref, u_sc = rest
    else:
        o_ref, u_sc = rest

    @pl.when(pl.program_id(1) == 0)
    def _():
        m = mod_ref[0]
        u = _norm_mod(h_ref[...], g_ref[...], m[0:1], m[1:2]).astype(BF16)
        u_sc[...] = u
        if n_tail:
            ot_ref[...] = jnp.dot(u, wt_ref[0], preferred_element_type=F32)

    o_ref[...] = jnp.dot(u_sc[...], w_ref[0], preferred_element_type=F32).astype(o_ref.dtype)


def _proj(h, g, mods, w, layer, ncols, n_tail, seq, nbatch, tm=512):
    t, d = h.shape
    tn = _pick_tile(ncols, 3072)
    in_specs = [pl.BlockSpec((tm, d), lambda i, j: (i, 0)),
                pl.BlockSpec((1, d), lambda i, j: (0, 0)),
                pl.BlockSpec((1, 6, d), lambda i, j: (jnp.minimum((i * tm) // seq, nbatch), 0, 0)),
                pl.BlockSpec((1, d, tn), lambda i, j: (layer, 0, j))]
    out_specs = [pl.BlockSpec((tm, tn), lambda i, j: (i, j))]
    out_shape = [jax.ShapeDtypeStruct((t, ncols), BF16)]
    args = [h, g.reshape(1, d), mods, w]
    if n_tail:
        assert ncols % n_tail == 0
        tail_blk = ncols // n_tail
        in_specs.append(pl.BlockSpec((1, d, n_tail), lambda i, j: (layer, 0, tail_blk)))
        out_specs.append(pl.BlockSpec((tm, n_tail), lambda i, j: (i, 0)))
        out_shape.append(jax.ShapeDtypeStruct((t, n_tail), F32))
        args.append(w)
    outs = pl.pallas_call(
        functools.partial(_proj_kernel, n_tail=n_tail),
        grid=(t // tm, ncols // tn),
        in_specs=in_specs,
        out_specs=out_specs,
        out_shape=out_shape,
        scratch_shapes=[pltpu.VMEM((tm, d), BF16)],
        compiler_params=_cparams(("arbitrary", "arbitrary"), 48),
        name="norm_mod_proj",
    )(*args)
    return outs if n_tail else outs[0]


def _mlp_kernel(h_ref, mod_ref, hn_ref, modn_ref, g_ref, w1_ref, w2_ref, fg_ref, o_ref, u_a, u_b, acc_sc, *,
                nk, final_norm):
    i, k = pl.program_id(0), pl.program_id(1)
    rows = hn_ref.shape[0] // nk

    @pl.when((i == 0) & (k == 0))
    def _():
        m = mod_ref[0]
        u_a[...] = _norm_mod(h_ref[...], g_ref[...], m[3:4], m[4:5]).astype(BF16)

    @pl.when(k == 0)
    def _():
        acc_sc[...] = jnp.zeros_like(acc_sc)

    def step(u_cur, u_nxt):
        r0 = pl.multiple_of(k * rows, rows)
        m = modn_ref[0]
        u_nxt[pl.ds(r0, rows), :] = _norm_mod(hn_ref[pl.ds(r0, rows), :], g_ref[...], m[3:4], m[4:5]).astype(BF16)
        hk = jnp.dot(u_cur[...], w1_ref[0], preferred_element_type=F32)
        hk = jnp.square(jnp.maximum(hk, 0.0)).astype(BF16)
        acc_sc[...] += jnp.dot(hk, w2_ref[0], preferred_element_type=F32)

    @pl.when(i % 2 == 0)
    def _():
        step(u_a, u_b)

    @pl.when(i % 2 == 1)
    def _():
        step(u_b, u_a)

    @pl.when(k == nk - 1)
    def _():
        m = mod_ref[0]
        out = h_ref[...] + m[5:6] * acc_sc[...]
        if final_norm:
            ms = jnp.mean(out * out, axis=-1, keepdims=True)
            out = out * lax.rsqrt(ms + EPS) * fg_ref[...]
        o_ref[...] = out


def _mlp(h, g, mods, w1, w2, layer, final_g, rows, seq, nbatch, final_norm, tm=512, tk=1024):
    d = h.shape[1]
    hid = w1.shape[2]
    nk = hid // tk
    ntiles = rows // tm
    nxt = lambda i: jnp.minimum(i + 1, ntiles - 1)
    row = lambda i: jnp.minimum((i * tm) // seq, nbatch)
    return pl.pallas_call(
        functools.partial(_mlp_kernel, nk=nk, final_norm=final_norm),
        grid=(ntiles, nk),
        in_specs=[pl.BlockSpec((tm, d), lambda i, k: (i, 0)),
                  pl.BlockSpec((1, 6, d), lambda i, k: (row(i), 0, 0)),
                  pl.BlockSpec((tm, d), lambda i, k: (nxt(i), 0)),
                  pl.BlockSpec((1, 6, d), lambda i, k: (row(nxt(i)), 0, 0)),
                  pl.BlockSpec((1, d), lambda i, k: (0, 0)),
                  pl.BlockSpec((1, d, tk), lambda i, k: (layer, 0, k)),
                  pl.BlockSpec((1, tk, d), lambda i, k: (layer, k, 0)),
                  pl.BlockSpec((1, d), lambda i, k: (0, 0))],
        out_specs=pl.BlockSpec((tm, d), lambda i, k: (i, 0)),
        out_shape=jax.ShapeDtypeStruct((rows, d), F32),
        scratch_shapes=[pltpu.VMEM((tm, d), BF16), pltpu.VMEM((tm, d), BF16), pltpu.VMEM((tm, d), F32)],
        compiler_params=_cparams(("arbitrary", "arbitrary"), 56),
        name="mlp",
    )(h, mods, h, mods, g.reshape(1, d), w1, w2, final_g.reshape(1, d))


def _outproj_kernel(y_ref, w_ref, h_ref, mod_ref, o_ref):
    acc = jnp.dot(y_ref[...], w_ref[0], preferred_element_type=F32)
    o_ref[...] = h_ref[...] + mod_ref[0][2:3] * acc


def _outproj(y, w, layer, h, mods, rows, seq, nbatch, tm=512):
    kdim = y.shape[1]
    d = w.shape[2]
    tn = d
    return pl.pallas_call(
        _outproj_kernel,
        grid=(rows // tm, d // tn),
        in_specs=[pl.BlockSpec((tm, kdim), lambda i, j: (i, 0)),
                  pl.BlockSpec((1, kdim, tn), lambda i, j: (layer, 0, j)),
                  pl.BlockSpec((tm, tn), lambda i, j: (i, j)),
                  pl.BlockSpec((1, 6, tn), lambda i, j: (jnp.minimum((i * tm) // seq, nbatch), 0, j))],
        out_specs=pl.BlockSpec((tm, tn), lambda i, j: (i, j)),
        out_shape=jax.ShapeDtypeStruct((rows, d), F32),
        compiler_params=_cparams(("arbitrary", "arbitrary"), 44),
        name="out_proj",
    )(y, w, h, mods)


def _rope_tables(n_tokens, head_dim):
    rows = n_tokens // GRID_W
    row = jnp.repeat(jnp.arange(rows, dtype=F32), GRID_W)
    col = jnp.tile(jnp.arange(GRID_W, dtype=F32), rows)
    n_freq = head_dim // 4
    inv_freq = ROPE_THETA ** (-jnp.arange(n_freq, dtype=F32) / n_freq)
    ang = jnp.concatenate([row[:, None] * inv_freq, col[:, None] * inv_freq], axis=-1)
    cos, sin = jnp.cos(ang), jnp.sin(ang)
    return jnp.concatenate([cos, cos], axis=-1), jnp.concatenate([-sin, sin], axis=-1)


def _rope(x, cos2, sin_s):
    return x * cos2 + pltpu.roll(x, x.shape[-1] // 2, 1) * sin_s


def _gqa_head(h, qs, segs, sink_ref, s_sc, p_sc, rep):
    blk = WINDOW
    for (c0, w, k_, _, _) in segs:
        s_sc[h, :, c0:c0 + w] = _nt_dot(qs, k_)
    linv = []
    for r in range(rep):
        rows = slice(r * blk, (r + 1) * blk)
        sink = sink_ref[h * rep + r] * LOG2E
        m = jnp.full((blk, 1), sink, F32)
        svals = []
        for (c0, w, _, _, msk) in segs:
            s = s_sc[h, rows, c0:c0 + w]
            if msk is not None:
                s = jnp.where(msk, s, -jnp.inf)
            svals.append(s)
            m = jnp.maximum(m, jnp.max(s, axis=-1, keepdims=True))
        l = jnp.exp2(sink - m)
        for (c0, w, _, _, _), s in zip(segs, svals):
            p = jnp.exp2(s - m)
            l = l + jnp.sum(p, axis=-1, keepdims=True)
            p_sc[h, rows, c0:c0 + w] = p.astype(BF16)
        linv.append(1.0 / l)
    acc = None
    for (c0, w, _, v_, _) in segs:
        pv = jnp.dot(p_sc[h, :, c0:c0 + w], v_, preferred_element_type=F32)
        acc = pv if acc is None else acc + pv
    return acc * jnp.concatenate(linv, axis=0)


def _gqa_kernel(sink_ref, q_ref, kp_ref, ko_ref, kn_ref, vp_ref, vo_ref, vn_ref, kc_ref, vc_ref,
                cos_ref, sin_ref, o_ref, s_sc, p_sc, *, nb, nctxb, rep):
    t = pl.program_id(1)
    hd = GQA_HEAD_DIM
    blk = WINDOW
    nctx = kc_ref.shape[0]
    qscale = hd ** -0.5 * LOG2E

    def write(h, o):
        for r in range(rep):
            c0 = (h * rep + r) * hd
            o_ref[:, c0:c0 + hd] = o[r * blk:(r + 1) * blk].astype(o_ref.dtype)

    @pl.when(t < nctxb)
    def _ctx():
        for h in range(GQA_KV_HEADS):
            qs = jnp.concatenate([q_ref[:, (h * rep + r) * hd:(h * rep + r + 1) * hd] for r in range(rep)], axis=0)
            qs = (qs.astype(F32) * qscale).astype(BF16)
            kc = kc_ref[:, h * hd:(h + 1) * hd]
            vc = vc_ref[:, h * hd:(h + 1) * hd]
            write(h, _gqa_head(h, qs, [(0, nctx, kc, vc, None)], sink_ref, s_sc, p_sc, rep))

    @pl.when(t >= nctxb)
    def _lat():
        n = t - nctxb
        pq = pl.multiple_of(n * blk, blk)
        pp = pl.multiple_of(jnp.maximum(n - 1, 0) * blk, blk)
        pn = pl.multiple_of(jnp.minimum(n + 1, nb - 1) * blk, blk)
        cq, sq = cos_ref[pl.ds(pq, blk), :], sin_ref[pl.ds(pq, blk), :]
        cp, sp = cos_ref[pl.ds(pp, blk), :], sin_ref[pl.ds(pp, blk), :]
        cn, sn = cos_ref[pl.ds(pn, blk), :], sin_ref[pl.ds(pn, blk), :]
        i = lax.broadcasted_iota(jnp.int32, (blk, 3 * blk), 0)
        j = lax.broadcasted_iota(jnp.int32, (blk, 3 * blk), 1)
        rel = j - blk - i
        valid = (rel <= WINDOW) & (rel >= -WINDOW)
        valid = valid & ((j >= blk) | (n >= 1)) & ((j < 2 * blk) | (n <= nb - 2))
        for h in range(GQA_KV_HEADS):
            qs = jnp.concatenate(
                [_rope(q_ref[:, (h * rep + r) * hd:(h * rep + r + 1) * hd].astype(F32), cq, sq)
                 for r in range(rep)], axis=0)
            sl = slice(h * hd, (h + 1) * hd)
            kw = jnp.concatenate([_rope(kp_ref[:, sl].astype(F32), cp, sp), _rope(ko_ref[:, sl].astype(F32), cq, sq),
                                  _rope(kn_ref[:, sl].astype(F32), cn, sn)], axis=0).astype(BF16)
            vw = jnp.concatenate([vp_ref[:, sl], vo_ref[:, sl], vn_ref[:, sl]], axis=0)
            kc = kc_ref[:, sl]
            vc = vc_ref[:, sl]
            segs = [(0, 3 * blk, kw, vw, valid), (3 * blk, nctx, kc, vc, None)]
            write(h, _gqa_head(h, (qs * qscale).astype(BF16), segs, sink_ref, s_sc, p_sc, rep))


def _gqa(p, sink, cos2, sin_s, nbatch, seq, nctx):
    t = p.shape[0]
    hd = GQA_HEAD_DIM
    nkv = GQA_KV_HEADS
    dq = p.shape[1] - 2 * nkv * hd
    rep = dq // hd // nkv
    blk = WINDOW
    nb = seq // blk
    nctxb = nctx // blk
    latb = nbatch * seq // blk
    kvw = nkv * hd
    kcol = dq // kvw
    vcol = kcol + 1

    def qrow(b, s):
        return jnp.where(s < nctxb, latb + b * nctxb + s, b * nb + s - nctxb)

    def nidx(s):
        return jnp.maximum(s - nctxb, 0)

    def kvspec(off, col):
        return pl.BlockSpec((blk, kvw), lambda b, s: (b * nb + jnp.clip(nidx(s) + off, 0, nb - 1), col))

    ctx_blk = nbatch * seq // nctx
    return pl.pallas_call(
        functools.partial(_gqa_kernel, nb=nb, nctxb=nctxb, rep=rep),
        grid=(nbatch, nctxb + nb),
        in_specs=[pl.BlockSpec(memory_space=pltpu.SMEM),
                  pl.BlockSpec((blk, dq), lambda b, s: (qrow(b, s), 0)),
                  kvspec(-1, kcol), kvspec(0, kcol), kvspec(1, kcol),
                  kvspec(-1, vcol), kvspec(0, vcol), kvspec(1, vcol),
                  pl.BlockSpec((nctx, kvw), lambda b, s: (ctx_blk + b, kcol)),
                  pl.BlockSpec((nctx, kvw), lambda b, s: (ctx_blk + b, vcol)),
                  pl.BlockSpec((seq, hd), lambda b, s: (0, 0)),
                  pl.BlockSpec((seq, hd), lambda b, s: (0, 0))],
        out_specs=pl.BlockSpec((blk, dq), lambda b, s: (qrow(b, s), 0)),
        out_shape=jax.ShapeDtypeStruct((t, dq), BF16),
        scratch_shapes=[pltpu.VMEM((nkv, rep * blk, 3 * blk + nctx), F32),
                        pltpu.VMEM((nkv, rep * blk, 3 * blk + nctx), BF16)],
        compiler_params=_cparams(("arbitrary", "arbitrary"), 40),
        name="gqa_window_attn",
    )(sink, p, p, p, p, p, p, p, p, p, cos2, sin_s)


def _diff_kernel(lq1_ref, lk1_ref, lq2_ref, lk2_ref, g_ref, ql_ref, qc_ref, kl_ref, kc_ref, vl_ref, vc_ref,
                 cos_ref, sin_ref, ol_ref, oc_ref, k_sc, s_sc, e_sc, *, nctx, seq, tq, kchunk, lambda_init):
    t = pl.program_id(2)
    hd = DIFF_HEAD_DIM
    qscale = hd ** -0.5 * LOG2E

    @pl.when(t == 0)
    def _prep():
        for r0 in range(0, seq, kchunk):
            c, s = cos_ref[r0:r0 + kchunk, :], sin_ref[r0:r0 + kchunk, :]
            for tt in range(2):
                k_sc[r0:r0 + kchunk, tt * hd:(tt + 1) * hd] = _rope(
                    kl_ref[r0:r0 + kchunk, tt * hd:(tt + 1) * hd].astype(F32), c, s).astype(BF16)

    lam = (jnp.exp(jnp.sum(lq1_ref[...] * lk1_ref[...], axis=-1, keepdims=True))
           - jnp.exp(jnp.sum(lq2_ref[...] * lk2_ref[...], axis=-1, keepdims=True)) + lambda_init)

    ctx_chunks = [(c0, min(kchunk, nctx - c0), kc_ref, c0) for c0 in range(0, nctx, kchunk)]
    lat_chunks = [(nctx + c0, kchunk, k_sc, c0) for c0 in range(0, seq, kchunk)]

    def run(qs, chunks, nrows, o_ref, with_lat):
        for tt in range(2):
            q = (qs[tt] * qscale).astype(BF16)
            for (c0, cs, kr, r0) in chunks:
                s_sc[tt, 0:nrows, c0:c0 + cs] = _nt_dot(q, kr[r0:r0 + cs, tt * hd:(tt + 1) * hd])
        outs = []
        for tt in range(2):
            linv = []
            for r in range(0, nrows, SM_ROWS):
                rows = slice(r, r + SM_ROWS)
                m = jnp.full((SM_ROWS, 1), -jnp.inf, F32)
                for (c0, cs, _, _) in chunks:
                    m = jnp.maximum(m, jnp.max(s_sc[tt, rows, c0:c0 + cs], axis=-1, keepdims=True))
                l = jnp.zeros((SM_ROWS, 1), F32)
                for (c0, cs, _, _) in chunks:
                    e = jnp.exp2(s_sc[tt, rows, c0:c0 + cs] - m)
                    l = l + jnp.sum(e, axis=-1, keepdims=True)
                    e_sc[tt, rows, c0:c0 + cs] = e.astype(BF16)
                linv.append(1.0 / l)
            acc = jnp.dot(e_sc[tt, 0:nrows, 0:nctx], vc_ref[...], preferred_element_type=F32)
            if with_lat:
                acc = acc + jnp.dot(e_sc[tt, 0:nrows, nctx:nctx + seq], vl_ref[...], preferred_element_type=F32)
            outs.append(acc * jnp.concatenate(linv, axis=0))
        o = outs[0] - lam * outs[1]
        ms = jnp.mean(o * o, axis=-1, keepdims=True)
        o = o * lax.rsqrt(ms + EPS) * g_ref[...] * (1.0 - lambda_init)
        o_ref[...] = o.astype(o_ref.dtype)

    @pl.when(t == 0)
    def _ctx():
        run([qc_ref[:, 0:hd].astype(F32), qc_ref[:, hd:2 * hd].astype(F32)], ctx_chunks, nctx, oc_ref, False)

    @pl.when(t > 0)
    def _lat():
        p0 = pl.multiple_of((t - 1) * tq, tq)
        c, s = cos_ref[pl.ds(p0, tq), :], sin_ref[pl.ds(p0, tq), :]
        run([_rope(ql_ref[:, 0:hd].astype(F32), c, s), _rope(ql_ref[:, hd:2 * hd].astype(F32), c, s)],
            ctx_chunks + lat_chunks, tq, ol_ref, True)


def _diff_attn(p, lq1, lk1, lq2, lk2, subln_g, cos2, sin_s, nbatch, seq, nctx, lambda_init):
    hd = DIFF_HEAD_DIM
    vd = 2 * hd
    nh = p.shape[1] // (3 * vd)
    tq = 512
    nq = seq // tq
    ctx_blk = nbatch * seq // nctx
    kchunk = 512

    def lrow(b, s):
        return b * nq + jnp.maximum(s - 1, 0)

    vec = lambda a: a.reshape(1, -1)
    o_lat, o_ctx = pl.pallas_call(
        functools.partial(_diff_kernel, nctx=nctx, seq=seq, tq=tq, kchunk=kchunk, lambda_init=lambda_init),
        grid=(nbatch, nh, nq + 1),
        in_specs=[pl.BlockSpec((1, hd), lambda b, h, s: (0, 0))] * 4
        + [pl.BlockSpec((1, vd), lambda b, h, s: (0, 0)),
           pl.BlockSpec((tq, vd), lambda b, h, s: (lrow(b, s), h)),
           pl.BlockSpec((nctx, vd), lambda b, h, s: (ctx_blk + b, h)),
           pl.BlockSpec((seq, vd), lambda b, h, s: (b, nh + h)),
           pl.BlockSpec((nctx, vd), lambda b, h, s: (ctx_blk + b, nh + h)),
           pl.BlockSpec((seq, vd), lambda b, h, s: (b, 2 * nh + h)),
           pl.BlockSpec((nctx, vd), lambda b, h, s: (ctx_blk + b, 2 * nh + h)),
           pl.BlockSpec((seq, hd), lambda b, h, s: (0, 0)),
           pl.BlockSpec((seq, hd), lambda b, h, s: (0, 0))],
        out_specs=[pl.BlockSpec((tq, vd), lambda b, h, s: (lrow(b, s), h)),
                   pl.BlockSpec((nctx, vd), lambda b, h, s: (b, h))],
        out_shape=[jax.ShapeDtypeStruct((nbatch * seq, nh * vd), BF16),
                   jax.ShapeDtypeStruct((nbatch * nctx, nh * vd), BF16)],
        scratch_shapes=[pltpu.VMEM((seq, vd), BF16), pltpu.VMEM((2, tq, nctx + seq), F32),
                        pltpu.VMEM((2, tq, nctx + seq), BF16)],
        compiler_params=_cparams(("arbitrary", "arbitrary", "arbitrary"), 56),
        name="diff_attn",
    )(vec(lq1), vec(lk1), vec(lq2), vec(lk2), vec(subln_g), p, p, p, p, p, p, cos2, sin_s)
    return jnp.concatenate([o_lat, o_ctx], axis=0)


def _ssd_conv_kernel(xm_ref, xp_ref, xn_ref, bcm_ref, bcp_ref, bcn_ref, dt_ref, wx_ref, wbc_ref, bx_ref,
                     bbc_ref, dtb_ref, xs_ref, bm_ref, cm_ref, dto_ref, ext_sc, *, rows, seq, nctx, nlat_rows,
                     slab):
    r = pl.program_id(0)
    row0 = r * rows
    in_lat = row0 < nlat_rows
    seg = jnp.where(in_lat, seq, nctx)
    off = jnp.where(in_lat, row0, row0 - nlat_rows)
    is_first = (off % seg) == 0
    is_last = ((off + rows) % seg) == 0
    halo = CONV_HALO
    pad = SSD_CONV // 2
    ngrp = SSD_GROUPS
    nst = SSD_STATE

    def conv_slab(m_ref, p_ref, n_ref, w_ref, b_ref, c0):
        prev = jnp.where(is_first, 0.0, p_ref[:, c0:c0 + slab].astype(F32))
        nxt = jnp.where(is_last, 0.0, n_ref[:, c0:c0 + slab].astype(F32))
        ext_sc[0:halo, :] = prev
        ext_sc[halo:halo + rows, :] = m_ref[:, c0:c0 + slab].astype(F32)
        ext_sc[halo + rows:2 * halo + rows, :] = nxt
        acc = jnp.broadcast_to(b_ref[:, c0:c0 + slab], (rows, slab))
        for k in range(SSD_CONV):
            acc = acc + w_ref[k:k + 1, c0:c0 + slab] * ext_sc[pl.ds(halo - pad + k, rows), :]
        return acc / (1.0 + jnp.exp(-acc))

    for g in range(ngrp):
        xs_ref[g] = conv_slab(xm_ref, xp_ref, xn_ref, wx_ref, bx_ref, g * slab).astype(xs_ref.dtype)
    nbc = 2 * ngrp * nst // slab
    for sidx in range(nbc):
        y = conv_slab(bcm_ref, bcp_ref, bcn_ref, wbc_ref, bbc_ref, sidx * slab).astype(bm_ref.dtype)
        for q in range(slab // nst):
            gi = sidx * (slab // nst) + q
            if gi < ngrp:
                bm_ref[gi] = y[:, q * nst:(q + 1) * nst]
            else:
                cm_ref[gi - ngrp] = y[:, q * nst:(q + 1) * nst]
    v = dt_ref[...] + dtb_ref[...]
    dto_ref[...] = jnp.maximum(v, 0.0) + jnp.log1p(jnp.exp(-jnp.abs(v)))


def _ssd_conv(zx, dt_raw, conv_w, conv_b, dt_bias, d_inner, nbatch, seq, nctx):
    t = zx.shape[0]
    rows = 256
    slab = 512
    ngrp, nst = SSD_GROUPS, SSD_STATE
    bcw = 2 * ngrp * nst
    nh2 = dt_bias.size
    halo = CONV_HALO
    rb = rows // halo
    nhalo = t // halo
    xcol = 1
    bccol = 2 * d_inner // bcw
    wx, wbc = conv_w[:, :d_inner], conv_w[:, d_inner:]
    bx, bbc = conv_b[:d_inner].reshape(1, -1), conv_b[d_inner:].reshape(1, -1)
    prev = lambda r: jnp.maximum(r * rb - 1, 0)
    nxt = lambda r: jnp.minimum(r * rb + rb, nhalo - 1)
    return pl.pallas_call(
        functools.partial(_ssd_conv_kernel, rows=rows, seq=seq, nctx=nctx, nlat_rows=nbatch * seq, slab=slab),
        grid=(t // rows,),
        in_specs=[pl.BlockSpec((rows, d_inner), lambda r: (r, xcol)),
                  pl.BlockSpec((halo, d_inner), lambda r: (prev(r), xcol)),
                  pl.BlockSpec((halo, d_inner), lambda r: (nxt(r), xcol)),
                  pl.BlockSpec((rows, bcw), lambda r: (r, bccol)),
                  pl.BlockSpec((halo, bcw), lambda r: (prev(r), bccol)),
                  pl.BlockSpec((halo, bcw), lambda r: (nxt(r), bccol)),
                  pl.BlockSpec((rows, nh2), lambda r: (r, 0)),
                  pl.BlockSpec((SSD_CONV, d_inner), lambda r: (0, 0)),
                  pl.BlockSpec((SSD_CONV, bcw), lambda r: (0, 0)),
                  pl.BlockSpec((1, d_inner), lambda r: (0, 0)),
                  pl.BlockSpec((1, bcw), lambda r: (0, 0)),
                  pl.BlockSpec((1, nh2), lambda r: (0, 0))],
        out_specs=[pl.BlockSpec((ngrp, rows, d_inner // ngrp), lambda r: (0, r, 0)),
                   pl.BlockSpec((ngrp, rows, nst), lambda r: (0, r, 0)),
                   pl.BlockSpec((ngrp, rows, nst), lambda r: (0, r, 0)),
                   pl.BlockSpec((rows, nh2), lambda r: (r, 0))],
        out_shape=[jax.ShapeDtypeStruct((ngrp, t, d_inner // ngrp), BF16),
                   jax.ShapeDtypeStruct((ngrp, t, nst), BF16),
                   jax.ShapeDtypeStruct((ngrp, t, nst), BF16),
                   jax.ShapeDtypeStruct((t, nh2), F32)],
        scratch_shapes=[pltpu.VMEM((rows + 2 * halo, slab), F32)],
        compiler_params=_cparams(("arbitrary",), 40),
        name="ssd_conv",
    )(zx, zx, zx, zx, zx, zx, dt_raw, wx, wbc, bx, bbc, dt_bias.reshape(1, nh2))


def _ssd_scan_kernel(alog_ref, tri_ref, xf_ref, bf_ref, cf_ref, dtf_ref, xb_ref, bb_ref, cb_ref, dtb_ref,
                     yf_ref, yb_ref, st_sc, e_sc, et_sc, wt_sc, ext_sc, *, nheads):
    q = SSD_CHUNK
    ngrp = SSD_GROUPS
    hpg = nheads // ngrp
    p = SSD_HEAD_DIM

    @pl.when(pl.program_id(1) == 0)
    def _():
        st_sc[...] = jnp.zeros_like(st_sc)

    a = -jnp.exp(alog_ref[...]) * LOG2E
    tri = tri_ref[...]
    es, dts, ws = [], [], []
    for d, dt_ref in enumerate((dtf_ref, dtb_ref)):
        dt = dt_ref[:, d * nheads:(d + 1) * nheads]
        la = dt * a[:, d * nheads:(d + 1) * nheads]
        acum = jnp.dot(tri, la, preferred_element_type=F32, precision=lax.Precision.HIGHEST)
        tot = acum[q - 1:q, :]
        e = acum if d == 0 else tot - acum + la
        es.append(e)
        dts.append(dt)
        ws.append(dt * jnp.exp2(tot - e))
    e2 = jnp.concatenate(es, axis=1)
    e_sc[...] = e2
    e2t = e2.T
    et_sc[...] = (e2t - jnp.log2(jnp.concatenate(dts, axis=1).T)).reshape(2 * ngrp, hpg, q)
    wt_sc[...] = jnp.concatenate(ws, axis=1).T.reshape(2 * ngrp, hpg, q)
    tot_col = jnp.concatenate([e2t[0:nheads, q - 1:q], e2t[nheads:2 * nheads, 0:1]], axis=0)
    ext_sc[...] = jnp.broadcast_to(jnp.exp2(tot_col), (2 * nheads, q)).reshape(2 * ngrp, hpg, q)

    lane = lax.broadcasted_iota(jnp.int32, (q, 2 * nheads), 1)
    li = lax.broadcasted_iota(jnp.int32, (q, q), 0)
    si = lax.broadcasted_iota(jnp.int32, (q, q), 1)
    lo = lax.broadcasted_iota(jnp.int32, (q, 2 * p), 1) < p
    causal = (si <= li, si >= li)

    dirs = ((xf_ref, bf_ref, cf_ref, yf_ref), (xb_ref, bb_ref, cb_ref, yb_ref))

    def group(g, carry):
        pre = []
        for d, (x_ref, b_ref, c_ref, y_ref) in enumerate(dirs):
            bg = b_ref[g]
            cg = c_ref[g]
            st = st_sc[d, g]
            pre.append((_nt_dot(cg, bg),
                        bg.astype(F32).T,
                        st,
                        jnp.dot(cg, st.astype(BF16), preferred_element_type=F32),
                        et_sc[d * ngrp + g], wt_sc[d * ngrp + g], ext_sc[d * ngrp + g]))
        e_all = e_sc[...]
        for d, (x_ref, b_ref, c_ref, y_ref) in enumerate(dirs):
            cbm, bgt, st, z, et, wt, ext = pre[d]
            for jp in range(hpg // 2):
                ms, decc, bws, exts = [], [], [], []
                for j in (2 * jp, 2 * jp + 1):
                    col = d * nheads + g * hpg + j
                    sel = lane == col
                    ecol = jnp.sum(jnp.where(sel, e_all, 0.0), axis=1, keepdims=True)
                    decc.append(jnp.exp2(ecol))
                    lm = jnp.exp2(jnp.where(causal[d], ecol - et[j:j + 1, :], -jnp.inf))
                    ms.append((cbm * lm).astype(BF16))
                    bws.append((bgt * wt[j:j + 1, :]).astype(BF16))
                    exts.append(ext[j:j + 1, :])
                cols = slice(jp * 2 * p, (jp + 1) * 2 * p)
                xp = x_ref[g, :, cols]
                zero = jnp.zeros_like(xp)
                rhs = jnp.concatenate([jnp.where(lo, xp, zero), jnp.where(lo, zero, xp)], axis=0)
                y = jnp.dot(jnp.concatenate(ms, axis=1), rhs, preferred_element_type=F32)
                y = y + jnp.where(lo, decc[0], decc[1]) * z[:, cols]
                y_ref[g, :, cols] = y.astype(y_ref.dtype)
                contrib = jnp.dot(jnp.concatenate(bws, axis=1), rhs, preferred_element_type=F32)
                decay = jnp.where(lo[0:1, :], exts[0], exts[1])
                st_sc[d, g, :, cols] = st[:, cols] * decay + contrib
        return carry

    lax.fori_loop(0, ngrp, group, 0, unroll=2)


def _ssd_scan(xs_g, bm_g, cm_g, dt, a_log, nbatch, seq, nctx):
    ngrp, t, gw = xs_g.shape
    nst = bm_g.shape[2]
    q = SSD_CHUNK
    nheads = a_log.shape[1]
    hpg = nheads // ngrp
    nlat = seq // q
    nctxc = nctx // q
    latc = nbatch * nlat

    def fidx(b, s):
        return jnp.where(s < nctxc, latc + b * nctxc + s, b * nlat + s - nctxc)

    def bidx(b, s):
        return jnp.where(s < nctxc, latc + b * nctxc + nctxc - 1 - s, b * nlat + nlat - 1 - (s - nctxc))

    def specs(idx):
        return [pl.BlockSpec((ngrp, q, gw), lambda b, s: (0, idx(b, s), 0)),
                pl.BlockSpec((ngrp, q, nst), lambda b, s: (0, idx(b, s), 0)),
                pl.BlockSpec((ngrp, q, nst), lambda b, s: (0, idx(b, s), 0)),
                pl.BlockSpec((q, 2 * nheads), lambda b, s: (idx(b, s), 0))]

    tri = jnp.tril(jnp.ones((q, q), F32))
    yshape = jax.ShapeDtypeStruct((ngrp, t, gw), BF16)
    small = pltpu.VMEM((2 * ngrp, hpg, q), F32)
    return pl.pallas_call(
        functools.partial(_ssd_scan_kernel, nheads=nheads),
        grid=(nbatch, nctxc + nlat),
        in_specs=[pl.BlockSpec((1, 2 * nheads), lambda b, s: (0, 0)),
                  pl.BlockSpec((q, q), lambda b, s: (0, 0))] + specs(fidx) + specs(bidx),
        out_specs=[pl.BlockSpec((ngrp, q, gw), lambda b, s: (0, fidx(b, s), 0)),
                   pl.BlockSpec((ngrp, q, gw), lambda b, s: (0, bidx(b, s), 0))],
        out_shape=[yshape, yshape],
        scratch_shapes=[pltpu.VMEM((2, ngrp, nst, gw), F32), pltpu.VMEM((q, 2 * nheads), F32),
                        small, small, small],
        compiler_params=_cparams(("arbitrary", "arbitrary"), 48),
        name="ssd_scan",
    )(a_log.reshape(1, 2 * nheads), tri, xs_g, bm_g, cm_g, dt, xs_g, bm_g, cm_g, dt)


def _ssd_out_kernel(yf_ref, yb_ref, xs_ref, z_ref, dx_ref, ng_ref, w_ref, h_ref, mod_ref, o_ref, acc_sc, *,
                    nsteps, gps, gw):
    s = pl.program_id(1)

    @pl.when(s == 0)
    def _():
        acc_sc[...] = jnp.zeros_like(acc_sc)

    acc = acc_sc[...]
    for g in range(gps):
        sl = slice(g * gw, (g + 1) * gw)
        z = z_ref[:, sl].astype(F32)
        y = yf_ref[g].astype(F32) + yb_ref[g].astype(F32) + xs_ref[g].astype(F32) * dx_ref[:, sl]
        y = y * (z / (1.0 + jnp.exp(-z)))
        ms = jnp.mean(y * y, axis=-1, keepdims=True)
        yn = (y * lax.rsqrt(ms + EPS) * ng_ref[:, sl]).astype(BF16)
        acc = acc + jnp.dot(yn, w_ref[0, sl, :], preferred_element_type=F32)

    @pl.when(s < nsteps - 1)
    def _():
        acc_sc[...] = acc

    @pl.when(s == nsteps - 1)
    def _():
        o_ref[...] = h_ref[...] + mod_ref[0][2:3] * acc


def _ssd_out(yf, yb, xs_g, zx, d_skip, norm_g, w_out, layer, h, mods, rows, seq, nbatch, tm=512, gps=2):
    ngrp, _, gw = xs_g.shape
    d_inner = ngrp * gw
    d = w_out.shape[2]
    nsteps = ngrp // gps
    dx = jnp.repeat(d_skip, SSD_HEAD_DIM).reshape(1, d_inner)
    gspec = pl.BlockSpec((gps, tm, gw), lambda i, g: (g, i, 0))
    return pl.pallas_call(
        functools.partial(_ssd_out_kernel, nsteps=nsteps, gps=gps, gw=gw),
        grid=(rows // tm, nsteps),
        in_specs=[gspec, gspec, gspec,
                  pl.BlockSpec((tm, gps * gw), lambda i, g: (i, g)),
                  pl.BlockSpec((1, gps * gw), lambda i, g: (0, g)),
                  pl.BlockSpec((1, gps * gw), lambda i, g: (0, g)),
                  pl.BlockSpec((1, gps * gw, d), lambda i, g: (layer, g, 0)),
                  pl.BlockSpec((tm, d), lambda i, g: (i, 0)),
                  pl.BlockSpec((1, 6, d), lambda i, g: (jnp.minimum((i * tm) // seq, nbatch), 0, 0))],
        out_specs=pl.BlockSpec((tm, d), lambda i, g: (i, 0)),
        out_shape=jax.ShapeDtypeStruct((rows, d), F32),
        scratch_shapes=[pltpu.VMEM((tm, d), F32)],
        compiler_params=_cparams(("arbitrary", "arbitrary"), 48),
        name="ssd_gate_norm_out_proj",
    )(yf, yb, xs_g, zx, dx, norm_g.reshape(1, d_inner), w_out, h, mods)


def kernel(x, c, ctx, c_ctx, ada_w, ada_b, norm_mix_g, norm_mlp_g, mlp_w1, mlp_w2, ssd_w_in, ssd_conv_w, ssd_conv_b, ssd_dt_bias, ssd_a_log, ssd_d, ssd_norm_g, ssd_w_out, gqa_w_qkv, gqa_sink, gqa_w_out, diff_w_qkv, diff_lam_q1, diff_lam_k1, diff_lam_q2, diff_lam_k2, diff_subln_g, diff_w_out, final_norm_g):
    nbatch, seq, d = x.shape
    nctx = ctx.shape[1]
    depth = ada_w.shape[0]
    nlat_rows = nbatch * seq
    t = nlat_rows + nbatch * nctx

    cond8 = jnp.zeros((SUBLANES, d), F32).at[:nbatch].set(c).at[nbatch].set(c_ctx)
    mods_all = _ada_all(cond8, ada_w, ada_b)[:, :nbatch + 1].reshape(depth, nbatch + 1, 6, d)
    cos2, sin_s = _rope_tables(seq, GQA_HEAD_DIM)

    h = jnp.concatenate([x.reshape(nlat_rows, d), ctx.reshape(nbatch * nctx, d)], axis=0)
    w1_b, w2_b = mlp_w1.astype(BF16), mlp_w2.astype(BF16)
    ssd_in_b, ssd_out_b = ssd_w_in.astype(BF16), ssd_w_out.astype(BF16)
    gqa_in_b, gqa_out_b = gqa_w_qkv.astype(BF16), gqa_w_out.astype(BF16)
    diff_in_b, diff_out_b = diff_w_qkv.astype(BF16), diff_w_out.astype(BF16)
    for i in range(depth):
        kind, j = i % N_MIXERS, i // N_MIXERS
        need_ctx = i < depth - 1
        rows = t if need_ctx else nlat_rows
        mods = mods_all[i]
        if kind == 0:
            d_inner = ssd_w_out.shape[1]
            nmain = ssd_w_in.shape[2] - 2 * ssd_a_log.shape[2]
            zx, dt_raw = _proj(h, norm_mix_g[i], mods, ssd_in_b, j, nmain, 2 * ssd_a_log.shape[2], seq, nbatch)
            xs_g, bm_g, cm_g, dt = _ssd_conv(zx, dt_raw, ssd_conv_w[j], ssd_conv_b[j], ssd_dt_bias[j], d_inner,
                                             nbatch, seq, nctx)
            yf, yb = _ssd_scan(xs_g, bm_g, cm_g, dt, ssd_a_log[j], nbatch, seq, nctx)
            h = _ssd_out(yf, yb, xs_g, zx, ssd_d[j], ssd_norm_g[j], ssd_out_b, j, h, mods, rows, seq, nbatch)
        elif kind == 1:
            p = _proj(h, norm_mix_g[i], mods, gqa_in_b, j, gqa_w_qkv.shape[2], 0, seq, nbatch)
            o = _gqa(p, gqa_sink[j], cos2, sin_s, nbatch, seq, nctx)
            h = _outproj(o, gqa_out_b, j, h, mods, rows, seq, nbatch)
        else:
            lambda_init = 0.8 - 0.6 * math.exp(-0.3 * i)
            p = _proj(h, norm_mix_g[i], mods, diff_in_b, j, diff_w_qkv.shape[2], 0, seq, nbatch)
            o = _diff_attn(p, diff_lam_q1[j], diff_lam_k1[j], diff_lam_q2[j], diff_lam_k2[j], diff_subln_g[j],
                           cos2, sin_s, nbatch, seq, nctx, lambda_init)
            h = _outproj(o, diff_out_b, j, h, mods, rows, seq, nbatch)
        h = _mlp(h, norm_mlp_g[i], mods, w1_b, w2_b, i, final_norm_g, rows, seq, nbatch, final_norm=not need_ctx)
    return h[:nlat_rows].reshape(nbatch, seq, d)
```

```python
import functools
import math

import jax
import jax.numpy as jnp
from jax import lax
from jax.experimental import pallas as pl
from jax.experimental.pallas import tpu as pltpu

F32 = jnp.float32
BF16 = jnp.bfloat16

EPS = 1e-6
LOG2E = math.log2(math.e)
GRID_W = 64
ROPE_THETA = 10000.0
N_MIXERS = 3

SSD_HEAD_DIM = 64
SSD_GROUPS = 8
SSD_STATE = 128
SSD_CONV = 5
SSD_CHUNK = 128

GQA_HEAD_DIM = 128
GQA_KV_HEADS = 4
WINDOW = 128

DIFF_HEAD_DIM = 128
SM_ROWS = 128

LANES = 128
SUBLANES = 8
CONV_HALO = 16
VMEM_CAP = 56 * 1024 * 1024


def _cparams(sem, vmem_mb):
    return pltpu.CompilerParams(dimension_semantics=sem,
                                vmem_limit_bytes=min(int(vmem_mb * 1024 * 1024), VMEM_CAP))


def _nt_dot(a, b):
    return lax.dot_general(a, b, (((1,), (1,)), ((), ())), preferred_element_type=F32)


def _pick_tile(n, cap):
    best = LANES
    t = LANES
    while t <= cap:
        if n % t == 0:
            best = t
        t += LANES
    return best


def _split_specs(parts, tm, width, col):
    lat, ctx = parts
    nl = lat.shape[0] // tm
    assert lat.shape[0] % tm == 0 and ctx.shape[0] % tm == 0
    return ([pl.BlockSpec((tm, width), lambda i, j: (jnp.minimum(i, nl - 1), col(i, j))),
             pl.BlockSpec((tm, width), lambda i, j: (jnp.maximum(i - nl, 0), col(i, j)),
                          pipeline_mode=pl.Buffered(1))], nl)


def _pick_part(i, nl, lat_ref, ctx_ref):
    return jnp.where(i >= nl, ctx_ref[...], lat_ref[...])


def _norm_mod(x, g, shift, scale):
    ms = jnp.mean(x * x, axis=-1, keepdims=True)
    y = x * lax.rsqrt(ms + EPS) * g
    return y * (1.0 + scale) + shift


def _ada_kernel(cond_ref, w_ref, b_ref, o_ref):
    a = cond_ref[...]
    s = a / (1.0 + jnp.exp(-a))
    o_ref[0] = jnp.dot(s.astype(BF16), w_ref[0].astype(BF16), preferred_element_type=F32) + b_ref[0]


def _ada_all(cond8, ada_w, ada_b):
    depth, d, n = ada_w.shape
    tn = _pick_tile(n, 1024)
    return pl.pallas_call(
        _ada_kernel,
        grid=(depth, n // tn),
        in_specs=[pl.BlockSpec((SUBLANES, d), lambda l, j: (0, 0)),
                  pl.BlockSpec((1, d, tn), lambda l, j: (l, 0, j)),
                  pl.BlockSpec((1, 1, tn), lambda l, j: (l, 0, j))],
        out_specs=pl.BlockSpec((1, SUBLANES, tn), lambda l, j: (l, 0, j)),
        out_shape=jax.ShapeDtypeStruct((depth, SUBLANES, n), F32),
        compiler_params=_cparams(("arbitrary", "arbitrary"), 40),
        name="ada_mod",
    )(cond8, ada_w, ada_b.reshape(depth, 1, n))


def _proj_kernel(*refs, n_tail, split_at):
    nh = 1 if split_at is None else 2
    g_ref, mod_ref, w_ref = refs[nh:nh + 3]
    rest = refs[nh + 3:]
    if n_tail:
        wt_ref, o_ref, ot_ref, u_sc = rest
    else:
        o_ref, u_sc = rest

    @pl.when(pl.program_id(1) == 0)
    def _():
        m = mod_ref[0]
        x = refs[0][...] if split_at is None else _pick_part(pl.program_id(0), split_at, refs[0], refs[1])
        u = _norm_mod(x, g_ref[...], m[0:1], m[1:2]).astype(BF16)
        u_sc[...] = u
        if n_tail:
            ot_ref[...] = jnp.dot(u, wt_ref[0], preferred_element_type=F32)

    o_ref[...] = jnp.dot(u_sc[...], w_ref[0], preferred_element_type=F32).astype(o_ref.dtype)


def _proj(h, g, mods, w, layer, ncols, n_tail, seq, nbatch, tm=512):
    if isinstance(h, tuple):
        t, d = h[0].shape[0] + h[1].shape[0], h[0].shape[1]
        h_specs, split_at = _split_specs(h, tm, d, lambda i, j: 0)
        h_args = list(h)
    else:
        t, d = h.shape
        h_specs, split_at, h_args = [pl.BlockSpec((tm, d), lambda i, j: (i, 0))], None, [h]
    tn = _pick_tile(ncols, 3072)
    in_specs = h_specs + [pl.BlockSpec((1, d), lambda i, j: (0, 0)),
                          pl.BlockSpec((1, 6, d), lambda i, j: (jnp.minimum((i * tm) // seq, nbatch), 0, 0)),
                          pl.BlockSpec((1, d, tn), lambda i, j: (layer, 0, j))]
    out_specs = [pl.BlockSpec((tm, tn), lambda i, j: (i, j))]
    out_shape = [jax.ShapeDtypeStruct((t, ncols), BF16)]
    args = h_args + [g.reshape(1, d), mods, w]
    if n_tail:
        assert ncols % n_tail == 0
        tail_blk = ncols // n_tail
        in_specs.append(pl.BlockSpec((1, d, n_tail), lambda i, j: (layer, 0, tail_blk)))
        out_specs.append(pl.BlockSpec((tm, n_tail), lambda i, j: (i, 0)))
        out_shape.append(jax.ShapeDtypeStruct((t, n_tail), F32))
        args.append(w)
    outs = pl.pallas_call(
        functools.partial(_proj_kernel, n_tail=n_tail, split_at=split_at),
        grid=(t // tm, ncols // tn),
        in_specs=in_specs,
        out_specs=out_specs,
        out_shape=out_shape,
        scratch_shapes=[pltpu.VMEM((tm, d), BF16)],
        compiler_params=_cparams(("arbitrary", "arbitrary"), 48),
        name="norm_mod_proj",
    )(*args)
    return outs if n_tail else outs[0]


def _mlp_kernel(h_ref, g_ref, mod_ref, w1_ref, w2_ref, fg_ref, o_ref, u_sc, acc_sc, *, nk, final_norm):
    k = pl.program_id(1)

    @pl.when(k == 0)
    def _():
        m = mod_ref[0]
        u = _norm_mod(h_ref[...], g_ref[...], m[3:4], m[4:5])
        u_sc[...] = u.astype(BF16)
        acc_sc[...] = jnp.zeros_like(acc_sc)

    hk = jnp.dot(u_sc[...], w1_ref[0], preferred_element_type=F32)
    hk = jnp.square(jnp.maximum(hk, 0.0)).astype(BF16)
    acc_sc[...] += jnp.dot(hk, w2_ref[0], preferred_element_type=F32)

    @pl.when(k == nk - 1)
    def _():
        m = mod_ref[0]
        out = h_ref[...] + m[5:6] * acc_sc[...]
        if final_norm:
            ms = jnp.mean(out * out, axis=-1, keepdims=True)
            out = out * lax.rsqrt(ms + EPS) * fg_ref[...]
        o_ref[...] = out


def _mlp(h, g, mods, w1, w2, layer, final_g, rows, seq, nbatch, final_norm, tm=512, tk=1024):
    d = h.shape[1]
    hid = w1.shape[2]
    nk = hid // tk
    return pl.pallas_call(
        functools.partial(_mlp_kernel, nk=nk, final_norm=final_norm),
        grid=(rows // tm, nk),
        in_specs=[pl.BlockSpec((tm, d), lambda i, k: (i, 0)),
                  pl.BlockSpec((1, d), lambda i, k: (0, 0)),
                  pl.BlockSpec((1, 6, d), lambda i, k: (jnp.minimum((i * tm) // seq, nbatch), 0, 0)),
                  pl.BlockSpec((1, d, tk), lambda i, k: (layer, 0, k)),
                  pl.BlockSpec((1, tk, d), lambda i, k: (layer, k, 0)),
                  pl.BlockSpec((1, d), lambda i, k: (0, 0))],
        out_specs=pl.BlockSpec((tm, d), lambda i, k: (i, 0)),
        out_shape=jax.ShapeDtypeStruct((rows, d), F32),
        scratch_shapes=[pltpu.VMEM((tm, d), BF16), pltpu.VMEM((tm, d), F32)],
        compiler_params=_cparams(("arbitrary", "arbitrary"), 52),
        name="mlp",
    )(h, g.reshape(1, d), mods, w1, w2, final_g.reshape(1, d))


def _outproj_kernel(*refs, split_at):
    ny = 1 if split_at is None else 2
    w_ref, h_ref, mod_ref, o_ref = refs[ny:]
    y = refs[0][...] if split_at is None else _pick_part(pl.program_id(0), split_at, refs[0], refs[1])
    acc = jnp.dot(y, w_ref[0], preferred_element_type=F32)
    o_ref[...] = h_ref[...] + mod_ref[0][2:3] * acc


def _outproj(y, w, layer, h, mods, rows, seq, nbatch, tm=512):
    d = w.shape[2]
    tn = d
    if isinstance(y, tuple):
        kdim = y[0].shape[1]
        y_specs, split_at = _split_specs(y, tm, kdim, lambda i, j: 0)
        y_args = list(y)
    else:
        kdim = y.shape[1]
        y_specs, split_at, y_args = [pl.BlockSpec((tm, kdim), lambda i, j: (i, 0))], None, [y]
    return pl.pallas_call(
        functools.partial(_outproj_kernel, split_at=split_at),
        grid=(rows // tm, d // tn),
        in_specs=y_specs + [
                  pl.BlockSpec((1, kdim, tn), lambda i, j: (layer, 0, j)),
                  pl.BlockSpec((tm, tn), lambda i, j: (i, j)),
                  pl.BlockSpec((1, 6, tn), lambda i, j: (jnp.minimum((i * tm) // seq, nbatch), 0, j))],
        out_specs=pl.BlockSpec((tm, tn), lambda i, j: (i, j)),
        out_shape=jax.ShapeDtypeStruct((rows, d), F32),
        compiler_params=_cparams(("arbitrary", "arbitrary"), 48),
        name="out_proj",
    )(*y_args, w, h, mods)


def _rope_tables(n_tokens, head_dim):
    rows = n_tokens // GRID_W
    row = jnp.repeat(jnp.arange(rows, dtype=F32), GRID_W)
    col = jnp.tile(jnp.arange(GRID_W, dtype=F32), rows)
    n_freq = head_dim // 4
    inv_freq = ROPE_THETA ** (-jnp.arange(n_freq, dtype=F32) / n_freq)
    ang = jnp.concatenate([row[:, None] * inv_freq, col[:, None] * inv_freq], axis=-1)
    cos, sin = jnp.cos(ang), jnp.sin(ang)
    return jnp.concatenate([cos, cos], axis=-1), jnp.concatenate([-sin, sin], axis=-1)


def _rope(x, cos2, sin_s):
    return x * cos2 + pltpu.roll(x, x.shape[-1] // 2, 1) * sin_s


def _gqa_head(h, qs, segs, sink_ref, s_sc, p_sc, rep):
    blk = WINDOW
    for (c0, w, k_, _, _) in segs:
        s_sc[h, :, c0:c0 + w] = _nt_dot(qs, k_)
    linv = []
    for r in range(rep):
        rows = slice(r * blk, (r + 1) * blk)
        sink = sink_ref[h * rep + r] * LOG2E
        m = jnp.full((blk, 1), sink, F32)
        svals = []
        for (c0, w, _, _, msk) in segs:
            s = s_sc[h, rows, c0:c0 + w]
            if msk is not None:
                s = jnp.where(msk, s, -jnp.inf)
            svals.append(s)
            m = jnp.maximum(m, jnp.max(s, axis=-1, keepdims=True))
        l = jnp.exp2(sink - m)
        for (c0, w, _, _, _), s in zip(segs, svals):
            p = jnp.exp2(s - m)
            l = l + jnp.sum(p, axis=-1, keepdims=True)
            p_sc[h, rows, c0:c0 + w] = p.astype(BF16)
        linv.append(1.0 / l)
    acc = None
    for (c0, w, _, v_, _) in segs:
        pv = jnp.dot(p_sc[h, :, c0:c0 + w], v_, preferred_element_type=F32)
        acc = pv if acc is None else acc + pv
    return acc * jnp.concatenate(linv, axis=0)


def _gqa_kernel(sink_ref, q_ref, kp_ref, ko_ref, kn_ref, vp_ref, vo_ref, vn_ref, kc_ref, vc_ref,
                cos_ref, sin_ref, o_ref, s_sc, p_sc, *, nb, nctxb, rep):
    t = pl.program_id(1)
    hd = GQA_HEAD_DIM
    blk = WINDOW
    nctx = kc_ref.shape[0]
    qscale = hd ** -0.5 * LOG2E

    def write(h, o):
        for r in range(rep):
            c0 = (h * rep + r) * hd
            o_ref[:, c0:c0 + hd] = o[r * blk:(r + 1) * blk].astype(o_ref.dtype)

    @pl.when(t < nctxb)
    def _ctx():
        for h in range(GQA_KV_HEADS):
            qs = jnp.concatenate([q_ref[:, (h * rep + r) * hd:(h * rep + r + 1) * hd] for r in range(rep)], axis=0)
            qs = (qs.astype(F32) * qscale).astype(BF16)
            kc = kc_ref[:, h * hd:(h + 1) * hd]
            vc = vc_ref[:, h * hd:(h + 1) * hd]
            write(h, _gqa_head(h, qs, [(0, nctx, kc, vc, None)], sink_ref, s_sc, p_sc, rep))

    @pl.when(t >= nctxb)
    def _lat():
        n = t - nctxb
        pq = pl.multiple_of(n * blk, blk)
        pp = pl.multiple_of(jnp.maximum(n - 1, 0) * blk, blk)
        pn = pl.multiple_of(jnp.minimum(n + 1, nb - 1) * blk, blk)
        cq, sq = cos_ref[pl.ds(pq, blk), :], sin_ref[pl.ds(pq, blk), :]
        cp, sp = cos_ref[pl.ds(pp, blk), :], sin_ref[pl.ds(pp, blk), :]
        cn, sn = cos_ref[pl.ds(pn, blk), :], sin_ref[pl.ds(pn, blk), :]
        i = lax.broadcasted_iota(jnp.int32, (blk, 3 * blk), 0)
        j = lax.broadcasted_iota(jnp.int32, (blk, 3 * blk), 1)
        rel = j - blk - i
        valid = (rel <= WINDOW) & (rel >= -WINDOW)
        valid = valid & ((j >= blk) | (n >= 1)) & ((j < 2 * blk) | (n <= nb - 2))
        for h in range(GQA_KV_HEADS):
            qs = jnp.concatenate(
                [_rope(q_ref[:, (h * rep + r) * hd:(h * rep + r + 1) * hd].astype(F32), cq, sq)
                 for r in range(rep)], axis=0)
            sl = slice(h * hd, (h + 1) * hd)
            kw = jnp.concatenate([_rope(kp_ref[:, sl].astype(F32), cp, sp), _rope(ko_ref[:, sl].astype(F32), cq, sq),
                                  _rope(kn_ref[:, sl].astype(F32), cn, sn)], axis=0).astype(BF16)
            vw = jnp.concatenate([vp_ref[:, sl], vo_ref[:, sl], vn_ref[:, sl]], axis=0)
            kc = kc_ref[:, sl]
            vc = vc_ref[:, sl]
            segs = [(0, 3 * blk, kw, vw, valid), (3 * blk, nctx, kc, vc, None)]
            write(h, _gqa_head(h, (qs * qscale).astype(BF16), segs, sink_ref, s_sc, p_sc, rep))


def _gqa(p, sink, cos2, sin_s, nbatch, seq, nctx):
    t = p.shape[0]
    hd = GQA_HEAD_DIM
    nkv = GQA_KV_HEADS
    dq = p.shape[1] - 2 * nkv * hd
    rep = dq // hd // nkv
    blk = WINDOW
    nb = seq // blk
    nctxb = nctx // blk
    latb = nbatch * seq // blk
    kvw = nkv * hd
    kcol = dq // kvw
    vcol = kcol + 1

    def qrow(b, s):
        return jnp.where(s < nctxb, latb + b * nctxb + s, b * nb + s - nctxb)

    def nidx(s):
        return jnp.maximum(s - nctxb, 0)

    def kvspec(off, col):
        return pl.BlockSpec((blk, kvw), lambda b, s: (b * nb + jnp.clip(nidx(s) + off, 0, nb - 1), col))

    ctx_blk = nbatch * seq // nctx
    return pl.pallas_call(
        functools.partial(_gqa_kernel, nb=nb, nctxb=nctxb, rep=rep),
        grid=(nbatch, nctxb + nb),
        in_specs=[pl.BlockSpec(memory_space=pltpu.SMEM),
                  pl.BlockSpec((blk, dq), lambda b, s: (qrow(b, s), 0)),
                  kvspec(-1, kcol), kvspec(0, kcol), kvspec(1, kcol),
                  kvspec(-1, vcol), kvspec(0, vcol), kvspec(1, vcol),
                  pl.BlockSpec((nctx, kvw), lambda b, s: (ctx_blk + b, kcol)),
                  pl.BlockSpec((nctx, kvw), lambda b, s: (ctx_blk + b, vcol)),
                  pl.BlockSpec((seq, hd), lambda b, s: (0, 0)),
                  pl.BlockSpec((seq, hd), lambda b, s: (0, 0))],
        out_specs=pl.BlockSpec((blk, dq), lambda b, s: (qrow(b, s), 0)),
        out_shape=jax.ShapeDtypeStruct((t, dq), BF16),
        scratch_shapes=[pltpu.VMEM((nkv, rep * blk, 3 * blk + nctx), F32),
                        pltpu.VMEM((nkv, rep * blk, 3 * blk + nctx), BF16)],
        compiler_params=_cparams(("arbitrary", "arbitrary"), 40),
        name="gqa_window_attn",
    )(sink, p, p, p, p, p, p, p, p, p, cos2, sin_s)


def _diff_kernel(lq1_ref, lk1_ref, lq2_ref, lk2_ref, g_ref, ql_ref, qc_ref, kl_ref, kc_ref, vl_ref, vc_ref,
                 cos_ref, sin_ref, ol_ref, oc_ref, k_sc, s_sc, e_sc, *, nctx, seq, tq, kchunk, lambda_init):
    t = pl.program_id(2)
    hd = DIFF_HEAD_DIM
    qscale = hd ** -0.5 * LOG2E

    @pl.when(t == 0)
    def _prep():
        for r0 in range(0, seq, kchunk):
            c, s = cos_ref[r0:r0 + kchunk, :], sin_ref[r0:r0 + kchunk, :]
            for tt in range(2):
                k_sc[r0:r0 + kchunk, tt * hd:(tt + 1) * hd] = _rope(
                    kl_ref[r0:r0 + kchunk, tt * hd:(tt + 1) * hd].astype(F32), c, s).astype(BF16)

    lam = (jnp.exp(jnp.sum(lq1_ref[...] * lk1_ref[...], axis=-1, keepdims=True))
           - jnp.exp(jnp.sum(lq2_ref[...] * lk2_ref[...], axis=-1, keepdims=True)) + lambda_init)

    ctx_chunks = [(c0, min(kchunk, nctx - c0), kc_ref, c0) for c0 in range(0, nctx, kchunk)]
    lat_chunks = [(nctx + c0, kchunk, k_sc, c0) for c0 in range(0, seq, kchunk)]

    def run(qs, chunks, nrows, o_ref, with_lat):
        for tt in range(2):
            q = (qs[tt] * qscale).astype(BF16)
            for (c0, cs, kr, r0) in chunks:
                s_sc[tt, 0:nrows, c0:c0 + cs] = _nt_dot(q, kr[r0:r0 + cs, tt * hd:(tt + 1) * hd])
        outs = []
        for tt in range(2):
            linv = []
            for r in range(0, nrows, SM_ROWS):
                rows = slice(r, r + SM_ROWS)
                m = jnp.full((SM_ROWS, 1), -jnp.inf, F32)
                for (c0, cs, _, _) in chunks:
                    m = jnp.maximum(m, jnp.max(s_sc[tt, rows, c0:c0 + cs], axis=-1, keepdims=True))
                l = jnp.zeros((SM_ROWS, 1), F32)
                for (c0, cs, _, _) in chunks:
                    e = jnp.exp2(s_sc[tt, rows, c0:c0 + cs] - m)
                    l = l + jnp.sum(e, axis=-1, keepdims=True)
                    e_sc[tt, rows, c0:c0 + cs] = e.astype(BF16)
                linv.append(1.0 / l)
            acc = jnp.dot(e_sc[tt, 0:nrows, 0:nctx], vc_ref[...], preferred_element_type=F32)
            if with_lat:
                acc = acc + jnp.dot(e_sc[tt, 0:nrows, nctx:nctx + seq], vl_ref[...], preferred_element_type=F32)
            outs.append(acc * jnp.concatenate(linv, axis=0))
        o = outs[0] - lam * outs[1]
        ms = jnp.mean(o * o, axis=-1, keepdims=True)
        o = o * lax.rsqrt(ms + EPS) * g_ref[...] * (1.0 - lambda_init)
        o_ref[...] = o.astype(o_ref.dtype)

    @pl.when(t == 0)
    def _ctx():
        run([qc_ref[:, 0:hd].astype(F32), qc_ref[:, hd:2 * hd].astype(F32)], ctx_chunks, nctx, oc_ref, False)

    @pl.when(t > 0)
    def _lat():
        p0 = pl.multiple_of((t - 1) * tq, tq)
        c, s = cos_ref[pl.ds(p0, tq), :], sin_ref[pl.ds(p0, tq), :]
        run([_rope(ql_ref[:, 0:hd].astype(F32), c, s), _rope(ql_ref[:, hd:2 * hd].astype(F32), c, s)],
            ctx_chunks + lat_chunks, tq, ol_ref, True)


def _diff_attn(p, lq1, lk1, lq2, lk2, subln_g, cos2, sin_s, nbatch, seq, nctx, lambda_init):
    hd = DIFF_HEAD_DIM
    vd = 2 * hd
    nh = p.shape[1] // (3 * vd)
    tq = 512
    nq = seq // tq
    ctx_blk = nbatch * seq // nctx
    kchunk = 512

    def lrow(b, s):
        return b * nq + jnp.maximum(s - 1, 0)

    vec = lambda a: a.reshape(1, -1)
    o_lat, o_ctx = pl.pallas_call(
        functools.partial(_diff_kernel, nctx=nctx, seq=seq, tq=tq, kchunk=kchunk, lambda_init=lambda_init),
        grid=(nbatch, nh, nq + 1),
        in_specs=[pl.BlockSpec((1, hd), lambda b, h, s: (0, 0))] * 4
        + [pl.BlockSpec((1, vd), lambda b, h, s: (0, 0)),
           pl.BlockSpec((tq, vd), lambda b, h, s: (lrow(b, s), h)),
           pl.BlockSpec((nctx, vd), lambda b, h, s: (ctx_blk + b, h)),
           pl.BlockSpec((seq, vd), lambda b, h, s: (b, nh + h)),
           pl.BlockSpec((nctx, vd), lambda b, h, s: (ctx_blk + b, nh + h)),
           pl.BlockSpec((seq, vd), lambda b, h, s: (b, 2 * nh + h)),
           pl.BlockSpec((nctx, vd), lambda b, h, s: (ctx_blk + b, 2 * nh + h)),
           pl.BlockSpec((seq, hd), lambda b, h, s: (0, 0)),
           pl.BlockSpec((seq, hd), lambda b, h, s: (0, 0))],
        out_specs=[pl.BlockSpec((tq, vd), lambda b, h, s: (lrow(b, s), h)),
                   pl.BlockSpec((nctx, vd), lambda b, h, s: (b, h))],
        out_shape=[jax.ShapeDtypeStruct((nbatch * seq, nh * vd), BF16),
                   jax.ShapeDtypeStruct((nbatch * nctx, nh * vd), BF16)],
        scratch_shapes=[pltpu.VMEM((seq, vd), BF16), pltpu.VMEM((2, tq, nctx + seq), F32),
                        pltpu.VMEM((2, tq, nctx + seq), BF16)],
        compiler_params=_cparams(("arbitrary", "arbitrary", "arbitrary"), 56),
        name="diff_attn",
    )(vec(lq1), vec(lk1), vec(lq2), vec(lk2), vec(subln_g), p, p, p, p, p, p, cos2, sin_s)
    return o_lat, o_ctx


def _ssd_conv_kernel(xm_ref, xp_ref, xn_ref, bcm_ref, bcp_ref, bcn_ref, dt_ref, wx_ref, wbc_ref, bx_ref,
                     bbc_ref, dtb_ref, sh_ref, xs_ref, bm_ref, cm_ref, dto_ref, *, rows, seq, nctx, nlat_rows,
                     slab):
    r = pl.program_id(0)
    row0 = r * rows
    in_lat = row0 < nlat_rows
    seg = jnp.where(in_lat, seq, nctx)
    off = jnp.where(in_lat, row0, row0 - nlat_rows)
    is_first = (off % seg) == 0
    is_last = ((off + rows) % seg) == 0
    halo = CONV_HALO
    pad = SSD_CONV // 2
    ngrp = SSD_GROUPS
    nst = SSD_STATE
    sub = lax.broadcasted_iota(jnp.int32, (SUBLANES, slab), 0)

    def conv_slab(m_ref, p_ref, n_ref, w_ref, b_ref, c0):
        main = m_ref[:, c0:c0 + slab]
        prev = jnp.where(is_first, 0.0, p_ref[:, c0:c0 + slab].astype(F32))
        nxt = jnp.where(is_last, 0.0, n_ref[:, c0:c0 + slab].astype(F32))
        acc = b_ref[:, c0:c0 + slab] + w_ref[pad:pad + 1, c0:c0 + slab] * main.astype(F32)
        for k in range(SSD_CONV):
            dlt = k - pad
            if dlt == 0:
                continue
            sh = jnp.dot(sh_ref[k], main, preferred_element_type=F32)
            if dlt < 0:
                top = sh[0:SUBLANES]
                for q in range(-dlt):
                    top = jnp.where(sub == q, prev[halo + dlt + q:halo + dlt + q + 1, :], top)
                sh = jnp.concatenate([top, sh[SUBLANES:]], axis=0)
            else:
                bot = sh[rows - SUBLANES:]
                for q in range(dlt):
                    bot = jnp.where(sub == SUBLANES - dlt + q, nxt[q:q + 1, :], bot)
                sh = jnp.concatenate([sh[:rows - SUBLANES], bot], axis=0)
            acc = acc + w_ref[k:k + 1, c0:c0 + slab] * sh
        return acc / (1.0 + jnp.exp(-acc))

    for g in range(ngrp):
        xs_ref[g] = conv_slab(xm_ref, xp_ref, xn_ref, wx_ref, bx_ref, g * slab).astype(xs_ref.dtype)
    nbc = 2 * ngrp * nst // slab
    for sidx in range(nbc):
        y = conv_slab(bcm_ref, bcp_ref, bcn_ref, wbc_ref, bbc_ref, sidx * slab).astype(bm_ref.dtype)
        for q in range(slab // nst):
            gi = sidx * (slab // nst) + q
            if gi < ngrp:
                bm_ref[gi] = y[:, q * nst:(q + 1) * nst]
            else:
                cm_ref[gi - ngrp] = y[:, q * nst:(q + 1) * nst]
    v = dt_ref[...] + dtb_ref[...]
    dto_ref[...] = jnp.maximum(v, 0.0) + jnp.log1p(jnp.exp(-jnp.abs(v)))


def _ssd_conv(zx, dt_raw, conv_w, conv_b, dt_bias, d_inner, nbatch, seq, nctx):
    t = zx.shape[0]
    rows = 256
    slab = 512
    ngrp, nst = SSD_GROUPS, SSD_STATE
    bcw = 2 * ngrp * nst
    nh2 = dt_bias.size
    halo = CONV_HALO
    rb = rows // halo
    nhalo = t // halo
    xcol = 1
    bccol = 2 * d_inner // bcw
    wx, wbc = conv_w[:, :d_inner], conv_w[:, d_inner:]
    bx, bbc = conv_b[:d_inner].reshape(1, -1), conv_b[d_inner:].reshape(1, -1)
    shifts = jnp.stack([jnp.eye(rows, k=k - SSD_CONV // 2, dtype=BF16) for k in range(SSD_CONV)])
    prev = lambda r: jnp.maximum(r * rb - 1, 0)
    nxt = lambda r: jnp.minimum(r * rb + rb, nhalo - 1)
    return pl.pallas_call(
        functools.partial(_ssd_conv_kernel, rows=rows, seq=seq, nctx=nctx, nlat_rows=nbatch * seq, slab=slab),
        grid=(t // rows,),
        in_specs=[pl.BlockSpec((rows, d_inner), lambda r: (r, xcol)),
                  pl.BlockSpec((halo, d_inner), lambda r: (prev(r), xcol)),
                  pl.BlockSpec((halo, d_inner), lambda r: (nxt(r), xcol)),
                  pl.BlockSpec((rows, bcw), lambda r: (r, bccol)),
                  pl.BlockSpec((halo, bcw), lambda r: (prev(r), bccol)),
                  pl.BlockSpec((halo, bcw), lambda r: (nxt(r), bccol)),
                  pl.BlockSpec((rows, nh2), lambda r: (r, 0)),
                  pl.BlockSpec((SSD_CONV, d_inner), lambda r: (0, 0)),
                  pl.BlockSpec((SSD_CONV, bcw), lambda r: (0, 0)),
                  pl.BlockSpec((1, d_inner), lambda r: (0, 0)),
                  pl.BlockSpec((1, bcw), lambda r: (0, 0)),
                  pl.BlockSpec((1, nh2), lambda r: (0, 0)),
                  pl.BlockSpec((SSD_CONV, rows, rows), lambda r: (0, 0, 0))],
        out_specs=[pl.BlockSpec((ngrp, rows, d_inner // ngrp), lambda r: (0, r, 0)),
                   pl.BlockSpec((ngrp, rows, nst), lambda r: (0, r, 0)),
                   pl.BlockSpec((ngrp, rows, nst), lambda r: (0, r, 0)),
                   pl.BlockSpec((rows, nh2), lambda r: (r, 0))],
        out_shape=[jax.ShapeDtypeStruct((ngrp, t, d_inner // ngrp), BF16),
                   jax.ShapeDtypeStruct((ngrp, t, nst), BF16),
                   jax.ShapeDtypeStruct((ngrp, t, nst), BF16),
                   jax.ShapeDtypeStruct((t, nh2), F32)],
        compiler_params=_cparams(("arbitrary",), 40),
        name="ssd_conv",
    )(zx, zx, zx, zx, zx, zx, dt_raw, wx, wbc, bx, bbc, dt_bias.reshape(1, nh2), shifts)


def _ssd_scan_kernel(alog_ref, tri_ref, xf_ref, bf_ref, cf_ref, dtf_ref, xb_ref, bb_ref, cb_ref, dtb_ref,
                     yf_ref, yb_ref, st_sc, e_sc, et_sc, wt_sc, ext_sc, *, nheads):
    q = SSD_CHUNK
    ngrp = SSD_GROUPS
    hpg = nheads // ngrp
    p = SSD_HEAD_DIM

    @pl.when(pl.program_id(1) == 0)
    def _():
        st_sc[...] = jnp.zeros_like(st_sc)

    a = -jnp.exp(alog_ref[...]) * LOG2E
    tri = tri_ref[...]
    es, dts, ws = [], [], []
    for d, dt_ref in enumerate((dtf_ref, dtb_ref)):
        dt = dt_ref[:, d * nheads:(d + 1) * nheads]
        la = dt * a[:, d * nheads:(d + 1) * nheads]
        acum = jnp.dot(tri, la, preferred_element_type=F32, precision=lax.Precision.HIGHEST)
        tot = acum[q - 1:q, :]
        e = acum if d == 0 else tot - acum + la
        es.append(e)
        dts.append(dt)
        ws.append(dt * jnp.exp2(tot - e))
    e2 = jnp.concatenate(es, axis=1)
    e_sc[...] = e2
    e2t = e2.T
    et_sc[...] = (e2t - jnp.log2(jnp.concatenate(dts, axis=1).T)).reshape(2 * ngrp, hpg, q)
    wt_sc[...] = jnp.concatenate(ws, axis=1).T.reshape(2 * ngrp, hpg, q)
    tot_col = jnp.concatenate([e2t[0:nheads, q - 1:q], e2t[nheads:2 * nheads, 0:1]], axis=0)
    ext_sc[...] = jnp.broadcast_to(jnp.exp2(tot_col), (2 * nheads, q)).reshape(2 * ngrp, hpg, q)

    lane = lax.broadcasted_iota(jnp.int32, (q, 2 * nheads), 1)
    li = lax.broadcasted_iota(jnp.int32, (q, q), 0)
    si = lax.broadcasted_iota(jnp.int32, (q, q), 1)
    lo = lax.broadcasted_iota(jnp.int32, (q, 2 * p), 1) < p
    causal = (si <= li, si >= li)

    dirs = ((xf_ref, bf_ref, cf_ref, yf_ref), (xb_ref, bb_ref, cb_ref, yb_ref))

    def group(g, carry):
        pre = []
        for d, (x_ref, b_ref, c_ref, y_ref) in enumerate(dirs):
            bg = b_ref[g]
            cg = c_ref[g]
            st = st_sc[d, g]
            pre.append((_nt_dot(cg, bg),
                        bg.astype(F32).T,
                        st,
                        jnp.dot(cg, st.astype(BF16), preferred_element_type=F32),
                        et_sc[d * ngrp + g], wt_sc[d * ngrp + g], ext_sc[d * ngrp + g]))
        e_all = e_sc[...]
        for d, (x_ref, b_ref, c_ref, y_ref) in enumerate(dirs):
            cbm, bgt, st, z, et, wt, ext = pre[d]
            for jp in range(hpg // 2):
                ms, decc, bws, exts = [], [], [], []
                for j in (2 * jp, 2 * jp + 1):
                    col = d * nheads + g * hpg + j
                    sel = lane == col
                    ecol = jnp.sum(jnp.where(sel, e_all, 0.0), axis=1, keepdims=True)
                    decc.append(jnp.exp2(ecol))
                    lm = jnp.exp2(jnp.where(causal[d], ecol - et[j:j + 1, :], -jnp.inf))
                    ms.append((cbm * lm).astype(BF16))
                    bws.append((bgt * wt[j:j + 1, :]).astype(BF16))
                    exts.append(ext[j:j + 1, :])
                cols = slice(jp * 2 * p, (jp + 1) * 2 * p)
                xp = x_ref[g, :, cols]
                zero = jnp.zeros_like(xp)
                rhs = jnp.concatenate([jnp.where(lo, xp, zero), jnp.where(lo, zero, xp)], axis=0)
                y = jnp.dot(jnp.concatenate(ms, axis=1), rhs, preferred_element_type=F32)
                y = y + jnp.where(lo, decc[0], decc[1]) * z[:, cols]
                y_ref[g, :, cols] = y.astype(y_ref.dtype)
                contrib = jnp.dot(jnp.concatenate(bws, axis=1), rhs, preferred_element_type=F32)
                decay = jnp.where(lo[0:1, :], exts[0], exts[1])
                st_sc[d, g, :, cols] = st[:, cols] * decay + contrib
        return carry

    lax.fori_loop(0, ngrp, group, 0, unroll=2)


def _ssd_scan(xs_g, bm_g, cm_g, dt, a_log, nbatch, seq, nctx):
    ngrp, t, gw = xs_g.shape
    nst = bm_g.shape[2]
    q = SSD_CHUNK
    nheads = a_log.shape[1]
    hpg = nheads // ngrp
    nlat = seq // q
    nctxc = nctx // q
    latc = nbatch * nlat

    def fidx(b, s):
        return jnp.where(s < nctxc, latc + b * nctxc + s, b * nlat + s - nctxc)

    def bidx(b, s):
        return jnp.where(s < nctxc, latc + b * nctxc + nctxc - 1 - s, b * nlat + nlat - 1 - (s - nctxc))

    def specs(idx):
        return [pl.BlockSpec((ngrp, q, gw), lambda b, s: (0, idx(b, s), 0)),
                pl.BlockSpec((ngrp, q, nst), lambda b, s: (0, idx(b, s), 0)),
                pl.BlockSpec((ngrp, q, nst), lambda b, s: (0, idx(b, s), 0)),
                pl.BlockSpec((q, 2 * nheads), lambda b, s: (idx(b, s), 0))]

    tri = jnp.tril(jnp.ones((q, q), F32))
    yshape = jax.ShapeDtypeStruct((ngrp, t, gw), BF16)
    small = pltpu.VMEM((2 * ngrp, hpg, q), F32)
    return pl.pallas_call(
        functools.partial(_ssd_scan_kernel, nheads=nheads),
        grid=(nbatch, nctxc + nlat),
        in_specs=[pl.BlockSpec((1, 2 * nheads), lambda b, s: (0, 0)),
                  pl.BlockSpec((q, q), lambda b, s: (0, 0))] + specs(fidx) + specs(bidx),
        out_specs=[pl.BlockSpec((ngrp, q, gw), lambda b, s: (0, fidx(b, s), 0)),
                   pl.BlockSpec((ngrp, q, gw), lambda b, s: (0, bidx(b, s), 0))],
        out_shape=[yshape, yshape],
        scratch_shapes=[pltpu.VMEM((2, ngrp, nst, gw), F32), pltpu.VMEM((q, 2 * nheads), F32),
                        small, small, small],
        compiler_params=_cparams(("arbitrary", "arbitrary"), 48),
        name="ssd_scan",
    )(a_log.reshape(1, 2 * nheads), tri, xs_g, bm_g, cm_g, dt, xs_g, bm_g, cm_g, dt)


def _ssd_out_kernel(yf_ref, yb_ref, xs_ref, z_ref, dx_ref, ng_ref, w_ref, mod_ref, *rest, nsteps, gps, gw, split_at):
    o_ref, acc_sc = rest[-2:]
    s = pl.program_id(1)

    @pl.when(s == 0)
    def _():
        acc_sc[...] = jnp.zeros_like(acc_sc)

    acc = acc_sc[...]
    for g in range(gps):
        sl = slice(g * gw, (g + 1) * gw)
        z = z_ref[:, sl].astype(F32)
        y = yf_ref[g].astype(F32) + yb_ref[g].astype(F32) + xs_ref[g].astype(F32) * dx_ref[:, sl]
        y = y * (z / (1.0 + jnp.exp(-z)))
        ms = jnp.mean(y * y, axis=-1, keepdims=True)
        yn = (y * lax.rsqrt(ms + EPS) * ng_ref[:, sl]).astype(BF16)
        k0 = pl.multiple_of(s * (gps * gw) + g * gw, gw)
        acc = acc + jnp.dot(yn, w_ref[0, pl.ds(k0, gw), :], preferred_element_type=F32)

    @pl.when(s < nsteps - 1)
    def _():
        acc_sc[...] = acc

    @pl.when(s == nsteps - 1)
    def _():
        hres = rest[0][...] if split_at is None else _pick_part(pl.program_id(0), split_at, rest[0], rest[1])
        o_ref[...] = hres + mod_ref[0][2:3] * acc


def _ssd_out(yf, yb, xs_g, zx, d_skip, norm_g, w_out, layer, h, mods, rows, seq, nbatch, tm=512, gps=2):
    ngrp, _, gw = xs_g.shape
    d_inner = ngrp * gw
    d = w_out.shape[2]
    nsteps = ngrp // gps
    dx = jnp.repeat(d_skip, SSD_HEAD_DIM).reshape(1, d_inner)
    gspec = pl.BlockSpec((gps, tm, gw), lambda i, g: (g, i, 0))
    if isinstance(h, tuple):
        h_specs, split_at = _split_specs(h, tm, d, lambda i, g: 0)
        h_args = list(h)
    else:
        h_specs, split_at, h_args = [pl.BlockSpec((tm, d), lambda i, g: (i, 0))], None, [h]
    return pl.pallas_call(
        functools.partial(_ssd_out_kernel, nsteps=nsteps, gps=gps, gw=gw, split_at=split_at),
        grid=(rows // tm, nsteps),
        in_specs=[gspec, gspec, gspec,
                  pl.BlockSpec((tm, gps * gw), lambda i, g: (i, g)),
                  pl.BlockSpec((1, gps * gw), lambda i, g: (0, g)),
                  pl.BlockSpec((1, gps * gw), lambda i, g: (0, g)),
                  pl.BlockSpec((1, d_inner, d), lambda i, g: (layer, 0, 0), pipeline_mode=pl.Buffered(1)),
                  pl.BlockSpec((1, 6, d), lambda i, g: (jnp.minimum((i * tm) // seq, nbatch), 0, 0))] + h_specs,
        out_specs=pl.BlockSpec((tm, d), lambda i, g: (i, 0)),
        out_shape=jax.ShapeDtypeStruct((rows, d), F32),
        scratch_shapes=[pltpu.VMEM((tm, d), F32)],
        compiler_params=_cparams(("arbitrary", "arbitrary"), 56),
        name="ssd_gate_norm_out_proj",
    )(yf, yb, xs_g, zx, dx, norm_g.reshape(1, d_inner), w_out, mods, *h_args)


def kernel(x, c, ctx, c_ctx, ada_w, ada_b, norm_mix_g, norm_mlp_g, mlp_w1, mlp_w2, ssd_w_in, ssd_conv_w, ssd_conv_b, ssd_dt_bias, ssd_a_log, ssd_d, ssd_norm_g, ssd_w_out, gqa_w_qkv, gqa_sink, gqa_w_out, diff_w_qkv, diff_lam_q1, diff_lam_k1, diff_lam_q2, diff_lam_k2, diff_subln_g, diff_w_out, final_norm_g):
    nbatch, seq, d = x.shape
    nctx = ctx.shape[1]
    depth = ada_w.shape[0]
    nlat_rows = nbatch * seq
    t = nlat_rows + nbatch * nctx

    cond8 = jnp.zeros((SUBLANES, d), F32).at[:nbatch].set(c).at[nbatch].set(c_ctx)
    mods_all = _ada_all(cond8, ada_w, ada_b)[:, :nbatch + 1].reshape(depth, nbatch + 1, 6, d)
    cos2, sin_s = _rope_tables(seq, GQA_HEAD_DIM)

    h = (x.reshape(nlat_rows, d), ctx.reshape(nbatch * nctx, d))
    w1_b, w2_b = mlp_w1.astype(BF16), mlp_w2.astype(BF16)
    ssd_in_b, ssd_out_b = ssd_w_in.astype(BF16), ssd_w_out.astype(BF16)
    gqa_in_b, gqa_out_b = gqa_w_qkv.astype(BF16), gqa_w_out.astype(BF16)
    diff_in_b, diff_out_b = diff_w_qkv.astype(BF16), diff_w_out.astype(BF16)
    for i in range(depth):
        kind, j = i % N_MIXERS, i // N_MIXERS
        need_ctx = i < depth - 1
        rows = t if need_ctx else nlat_rows
        mods = mods_all[i]
        assert kind == 0 or not isinstance(h, tuple)
        if kind == 0:
            d_inner = ssd_w_out.shape[1]
            nmain = ssd_w_in.shape[2] - 2 * ssd_a_log.shape[2]
            zx, dt_raw = _proj(h, norm_mix_g[i], mods, ssd_in_b, j, nmain, 2 * ssd_a_log.shape[2], seq, nbatch)
            xs_g, bm_g, cm_g, dt = _ssd_conv(zx, dt_raw, ssd_conv_w[j], ssd_conv_b[j], ssd_dt_bias[j], d_inner,
                                             nbatch, seq, nctx)
            yf, yb = _ssd_scan(xs_g, bm_g, cm_g, dt, ssd_a_log[j], nbatch, seq, nctx)
            h = _ssd_out(yf, yb, xs_g, zx, ssd_d[j], ssd_norm_g[j], ssd_out_b, j, h, mods, rows, seq, nbatch)
        elif kind == 1:
            p = _proj(h, norm_mix_g[i], mods, gqa_in_b, j, gqa_w_qkv.shape[2], 0, seq, nbatch)
            o = _gqa(p, gqa_sink[j], cos2, sin_s, nbatch, seq, nctx)
            h = _outproj(o, gqa_out_b, j, h, mods, rows, seq, nbatch)
        else:
            lambda_init = 0.8 - 0.6 * math.exp(-0.3 * i)
            p = _proj(h, norm_mix_g[i], mods, diff_in_b, j, diff_w_qkv.shape[2], 0, seq, nbatch)
            o = _diff_attn(p, diff_lam_q1[j], diff_lam_k1[j], diff_lam_q2[j], diff_lam_k2[j], diff_subln_g[j],
                           cos2, sin_s, nbatch, seq, nctx, lambda_init)
            h = _outproj(o, diff_out_b, j, h, mods, rows, seq, nbatch)
        h = _mlp(h, norm_mlp_g[i], mods, w1_b, w2_b, i, final_norm_g, rows, seq, nbatch, final_norm=not need_ctx)
    return h[:nlat_rows].reshape(nbatch, seq, d)
```

```python
import functools
import math

import jax
import jax.numpy as jnp
from jax import lax
from jax.experimental import pallas as pl
from jax.experimental.pallas import tpu as pltpu

F32 = jnp.float32
BF16 = jnp.bfloat16

EPS = 1e-6
LOG2E = math.log2(math.e)
GRID_W = 64
ROPE_THETA = 10000.0
N_MIXERS = 3

SSD_HEAD_DIM = 64
SSD_GROUPS = 8
SSD_STATE = 128
SSD_CONV = 5
SSD_CHUNK = 128

GQA_HEAD_DIM = 128
GQA_KV_HEADS = 4
WINDOW = 128

DIFF_HEAD_DIM = 128
SM_ROWS = 128

LANES = 128
SUBLANES = 8
CONV_HALO = 16
CAST_ROW_ALIGN = 16
VMEM_CAP = 56 * 1024 * 1024


def _cparams(sem, vmem_mb):
    return pltpu.CompilerParams(dimension_semantics=sem,
                                vmem_limit_bytes=min(int(vmem_mb * 1024 * 1024), VMEM_CAP))


def _nt_dot(a, b):
    return lax.dot_general(a, b, (((1,), (1,)), ((), ())), preferred_element_type=F32)


def _pick_tile(n, cap):
    best = LANES
    t = LANES
    while t <= cap:
        if n % t == 0:
            best = t
        t += LANES
    return best


def _split_specs(parts, tm, width, col):
    lat, ctx = parts
    nl = lat.shape[0] // tm
    assert lat.shape[0] % tm == 0 and ctx.shape[0] % tm == 0
    return ([pl.BlockSpec((tm, width), lambda i, j: (jnp.minimum(i, nl - 1), col(i, j))),
             pl.BlockSpec((tm, width), lambda i, j: (jnp.maximum(i - nl, 0), col(i, j)),
                          pipeline_mode=pl.Buffered(1))], nl)


def _pick_part(i, nl, lat_ref, ctx_ref):
    return jnp.where(i >= nl, ctx_ref[...], lat_ref[...])


def _norm_mod(x, g, shift, scale):
    ms = jnp.mean(x * x, axis=-1, keepdims=True)
    y = x * lax.rsqrt(ms + EPS) * g
    return y * (1.0 + scale) + shift


def _ada_kernel(cond_ref, w_ref, b_ref, o_ref):
    a = cond_ref[...]
    s = a / (1.0 + jnp.exp(-a))
    o_ref[0] = jnp.dot(s.astype(BF16), w_ref[0].astype(BF16), preferred_element_type=F32) + b_ref[0]


def _ada_all(cond8, ada_w, ada_b):
    depth, d, n = ada_w.shape
    tn = _pick_tile(n, 1024)
    return pl.pallas_call(
        _ada_kernel,
        grid=(depth, n // tn),
        in_specs=[pl.BlockSpec((SUBLANES, d), lambda l, j: (0, 0)),
                  pl.BlockSpec((1, d, tn), lambda l, j: (l, 0, j)),
                  pl.BlockSpec((1, 1, tn), lambda l, j: (l, 0, j))],
        out_specs=pl.BlockSpec((1, SUBLANES, tn), lambda l, j: (l, 0, j)),
        out_shape=jax.ShapeDtypeStruct((depth, SUBLANES, n), F32),
        compiler_params=_cparams(("arbitrary", "arbitrary"), 40),
        name="ada_mod",
    )(cond8, ada_w, ada_b.reshape(depth, 1, n))


def _proj_kernel(*refs, n_tail, split_at):
    nh = 1 if split_at is None else 2
    g_ref, mod_ref, w_ref = refs[nh:nh + 3]
    rest = refs[nh + 3:]
    if n_tail:
        wt_ref, o_ref, ot_ref, u_sc = rest
    else:
        o_ref, u_sc = rest

    @pl.when(pl.program_id(1) == 0)
    def _():
        m = mod_ref[0]
        x = refs[0][...] if split_at is None else _pick_part(pl.program_id(0), split_at, refs[0], refs[1])
        u = _norm_mod(x, g_ref[...], m[0:1], m[1:2]).astype(BF16)
        u_sc[...] = u
        if n_tail:
            ot_ref[...] = jnp.dot(u, wt_ref[0], preferred_element_type=F32)

    o_ref[...] = jnp.dot(u_sc[...], w_ref[0], preferred_element_type=F32).astype(o_ref.dtype)


def _proj(h, g, mods, w, layer, ncols, n_tail, seq, nbatch, tm=512):
    if isinstance(h, tuple):
        t, d = h[0].shape[0] + h[1].shape[0], h[0].shape[1]
        h_specs, split_at = _split_specs(h, tm, d, lambda i, j: 0)
        h_args = list(h)
    else:
        t, d = h.shape
        h_specs, split_at, h_args = [pl.BlockSpec((tm, d), lambda i, j: (i, 0))], None, [h]
    tn = _pick_tile(ncols, 3072)
    in_specs = h_specs + [pl.BlockSpec((1, d), lambda i, j: (0, 0)),
                          pl.BlockSpec((1, 6, d), lambda i, j: (jnp.minimum((i * tm) // seq, nbatch), 0, 0)),
                          pl.BlockSpec((1, d, tn), lambda i, j: (layer, 0, j))]
    out_specs = [pl.BlockSpec((tm, tn), lambda i, j: (i, j))]
    out_shape = [jax.ShapeDtypeStruct((t, ncols), BF16)]
    args = h_args + [g.reshape(1, d), mods, w]
    if n_tail:
        assert ncols % n_tail == 0
        tail_blk = ncols // n_tail
        in_specs.append(pl.BlockSpec((1, d, n_tail), lambda i, j: (layer, 0, tail_blk)))
        out_specs.append(pl.BlockSpec((tm, n_tail), lambda i, j: (i, 0)))
        out_shape.append(jax.ShapeDtypeStruct((t, n_tail), F32))
        args.append(w)
    outs = pl.pallas_call(
        functools.partial(_proj_kernel, n_tail=n_tail, split_at=split_at),
        grid=(t // tm, ncols // tn),
        in_specs=in_specs,
        out_specs=out_specs,
        out_shape=out_shape,
        scratch_shapes=[pltpu.VMEM((tm, d), BF16)],
        compiler_params=_cparams(("arbitrary", "arbitrary"), 48),
        name="norm_mod_proj",
    )(*args)
    return outs if n_tail else outs[0]


def _mlp_kernel(h_ref, g_ref, mod_ref, w1_ref, w2_ref, fg_ref, *rest, nk, final_norm, ncast):
    cast_src, o_ref, cast_dst = rest[:ncast], rest[ncast], rest[ncast + 1:2 * ncast + 1]
    u_sc, acc_sc = rest[2 * ncast + 1:]
    k = pl.program_id(1)

    @pl.when(k == 0)
    def _():
        m = mod_ref[0]
        u = _norm_mod(h_ref[...], g_ref[...], m[3:4], m[4:5])
        u_sc[...] = u.astype(BF16)
        acc_sc[...] = jnp.zeros_like(acc_sc)

    hk = jnp.dot(u_sc[...], w1_ref[0], preferred_element_type=F32)
    hk = jnp.square(jnp.maximum(hk, 0.0)).astype(BF16)
    acc_sc[...] += jnp.dot(hk, w2_ref[0], preferred_element_type=F32)
    for src, dst in zip(cast_src, cast_dst):
        dst[...] = src[...].astype(BF16)

    @pl.when(k == nk - 1)
    def _():
        m = mod_ref[0]
        out = h_ref[...] + m[5:6] * acc_sc[...]
        if final_norm:
            ms = jnp.mean(out * out, axis=-1, keepdims=True)
            out = out * lax.rsqrt(ms + EPS) * fg_ref[...]
        o_ref[...] = out


def _mlp(h, g, mods, w1, w2, layer, final_g, rows, seq, nbatch, final_norm, casts=(), tm=512, tk=1024):
    d = h.shape[1]
    hid = w1.shape[2]
    nk = hid // tk
    nsteps = (rows // tm) * nk
    cast_in, cast_out, cast_shape = [], [], []
    for arr, idx in casts:
        _, r, c = arr.shape
        nblk = 1 << (nsteps.bit_length() - 1)
        while r % (nblk * CAST_ROW_ALIGN):
            nblk //= 2
        blk = lambda i, k, nblk=nblk: jnp.minimum(i * nk + k, nblk - 1)
        cast_in.append(pl.BlockSpec((1, r // nblk, c), lambda i, k, idx=idx, blk=blk: (idx, blk(i, k), 0)))
        cast_out.append(pl.BlockSpec((1, r // nblk, c), lambda i, k, blk=blk: (0, blk(i, k), 0)))
        cast_shape.append(jax.ShapeDtypeStruct((1, r, c), BF16))
    outs = pl.pallas_call(
        functools.partial(_mlp_kernel, nk=nk, final_norm=final_norm, ncast=len(casts)),
        grid=(rows // tm, nk),
        in_specs=[pl.BlockSpec((tm, d), lambda i, k: (i, 0)),
                  pl.BlockSpec((1, d), lambda i, k: (0, 0)),
                  pl.BlockSpec((1, 6, d), lambda i, k: (jnp.minimum((i * tm) // seq, nbatch), 0, 0)),
                  pl.BlockSpec((1, d, tk), lambda i, k: (layer, 0, k)),
                  pl.BlockSpec((1, tk, d), lambda i, k: (layer, k, 0)),
                  pl.BlockSpec((1, d), lambda i, k: (0, 0))] + cast_in,
        out_specs=[pl.BlockSpec((tm, d), lambda i, k: (i, 0))] + cast_out,
        out_shape=[jax.ShapeDtypeStruct((rows, d), F32)] + cast_shape,
        scratch_shapes=[pltpu.VMEM((tm, d), BF16), pltpu.VMEM((tm, d), F32)],
        compiler_params=_cparams(("arbitrary", "arbitrary"), 56),
        name="mlp",
    )(h, g.reshape(1, d), mods, w1, w2, final_g.reshape(1, d), *[a for a, _ in casts])
    return outs[0], list(outs[1:])


def _outproj_kernel(*refs, split_at):
    ny = 1 if split_at is None else 2
    w_ref, h_ref, mod_ref, o_ref = refs[ny:]
    y = refs[0][...] if split_at is None else _pick_part(pl.program_id(0), split_at, refs[0], refs[1])
    acc = jnp.dot(y, w_ref[0], preferred_element_type=F32)
    o_ref[...] = h_ref[...] + mod_ref[0][2:3] * acc


def _outproj(y, w, layer, h, mods, rows, seq, nbatch, tm=512):
    d = w.shape[2]
    tn = d
    if isinstance(y, tuple):
        kdim = y[0].shape[1]
        y_specs, split_at = _split_specs(y, tm, kdim, lambda i, j: 0)
        y_args = list(y)
    else:
        kdim = y.shape[1]
        y_specs, split_at, y_args = [pl.BlockSpec((tm, kdim), lambda i, j: (i, 0))], None, [y]
    return pl.pallas_call(
        functools.partial(_outproj_kernel, split_at=split_at),
        grid=(rows // tm, d // tn),
        in_specs=y_specs + [
                  pl.BlockSpec((1, kdim, tn), lambda i, j: (layer, 0, j)),
                  pl.BlockSpec((tm, tn), lambda i, j: (i, j)),
                  pl.BlockSpec((1, 6, tn), lambda i, j: (jnp.minimum((i * tm) // seq, nbatch), 0, j))],
        out_specs=pl.BlockSpec((tm, tn), lambda i, j: (i, j)),
        out_shape=jax.ShapeDtypeStruct((rows, d), F32),
        compiler_params=_cparams(("arbitrary", "arbitrary"), 48),
        name="out_proj",
    )(*y_args, w, h, mods)


def _rope_tables(n_tokens, head_dim):
    rows = n_tokens // GRID_W
    row = jnp.repeat(jnp.arange(rows, dtype=F32), GRID_W)
    col = jnp.tile(jnp.arange(GRID_W, dtype=F32), rows)
    n_freq = head_dim // 4
    inv_freq = ROPE_THETA ** (-jnp.arange(n_freq, dtype=F32) / n_freq)
    ang = jnp.concatenate([row[:, None] * inv_freq, col[:, None] * inv_freq], axis=-1)
    cos, sin = jnp.cos(ang), jnp.sin(ang)
    return jnp.concatenate([cos, cos], axis=-1), jnp.concatenate([-sin, sin], axis=-1)


def _rope(x, cos2, sin_s):
    return x * cos2 + pltpu.roll(x, x.shape[-1] // 2, 1) * sin_s


def _gqa_head(h, qs, segs, sink_ref, s_sc, p_sc, rep):
    blk = WINDOW
    for (c0, w, k_, _, _) in segs:
        s_sc[h, :, c0:c0 + w] = _nt_dot(qs, k_)
    linv = []
    for r in range(rep):
        rows = slice(r * blk, (r + 1) * blk)
        sink = sink_ref[h * rep + r] * LOG2E
        m = jnp.full((blk, 1), sink, F32)
        svals = []
        for (c0, w, _, _, msk) in segs:
            s = s_sc[h, rows, c0:c0 + w]
            if msk is not None:
                s = jnp.where(msk, s, -jnp.inf)
            svals.append(s)
            m = jnp.maximum(m, jnp.max(s, axis=-1, keepdims=True))
        l = jnp.exp2(sink - m)
        for (c0, w, _, _, _), s in zip(segs, svals):
            p = jnp.exp2(s - m)
            l = l + jnp.sum(p, axis=-1, keepdims=True)
            p_sc[h, rows, c0:c0 + w] = p.astype(BF16)
        linv.append(1.0 / l)
    acc = None
    for (c0, w, _, v_, _) in segs:
        pv = jnp.dot(p_sc[h, :, c0:c0 + w], v_, preferred_element_type=F32)
        acc = pv if acc is None else acc + pv
    return acc * jnp.concatenate(linv, axis=0)


def _gqa_kernel(sink_ref, q_ref, kp_ref, ko_ref, kn_ref, vp_ref, vo_ref, vn_ref, kc_ref, vc_ref,
                cos_ref, sin_ref, o_ref, s_sc, p_sc, *, nb, nctxb, rep):
    t = pl.program_id(1)
    hd = GQA_HEAD_DIM
    blk = WINDOW
    nctx = kc_ref.shape[0]
    qscale = hd ** -0.5 * LOG2E

    def write(h, o):
        for r in range(rep):
            c0 = (h * rep + r) * hd
            o_ref[:, c0:c0 + hd] = o[r * blk:(r + 1) * blk].astype(o_ref.dtype)

    @pl.when(t < nctxb)
    def _ctx():
        for h in range(GQA_KV_HEADS):
            qs = jnp.concatenate([q_ref[:, (h * rep + r) * hd:(h * rep + r + 1) * hd] for r in range(rep)], axis=0)
            qs = (qs.astype(F32) * qscale).astype(BF16)
            kc = kc_ref[:, h * hd:(h + 1) * hd]
            vc = vc_ref[:, h * hd:(h + 1) * hd]
            write(h, _gqa_head(h, qs, [(0, nctx, kc, vc, None)], sink_ref, s_sc, p_sc, rep))

    @pl.when(t >= nctxb)
    def _lat():
        n = t - nctxb
        pq = pl.multiple_of(n * blk, blk)
        pp = pl.multiple_of(jnp.maximum(n - 1, 0) * blk, blk)
        pn = pl.multiple_of(jnp.minimum(n + 1, nb - 1) * blk, blk)
        cq, sq = cos_ref[pl.ds(pq, blk), :], sin_ref[pl.ds(pq, blk), :]
        cp, sp = cos_ref[pl.ds(pp, blk), :], sin_ref[pl.ds(pp, blk), :]
        cn, sn = cos_ref[pl.ds(pn, blk), :], sin_ref[pl.ds(pn, blk), :]
        i = lax.broadcasted_iota(jnp.int32, (blk, 3 * blk), 0)
        j = lax.broadcasted_iota(jnp.int32, (blk, 3 * blk), 1)
        rel = j - blk - i
        valid = (rel <= WINDOW) & (rel >= -WINDOW)
        valid = valid & ((j >= blk) | (n >= 1)) & ((j < 2 * blk) | (n <= nb - 2))
        for h in range(GQA_KV_HEADS):
            qs = jnp.concatenate(
                [_rope(q_ref[:, (h * rep + r) * hd:(h * rep + r + 1) * hd].astype(F32), cq, sq)
                 for r in range(rep)], axis=0)
            sl = slice(h * hd, (h + 1) * hd)
            kw = jnp.concatenate([_rope(kp_ref[:, sl].astype(F32), cp, sp), _rope(ko_ref[:, sl].astype(F32), cq, sq),
                                  _rope(kn_ref[:, sl].astype(F32), cn, sn)], axis=0).astype(BF16)
            vw = jnp.concatenate([vp_ref[:, sl], vo_ref[:, sl], vn_ref[:, sl]], axis=0)
            kc = kc_ref[:, sl]
            vc = vc_ref[:, sl]
            segs = [(0, 3 * blk, kw, vw, valid), (3 * blk, nctx, kc, vc, None)]
            write(h, _gqa_head(h, (qs * qscale).astype(BF16), segs, sink_ref, s_sc, p_sc, rep))


def _gqa(p, sink, cos2, sin_s, nbatch, seq, nctx):
    t = p.shape[0]
    hd = GQA_HEAD_DIM
    nkv = GQA_KV_HEADS
    dq = p.shape[1] - 2 * nkv * hd
    rep = dq // hd // nkv
    blk = WINDOW
    nb = seq // blk
    nctxb = nctx // blk
    latb = nbatch * seq // blk
    kvw = nkv * hd
    kcol = dq // kvw
    vcol = kcol + 1

    def qrow(b, s):
        return jnp.where(s < nctxb, latb + b * nctxb + s, b * nb + s - nctxb)

    def nidx(s):
        return jnp.maximum(s - nctxb, 0)

    def kvspec(off, col):
        return pl.BlockSpec((blk, kvw), lambda b, s: (b * nb + jnp.clip(nidx(s) + off, 0, nb - 1), col))

    ctx_blk = nbatch * seq // nctx
    return pl.pallas_call(
        functools.partial(_gqa_kernel, nb=nb, nctxb=nctxb, rep=rep),
        grid=(nbatch, nctxb + nb),
        in_specs=[pl.BlockSpec(memory_space=pltpu.SMEM),
                  pl.BlockSpec((blk, dq), lambda b, s: (qrow(b, s), 0)),
                  kvspec(-1, kcol), kvspec(0, kcol), kvspec(1, kcol),
                  kvspec(-1, vcol), kvspec(0, vcol), kvspec(1, vcol),
                  pl.BlockSpec((nctx, kvw), lambda b, s: (ctx_blk + b, kcol)),
                  pl.BlockSpec((nctx, kvw), lambda b, s: (ctx_blk + b, vcol)),
                  pl.BlockSpec((seq, hd), lambda b, s: (0, 0)),
                  pl.BlockSpec((seq, hd), lambda b, s: (0, 0))],
        out_specs=pl.BlockSpec((blk, dq), lambda b, s: (qrow(b, s), 0)),
        out_shape=jax.ShapeDtypeStruct((t, dq), BF16),
        scratch_shapes=[pltpu.VMEM((nkv, rep * blk, 3 * blk + nctx), F32),
                        pltpu.VMEM((nkv, rep * blk, 3 * blk + nctx), BF16)],
        compiler_params=_cparams(("arbitrary", "arbitrary"), 40),
        name="gqa_window_attn",
    )(sink, p, p, p, p, p, p, p, p, p, cos2, sin_s)


def _diff_kernel(lq1_ref, lk1_ref, lq2_ref, lk2_ref, g_ref, ql_ref, qc_ref, kl_ref, kc_ref, vl_ref, vc_ref,
                 cos_ref, sin_ref, ol_ref, oc_ref, k_sc, s_sc, e_sc, *, nctx, seq, tq, kchunk, lambda_init):
    t = pl.program_id(2)
    hd = DIFF_HEAD_DIM
    qscale = hd ** -0.5 * LOG2E

    @pl.when(t == 0)
    def _prep():
        for r0 in range(0, seq, kchunk):
            c, s = cos_ref[r0:r0 + kchunk, :], sin_ref[r0:r0 + kchunk, :]
            for tt in range(2):
                k_sc[r0:r0 + kchunk, tt * hd:(tt + 1) * hd] = _rope(
                    kl_ref[r0:r0 + kchunk, tt * hd:(tt + 1) * hd].astype(F32), c, s).astype(BF16)

    lam = (jnp.exp(jnp.sum(lq1_ref[...] * lk1_ref[...], axis=-1, keepdims=True))
           - jnp.exp(jnp.sum(lq2_ref[...] * lk2_ref[...], axis=-1, keepdims=True)) + lambda_init)

    ctx_chunks = [(c0, min(kchunk, nctx - c0), kc_ref, c0) for c0 in range(0, nctx, kchunk)]
    lat_chunks = [(nctx + c0, kchunk, k_sc, c0) for c0 in range(0, seq, kchunk)]

    def run(qs, chunks, nrows, o_ref, with_lat):
        for tt in range(2):
            q = (qs[tt] * qscale).astype(BF16)
            for (c0, cs, kr, r0) in chunks:
                s_sc[tt, 0:nrows, c0:c0 + cs] = _nt_dot(q, kr[r0:r0 + cs, tt * hd:(tt + 1) * hd])
        outs = []
        for tt in range(2):
            linv = []
            for r in range(0, nrows, SM_ROWS):
                rows = slice(r, r + SM_ROWS)
                m = jnp.full((SM_ROWS, 1), -jnp.inf, F32)
                for (c0, cs, _, _) in chunks:
                    m = jnp.maximum(m, jnp.max(s_sc[tt, rows, c0:c0 + cs], axis=-1, keepdims=True))
                l = jnp.zeros((SM_ROWS, 1), F32)
                for (c0, cs, _, _) in chunks:
                    e = jnp.exp2(s_sc[tt, rows, c0:c0 + cs] - m)
                    l = l + jnp.sum(e, axis=-1, keepdims=True)
                    e_sc[tt, rows, c0:c0 + cs] = e.astype(BF16)
                linv.append(1.0 / l)
            acc = jnp.dot(e_sc[tt, 0:nrows, 0:nctx], vc_ref[...], preferred_element_type=F32)
            if with_lat:
                acc = acc + jnp.dot(e_sc[tt, 0:nrows, nctx:nctx + seq], vl_ref[...], preferred_element_type=F32)
            outs.append(acc * jnp.concatenate(linv, axis=0))
        o = outs[0] - lam * outs[1]
        ms = jnp.mean(o * o, axis=-1, keepdims=True)
        o = o * lax.rsqrt(ms + EPS) * g_ref[...] * (1.0 - lambda_init)
        o_ref[...] = o.astype(o_ref.dtype)

    @pl.when(t == 0)
    def _ctx():
        run([qc_ref[:, 0:hd].astype(F32), qc_ref[:, hd:2 * hd].astype(F32)], ctx_chunks, nctx, oc_ref, False)

    @pl.when(t > 0)
    def _lat():
        p0 = pl.multiple_of((t - 1) * tq, tq)
        c, s = cos_ref[pl.ds(p0, tq), :], sin_ref[pl.ds(p0, tq), :]
        run([_rope(ql_ref[:, 0:hd].astype(F32), c, s), _rope(ql_ref[:, hd:2 * hd].astype(F32), c, s)],
            ctx_chunks + lat_chunks, tq, ol_ref, True)


def _diff_attn(p, lq1, lk1, lq2, lk2, subln_g, cos2, sin_s, nbatch, seq, nctx, lambda_init):
    hd = DIFF_HEAD_DIM
    vd = 2 * hd
    nh = p.shape[1] // (3 * vd)
    tq = 512
    nq = seq // tq
    ctx_blk = nbatch * seq // nctx
    kchunk = 512

    def lrow(b, s):
        return b * nq + jnp.maximum(s - 1, 0)

    vec = lambda a: a.reshape(1, -1)
    o_lat, o_ctx = pl.pallas_call(
        functools.partial(_diff_kernel, nctx=nctx, seq=seq, tq=tq, kchunk=kchunk, lambda_init=lambda_init),
        grid=(nbatch, nh, nq + 1),
        in_specs=[pl.BlockSpec((1, hd), lambda b, h, s: (0, 0))] * 4
        + [pl.BlockSpec((1, vd), lambda b, h, s: (0, 0)),
           pl.BlockSpec((tq, vd), lambda b, h, s: (lrow(b, s), h)),
           pl.BlockSpec((nctx, vd), lambda b, h, s: (ctx_blk + b, h)),
           pl.BlockSpec((seq, vd), lambda b, h, s: (b, nh + h)),
           pl.BlockSpec((nctx, vd), lambda b, h, s: (ctx_blk + b, nh + h)),
           pl.BlockSpec((seq, vd), lambda b, h, s: (b, 2 * nh + h)),
           pl.BlockSpec((nctx, vd), lambda b, h, s: (ctx_blk + b, 2 * nh + h)),
           pl.BlockSpec((seq, hd), lambda b, h, s: (0, 0)),
           pl.BlockSpec((seq, hd), lambda b, h, s: (0, 0))],
        out_specs=[pl.BlockSpec((tq, vd), lambda b, h, s: (lrow(b, s), h)),
                   pl.BlockSpec((nctx, vd), lambda b, h, s: (b, h))],
        out_shape=[jax.ShapeDtypeStruct((nbatch * seq, nh * vd), BF16),
                   jax.ShapeDtypeStruct((nbatch * nctx, nh * vd), BF16)],
        scratch_shapes=[pltpu.VMEM((seq, vd), BF16), pltpu.VMEM((2, tq, nctx + seq), F32),
                        pltpu.VMEM((2, tq, nctx + seq), BF16)],
        compiler_params=_cparams(("arbitrary", "arbitrary", "arbitrary"), 56),
        name="diff_attn",
    )(vec(lq1), vec(lk1), vec(lq2), vec(lk2), vec(subln_g), p, p, p, p, p, p, cos2, sin_s)
    return o_lat, o_ctx


def _ssd_conv_kernel(xm_ref, xp_ref, xn_ref, bcm_ref, bcp_ref, bcn_ref, dt_ref, wx_ref, wbc_ref, bx_ref,
                     bbc_ref, dtb_ref, sh_ref, xs_ref, bm_ref, cm_ref, dto_ref, *, rows, seq, nctx, nlat_rows,
                     slab):
    r = pl.program_id(0)
    row0 = r * rows
    in_lat = row0 < nlat_rows
    seg = jnp.where(in_lat, seq, nctx)
    off = jnp.where(in_lat, row0, row0 - nlat_rows)
    is_first = (off % seg) == 0
    is_last = ((off + rows) % seg) == 0
    halo = CONV_HALO
    pad = SSD_CONV // 2
    ngrp = SSD_GROUPS
    nst = SSD_STATE
    sub = lax.broadcasted_iota(jnp.int32, (SUBLANES, slab), 0)

    def conv_slab(m_ref, p_ref, n_ref, w_ref, b_ref, c0):
        main = m_ref[:, c0:c0 + slab]
        prev = jnp.where(is_first, 0.0, p_ref[:, c0:c0 + slab].astype(F32))
        nxt = jnp.where(is_last, 0.0, n_ref[:, c0:c0 + slab].astype(F32))
        acc = b_ref[:, c0:c0 + slab] + w_ref[pad:pad + 1, c0:c0 + slab] * main.astype(F32)
        for k in range(SSD_CONV):
            dlt = k - pad
            if dlt == 0:
                continue
            sh = jnp.dot(sh_ref[k], main, preferred_element_type=F32)
            if dlt < 0:
                top = sh[0:SUBLANES]
                for q in range(-dlt):
                    top = jnp.where(sub == q, prev[halo + dlt + q:halo + dlt + q + 1, :], top)
                sh = jnp.concatenate([top, sh[SUBLANES:]], axis=0)
            else:
                bot = sh[rows - SUBLANES:]
                for q in range(dlt):
                    bot = jnp.where(sub == SUBLANES - dlt + q, nxt[q:q + 1, :], bot)
                sh = jnp.concatenate([sh[:rows - SUBLANES], bot], axis=0)
            acc = acc + w_ref[k:k + 1, c0:c0 + slab] * sh
        return acc / (1.0 + jnp.exp(-acc))

    for g in range(ngrp):
        xs_ref[g] = conv_slab(xm_ref, xp_ref, xn_ref, wx_ref, bx_ref, g * slab).astype(xs_ref.dtype)
    nbc = 2 * ngrp * nst // slab
    for sidx in range(nbc):
        y = conv_slab(bcm_ref, bcp_ref, bcn_ref, wbc_ref, bbc_ref, sidx * slab).astype(bm_ref.dtype)
        for q in range(slab // nst):
            gi = sidx * (slab // nst) + q
            if gi < ngrp:
                bm_ref[gi] = y[:, q * nst:(q + 1) * nst]
            else:
                cm_ref[gi - ngrp] = y[:, q * nst:(q + 1) * nst]
    v = dt_ref[...] + dtb_ref[...]
    dto_ref[...] = jnp.maximum(v, 0.0) + jnp.log1p(jnp.exp(-jnp.abs(v)))


def _ssd_conv(zx, dt_raw, conv_w, conv_b, dt_bias, d_inner, nbatch, seq, nctx):
    t = zx.shape[0]
    rows = 256
    slab = 512
    ngrp, nst = SSD_GROUPS, SSD_STATE
    bcw = 2 * ngrp * nst
    nh2 = dt_bias.size
    halo = CONV_HALO
    rb = rows // halo
    nhalo = t // halo
    xcol = 1
    bccol = 2 * d_inner // bcw
    wx, wbc = conv_w[:, :d_inner], conv_w[:, d_inner:]
    bx, bbc = conv_b[:d_inner].reshape(1, -1), conv_b[d_inner:].reshape(1, -1)
    shifts = jnp.stack([jnp.eye(rows, k=k - SSD_CONV // 2, dtype=BF16) for k in range(SSD_CONV)])
    prev = lambda r: jnp.maximum(r * rb - 1, 0)
    nxt = lambda r: jnp.minimum(r * rb + rb, nhalo - 1)
    return pl.pallas_call(
        functools.partial(_ssd_conv_kernel, rows=rows, seq=seq, nctx=nctx, nlat_rows=nbatch * seq, slab=slab),
        grid=(t // rows,),
        in_specs=[pl.BlockSpec((rows, d_inner), lambda r: (r, xcol)),
                  pl.BlockSpec((halo, d_inner), lambda r: (prev(r), xcol)),
                  pl.BlockSpec((halo, d_inner), lambda r: (nxt(r), xcol)),
                  pl.BlockSpec((rows, bcw), lambda r: (r, bccol)),
                  pl.BlockSpec((halo, bcw), lambda r: (prev(r), bccol)),
                  pl.BlockSpec((halo, bcw), lambda r: (nxt(r), bccol)),
                  pl.BlockSpec((rows, nh2), lambda r: (r, 0)),
                  pl.BlockSpec((SSD_CONV, d_inner), lambda r: (0, 0)),
                  pl.BlockSpec((SSD_CONV, bcw), lambda r: (0, 0)),
                  pl.BlockSpec((1, d_inner), lambda r: (0, 0)),
                  pl.BlockSpec((1, bcw), lambda r: (0, 0)),
                  pl.BlockSpec((1, nh2), lambda r: (0, 0)),
                  pl.BlockSpec((SSD_CONV, rows, rows), lambda r: (0, 0, 0))],
        out_specs=[pl.BlockSpec((ngrp, rows, d_inner // ngrp), lambda r: (0, r, 0)),
                   pl.BlockSpec((ngrp, rows, nst), lambda r: (0, r, 0)),
                   pl.BlockSpec((ngrp, rows, nst), lambda r: (0, r, 0)),
                   pl.BlockSpec((rows, nh2), lambda r: (r, 0))],
        out_shape=[jax.ShapeDtypeStruct((ngrp, t, d_inner // ngrp), BF16),
                   jax.ShapeDtypeStruct((ngrp, t, nst), BF16),
                   jax.ShapeDtypeStruct((ngrp, t, nst), BF16),
                   jax.ShapeDtypeStruct((t, nh2), F32)],
        compiler_params=_cparams(("arbitrary",), 40),
        name="ssd_conv",
    )(zx, zx, zx, zx, zx, zx, dt_raw, wx, wbc, bx, bbc, dt_bias.reshape(1, nh2), shifts)


def _ssd_scan_kernel(alog_ref, tri_ref, xf_ref, bf_ref, cf_ref, dtf_ref, xb_ref, bb_ref, cb_ref, dtb_ref,
                     yf_ref, yb_ref, st_sc, e_sc, et_sc, wt_sc, ext_sc, *, nheads):
    q = SSD_CHUNK
    ngrp = SSD_GROUPS
    hpg = nheads // ngrp
    p = SSD_HEAD_DIM

    @pl.when(pl.program_id(1) == 0)
    def _():
        st_sc[...] = jnp.zeros_like(st_sc)

    a = -jnp.exp(alog_ref[...]) * LOG2E
    tri = tri_ref[...]
    es, dts, ws = [], [], []
    for d, dt_ref in enumerate((dtf_ref, dtb_ref)):
        dt = dt_ref[:, d * nheads:(d + 1) * nheads]
        la = dt * a[:, d * nheads:(d + 1) * nheads]
        acum = jnp.dot(tri, la, preferred_element_type=F32, precision=lax.Precision.HIGHEST)
        tot = acum[q - 1:q, :]
        e = acum if d == 0 else tot - acum + la
        es.append(e)
        dts.append(dt)
        ws.append(dt * jnp.exp2(tot - e))
    e2 = jnp.concatenate(es, axis=1)
    e_sc[...] = e2
    e2t = e2.T
    et_sc[...] = (e2t - jnp.log2(jnp.concatenate(dts, axis=1).T)).reshape(2 * ngrp, hpg, q)
    wt_sc[...] = jnp.concatenate(ws, axis=1).T.reshape(2 * ngrp, hpg, q)
    tot_col = jnp.concatenate([e2t[0:nheads, q - 1:q], e2t[nheads:2 * nheads, 0:1]], axis=0)
    ext_sc[...] = jnp.broadcast_to(jnp.exp2(tot_col), (2 * nheads, q)).reshape(2 * ngrp, hpg, q)

    lane = lax.broadcasted_iota(jnp.int32, (q, 2 * nheads), 1)
    li = lax.broadcasted_iota(jnp.int32, (q, q), 0)
    si = lax.broadcasted_iota(jnp.int32, (q, q), 1)
    lo = lax.broadcasted_iota(jnp.int32, (q, 2 * p), 1) < p
    causal = (si <= li, si >= li)

    dirs = ((xf_ref, bf_ref, cf_ref, yf_ref), (xb_ref, bb_ref, cb_ref, yb_ref))

    def group(g, carry):
        pre = []
        for d, (x_ref, b_ref, c_ref, y_ref) in enumerate(dirs):
            bg = b_ref[g]
            cg = c_ref[g]
            st = st_sc[d, g]
            pre.append((_nt_dot(cg, bg),
                        bg.astype(F32).T,
                        st,
                        jnp.dot(cg, st.astype(BF16), preferred_element_type=F32),
                        et_sc[d * ngrp + g], wt_sc[d * ngrp + g], ext_sc[d * ngrp + g]))
        e_all = e_sc[...]
        for d, (x_ref, b_ref, c_ref, y_ref) in enumerate(dirs):
            cbm, bgt, st, z, et, wt, ext = pre[d]
            for jp in range(hpg // 2):
                ms, decc, bws, exts = [], [], [], []
                for j in (2 * jp, 2 * jp + 1):
                    col = d * nheads + g * hpg + j
                    sel = lane == col
                    ecol = jnp.sum(jnp.where(sel, e_all, 0.0), axis=1, keepdims=True)
                    decc.append(jnp.exp2(ecol))
                    lm = jnp.exp2(jnp.where(causal[d], ecol - et[j:j + 1, :], -jnp.inf))
                    ms.append((cbm * lm).astype(BF16))
                    bws.append((bgt * wt[j:j + 1, :]).astype(BF16))
                    exts.append(ext[j:j + 1, :])
                cols = slice(jp * 2 * p, (jp + 1) * 2 * p)
                xp = x_ref[g, :, cols]
                zero = jnp.zeros_like(xp)
                rhs = jnp.concatenate([jnp.where(lo, xp, zero), jnp.where(lo, zero, xp)], axis=0)
                y = jnp.dot(jnp.concatenate(ms, axis=1), rhs, preferred_element_type=F32)
                y = y + jnp.where(lo, decc[0], decc[1]) * z[:, cols]
                y_ref[g, :, cols] = y.astype(y_ref.dtype)
                contrib = jnp.dot(jnp.concatenate(bws, axis=1), rhs, preferred_element_type=F32)
                decay = jnp.where(lo[0:1, :], exts[0], exts[1])
                st_sc[d, g, :, cols] = st[:, cols] * decay + contrib
        return carry

    lax.fori_loop(0, ngrp, group, 0, unroll=2)


def _ssd_scan(xs_g, bm_g, cm_g, dt, a_log, nbatch, seq, nctx):
    ngrp, t, gw = xs_g.shape
    nst = bm_g.shape[2]
    q = SSD_CHUNK
    nheads = a_log.shape[1]
    hpg = nheads // ngrp
    nlat = seq // q
    nctxc = nctx // q
    latc = nbatch * nlat

    def fidx(b, s):
        return jnp.where(s < nctxc, latc + b * nctxc + s, b * nlat + s - nctxc)

    def bidx(b, s):
        return jnp.where(s < nctxc, latc + b * nctxc + nctxc - 1 - s, b * nlat + nlat - 1 - (s - nctxc))

    def specs(idx):
        return [pl.BlockSpec((ngrp, q, gw), lambda b, s: (0, idx(b, s), 0)),
                pl.BlockSpec((ngrp, q, nst), lambda b, s: (0, idx(b, s), 0)),
                pl.BlockSpec((ngrp, q, nst), lambda b, s: (0, idx(b, s), 0)),
                pl.BlockSpec((q, 2 * nheads), lambda b, s: (idx(b, s), 0))]

    tri = jnp.tril(jnp.ones((q, q), F32))
    yshape = jax.ShapeDtypeStruct((ngrp, t, gw), BF16)
    small = pltpu.VMEM((2 * ngrp, hpg, q), F32)
    return pl.pallas_call(
        functools.partial(_ssd_scan_kernel, nheads=nheads),
        grid=(nbatch, nctxc + nlat),
        in_specs=[pl.BlockSpec((1, 2 * nheads), lambda b, s: (0, 0)),
                  pl.BlockSpec((q, q), lambda b, s: (0, 0))] + specs(fidx) + specs(bidx),
        out_specs=[pl.BlockSpec((ngrp, q, gw), lambda b, s: (0, fidx(b, s), 0)),
                   pl.BlockSpec((ngrp, q, gw), lambda b, s: (0, bidx(b, s), 0))],
        out_shape=[yshape, yshape],
        scratch_shapes=[pltpu.VMEM((2, ngrp, nst, gw), F32), pltpu.VMEM((q, 2 * nheads), F32),
                        small, small, small],
        compiler_params=_cparams(("arbitrary", "arbitrary"), 48),
        name="ssd_scan",
    )(a_log.reshape(1, 2 * nheads), tri, xs_g, bm_g, cm_g, dt, xs_g, bm_g, cm_g, dt)


def _ssd_out_kernel(yf_ref, yb_ref, xs_ref, z_ref, dx_ref, ng_ref, w_ref, mod_ref, *rest, nsteps, gps, gw, split_at):
    o_ref, acc_sc = rest[-2:]
    s = pl.program_id(1)

    @pl.when(s == 0)
    def _():
        acc_sc[...] = jnp.zeros_like(acc_sc)

    acc = acc_sc[...]
    for g in range(gps):
        sl = slice(g * gw, (g + 1) * gw)
        z = z_ref[:, sl].astype(F32)
        y = yf_ref[g].astype(F32) + yb_ref[g].astype(F32) + xs_ref[g].astype(F32) * dx_ref[:, sl]
        y = y * (z / (1.0 + jnp.exp(-z)))
        ms = jnp.mean(y * y, axis=-1, keepdims=True)
        yn = (y * lax.rsqrt(ms + EPS) * ng_ref[:, sl]).astype(BF16)
        k0 = pl.multiple_of(s * (gps * gw) + g * gw, gw)
        acc = acc + jnp.dot(yn, w_ref[0, pl.ds(k0, gw), :], preferred_element_type=F32)

    @pl.when(s < nsteps - 1)
    def _():
        acc_sc[...] = acc

    @pl.when(s == nsteps - 1)
    def _():
        hres = rest[0][...] if split_at is None else _pick_part(pl.program_id(0), split_at, rest[0], rest[1])
        o_ref[...] = hres + mod_ref[0][2:3] * acc


def _ssd_out(yf, yb, xs_g, zx, d_skip, norm_g, w_out, layer, h, mods, rows, seq, nbatch, tm=512, gps=2):
    ngrp, _, gw = xs_g.shape
    d_inner = ngrp * gw
    d = w_out.shape[2]
    nsteps = ngrp // gps
    dx = jnp.repeat(d_skip, SSD_HEAD_DIM).reshape(1, d_inner)
    gspec = pl.BlockSpec((gps, tm, gw), lambda i, g: (g, i, 0))
    if isinstance(h, tuple):
        h_specs, split_at = _split_specs(h, tm, d, lambda i, g: 0)
        h_args = list(h)
    else:
        h_specs, split_at, h_args = [pl.BlockSpec((tm, d), lambda i, g: (i, 0))], None, [h]
    return pl.pallas_call(
        functools.partial(_ssd_out_kernel, nsteps=nsteps, gps=gps, gw=gw, split_at=split_at),
        grid=(rows // tm, nsteps),
        in_specs=[gspec, gspec, gspec,
                  pl.BlockSpec((tm, gps * gw), lambda i, g: (i, g)),
                  pl.BlockSpec((1, gps * gw), lambda i, g: (0, g)),
                  pl.BlockSpec((1, gps * gw), lambda i, g: (0, g)),
                  pl.BlockSpec((1, d_inner, d), lambda i, g: (layer, 0, 0), pipeline_mode=pl.Buffered(1)),
                  pl.BlockSpec((1, 6, d), lambda i, g: (jnp.minimum((i * tm) // seq, nbatch), 0, 0))] + h_specs,
        out_specs=pl.BlockSpec((tm, d), lambda i, g: (i, 0)),
        out_shape=jax.ShapeDtypeStruct((rows, d), F32),
        scratch_shapes=[pltpu.VMEM((tm, d), F32)],
        compiler_params=_cparams(("arbitrary", "arbitrary"), 56),
        name="ssd_gate_norm_out_proj",
    )(yf, yb, xs_g, zx, dx, norm_g.reshape(1, d_inner), w_out, mods, *h_args)


def kernel(x, c, ctx, c_ctx, ada_w, ada_b, norm_mix_g, norm_mlp_g, mlp_w1, mlp_w2, ssd_w_in, ssd_conv_w, ssd_conv_b, ssd_dt_bias, ssd_a_log, ssd_d, ssd_norm_g, ssd_w_out, gqa_w_qkv, gqa_sink, gqa_w_out, diff_w_qkv, diff_lam_q1, diff_lam_k1, diff_lam_q2, diff_lam_k2, diff_subln_g, diff_w_out, final_norm_g):
    nbatch, seq, d = x.shape
    nctx = ctx.shape[1]
    depth = ada_w.shape[0]
    nlat_rows = nbatch * seq
    t = nlat_rows + nbatch * nctx

    cond8 = jnp.zeros((SUBLANES, d), F32).at[:nbatch].set(c).at[nbatch].set(c_ctx)
    mods_all = _ada_all(cond8, ada_w, ada_b)[:, :nbatch + 1].reshape(depth, nbatch + 1, 6, d)
    cos2, sin_s = _rope_tables(seq, GQA_HEAD_DIM)

    h = (x.reshape(nlat_rows, d), ctx.reshape(nbatch * nctx, d))

    def layer_weights(i):
        mix = ((ssd_w_in, ssd_w_out), (gqa_w_qkv, gqa_w_out), (diff_w_qkv, diff_w_out))[i % N_MIXERS]
        return [(mlp_w1, i), (mlp_w2, i), (mix[0], i // N_MIXERS), (mix[1], i // N_MIXERS)]

    wcur = [a[idx:idx + 1].astype(BF16) for a, idx in layer_weights(0)]
    for i in range(depth):
        kind, j = i % N_MIXERS, i // N_MIXERS
        need_ctx = i < depth - 1
        rows = t if need_ctx else nlat_rows
        mods = mods_all[i]
        w1_b, w2_b, win_b, wout_b = wcur
        assert kind == 0 or not isinstance(h, tuple)
        if kind == 0:
            nh2 = 2 * ssd_a_log.shape[2]
            zx, dt_raw = _proj(h, norm_mix_g[i], mods, win_b, 0, ssd_w_in.shape[2] - nh2, nh2, seq, nbatch)
            xs_g, bm_g, cm_g, dt = _ssd_conv(zx, dt_raw, ssd_conv_w[j], ssd_conv_b[j], ssd_dt_bias[j],
                                             ssd_w_out.shape[1], nbatch, seq, nctx)
            yf, yb = _ssd_scan(xs_g, bm_g, cm_g, dt, ssd_a_log[j], nbatch, seq, nctx)
            h = _ssd_out(yf, yb, xs_g, zx, ssd_d[j], ssd_norm_g[j], wout_b, 0, h, mods, rows, seq, nbatch)
        elif kind == 1:
            p = _proj(h, norm_mix_g[i], mods, win_b, 0, gqa_w_qkv.shape[2], 0, seq, nbatch)
            o = _gqa(p, gqa_sink[j], cos2, sin_s, nbatch, seq, nctx)
            h = _outproj(o, wout_b, 0, h, mods, rows, seq, nbatch)
        else:
            lambda_init = 0.8 - 0.6 * math.exp(-0.3 * i)
            p = _proj(h, norm_mix_g[i], mods, win_b, 0, diff_w_qkv.shape[2], 0, seq, nbatch)
            o = _diff_attn(p, diff_lam_q1[j], diff_lam_k1[j], diff_lam_q2[j], diff_lam_k2[j], diff_subln_g[j],
                           cos2, sin_s, nbatch, seq, nctx, lambda_init)
            h = _outproj(o, wout_b, 0, h, mods, rows, seq, nbatch)
        h, wcur = _mlp(h, norm_mlp_g[i], mods, w1_b, w2_b, 0, final_norm_g, rows, seq, nbatch,
                       final_norm=not need_ctx, casts=layer_weights(i + 1) if need_ctx else ())
    return h[:nlat_rows].reshape(nbatch, seq, d)
```

```python
import functools
import math

import jax
import jax.numpy as jnp
from jax import lax
from jax.experimental import pallas as pl
from jax.experimental.pallas import tpu as pltpu

F32 = jnp.float32
BF16 = jnp.bfloat16

EPS = 1e-6
LOG2E = math.log2(math.e)
GRID_W = 64
ROPE_THETA = 10000.0
N_MIXERS = 3

SSD_HEAD_DIM = 64
SSD_GROUPS = 8
SSD_STATE = 128
SSD_CONV = 5
SSD_CHUNK = 128

GQA_HEAD_DIM = 128
GQA_KV_HEADS = 4
WINDOW = 128

DIFF_HEAD_DIM = 128
SM_ROWS = 128

LANES = 128
SUBLANES = 8
CONV_HALO = 16
CAST_ROW_ALIGN = 16
VMEM_CAP = 56 * 1024 * 1024


def _cparams(sem, vmem_mb):
    return pltpu.CompilerParams(dimension_semantics=sem,
                                vmem_limit_bytes=min(int(vmem_mb * 1024 * 1024), VMEM_CAP))


def _nt_dot(a, b):
    return lax.dot_general(a, b, (((1,), (1,)), ((), ())), preferred_element_type=F32)


def _pick_tile(n, cap):
    best = LANES
    t = LANES
    while t <= cap:
        if n % t == 0:
            best = t
        t += LANES
    return best


def _split_specs(parts, tm, width, col):
    lat, ctx = parts
    nl = lat.shape[0] // tm
    assert lat.shape[0] % tm == 0 and ctx.shape[0] % tm == 0
    return ([pl.BlockSpec((tm, width), lambda i, j: (jnp.minimum(i, nl - 1), col(i, j))),
             pl.BlockSpec((tm, width), lambda i, j: (jnp.maximum(i - nl, 0), col(i, j)),
                          pipeline_mode=pl.Buffered(1))], nl)


def _cast_specs(casts, nsteps, step):
    ins, outs, shapes = [], [], []
    for arr, idx in casts:
        _, r, c = arr.shape
        nblk = 1 << (nsteps.bit_length() - 1)
        while r % (nblk * CAST_ROW_ALIGN):
            nblk //= 2
        blk = lambda i, j, nblk=nblk: jnp.minimum(step(i, j), nblk - 1)
        ins.append(pl.BlockSpec((1, r // nblk, c), lambda i, j, idx=idx, blk=blk: (idx, blk(i, j), 0)))
        outs.append(pl.BlockSpec((1, r // nblk, c), lambda i, j, blk=blk: (0, blk(i, j), 0)))
        shapes.append(jax.ShapeDtypeStruct((1, r, c), BF16))
    return ins, outs, shapes, [a for a, _ in casts]


def _pick_part(i, nl, lat_ref, ctx_ref):
    return jnp.where(i >= nl, ctx_ref[...], lat_ref[...])


def _norm_mod(x, g, shift, scale):
    ms = jnp.mean(x * x, axis=-1, keepdims=True)
    y = x * lax.rsqrt(ms + EPS) * g
    return y * (1.0 + scale) + shift


def _ada_block(cond_ref, w_ref, b_ref, o_ref):
    a = cond_ref[...]
    s = a / (1.0 + jnp.exp(-a))
    o_ref[0] = jnp.dot(s.astype(BF16), w_ref[0].astype(BF16), preferred_element_type=F32) + b_ref[0]


def _ada_layers(cond8, ada_w, ada_b, depth):
    _, d, n = ada_w.shape
    tn = _pick_tile(n, 1024)
    return pl.pallas_call(
        _ada_block,
        grid=(depth, n // tn),
        in_specs=[pl.BlockSpec((SUBLANES, d), lambda l, j: (0, 0)),
                  pl.BlockSpec((1, d, tn), lambda l, j: (l, 0, j)),
                  pl.BlockSpec((1, 1, tn), lambda l, j: (l, 0, j))],
        out_specs=pl.BlockSpec((1, SUBLANES, tn), lambda l, j: (l, 0, j)),
        out_shape=jax.ShapeDtypeStruct((depth, SUBLANES, n), F32),
        compiler_params=_cparams(("arbitrary", "arbitrary"), 40),
        name="ada_mod",
    )(cond8, ada_w, ada_b.reshape(ada_w.shape[0], 1, n))


def _proj_kernel(*refs, n_tail, split_at, ncast):
    nh = 1 if split_at is None else 2
    nt = 1 if n_tail else 0
    g_ref, mod_ref, w_ref = refs[nh:nh + 3]
    rest = refs[nh + 3:]
    wt_ref = rest[0] if n_tail else None
    cast_src = rest[nt:nt + ncast]
    o_ref = rest[nt + ncast]
    ot_ref = rest[nt + ncast + 1] if n_tail else None
    cast_dst = rest[2 * nt + ncast + 1:2 * nt + 2 * ncast + 1]
    u_sc = rest[-1]

    @pl.when(pl.program_id(1) == 0)
    def _():
        m = mod_ref[0]
        x = refs[0][...] if split_at is None else _pick_part(pl.program_id(0), split_at, refs[0], refs[1])
        u = _norm_mod(x, g_ref[...], m[0:1], m[1:2]).astype(BF16)
        u_sc[...] = u
        if n_tail:
            ot_ref[...] = jnp.dot(u, wt_ref[0], preferred_element_type=F32)

    o_ref[...] = jnp.dot(u_sc[...], w_ref[0], preferred_element_type=F32).astype(o_ref.dtype)
    for src, dst in zip(cast_src, cast_dst):
        dst[...] = src[...].astype(BF16)


def _proj(h, g, mods, w, layer, ncols, n_tail, seq, nbatch, casts=(), tm=512):
    if isinstance(h, tuple):
        t, d = h[0].shape[0] + h[1].shape[0], h[0].shape[1]
        h_specs, split_at = _split_specs(h, tm, d, lambda i, j: 0)
        h_args = list(h)
    else:
        t, d = h.shape
        h_specs, split_at, h_args = [pl.BlockSpec((tm, d), lambda i, j: (i, 0))], None, [h]
    tn = _pick_tile(ncols, 3072)
    in_specs = h_specs + [pl.BlockSpec((1, d), lambda i, j: (0, 0)),
                          pl.BlockSpec((1, 6, d), lambda i, j: (jnp.minimum((i * tm) // seq, nbatch), 0, 0)),
                          pl.BlockSpec((1, d, tn), lambda i, j: (layer, 0, j))]
    out_specs = [pl.BlockSpec((tm, tn), lambda i, j: (i, j))]
    out_shape = [jax.ShapeDtypeStruct((t, ncols), BF16)]
    args = h_args + [g.reshape(1, d), mods, w]
    if n_tail:
        assert ncols % n_tail == 0
        tail_blk = ncols // n_tail
        in_specs.append(pl.BlockSpec((1, d, n_tail), lambda i, j: (layer, 0, tail_blk)))
        out_specs.append(pl.BlockSpec((tm, n_tail), lambda i, j: (i, 0)))
        out_shape.append(jax.ShapeDtypeStruct((t, n_tail), F32))
        args.append(w)
    nj = ncols // tn
    c_in, c_out, c_shape, c_args = _cast_specs(casts, (t // tm) * nj, lambda i, j: i * nj + j)
    outs = pl.pallas_call(
        functools.partial(_proj_kernel, n_tail=n_tail, split_at=split_at, ncast=len(casts)),
        grid=(t // tm, nj),
        in_specs=in_specs + c_in,
        out_specs=out_specs + c_out,
        out_shape=out_shape + c_shape,
        scratch_shapes=[pltpu.VMEM((tm, d), BF16)],
        compiler_params=_cparams(("arbitrary", "arbitrary"), 56 if casts else 48),
        name="norm_mod_proj",
    )(*args, *c_args)
    nmain = 2 if n_tail else 1
    return outs[0], (outs[1] if n_tail else None), list(outs[nmain:])


def _mlp_kernel(h_ref, g_ref, mod_ref, w1_ref, w2_ref, fg_ref, *rest, nk, final_norm, ncast, with_ada):
    na = 3 if with_ada else 0
    cast_src, ada_in = rest[:ncast], rest[ncast:ncast + na]
    o_ref = rest[ncast + na]
    cast_dst = rest[ncast + na + 1:2 * ncast + na + 1]
    u_sc, acc_sc = rest[-2:]
    k = pl.program_id(1)

    @pl.when(k == 0)
    def _():
        m = mod_ref[0]
        u = _norm_mod(h_ref[...], g_ref[...], m[3:4], m[4:5])
        u_sc[...] = u.astype(BF16)
        acc_sc[...] = jnp.zeros_like(acc_sc)

    hk = jnp.dot(u_sc[...], w1_ref[0], preferred_element_type=F32)
    hk = jnp.square(jnp.maximum(hk, 0.0)).astype(BF16)
    acc_sc[...] += jnp.dot(hk, w2_ref[0], preferred_element_type=F32)
    for src, dst in zip(cast_src, cast_dst):
        dst[...] = src[...].astype(BF16)
    if with_ada:
        _ada_block(*ada_in, rest[2 * ncast + na + 1])

    @pl.when(k == nk - 1)
    def _():
        m = mod_ref[0]
        out = h_ref[...] + m[5:6] * acc_sc[...]
        if final_norm:
            ms = jnp.mean(out * out, axis=-1, keepdims=True)
            out = out * lax.rsqrt(ms + EPS) * fg_ref[...]
        o_ref[...] = out


def _mlp(h, g, mods, w1, w2, layer, final_g, rows, seq, nbatch, final_norm, casts=(), ada=None, tm=512, tk=1024):
    d = h.shape[1]
    hid = w1.shape[2]
    nk = hid // tk
    nsteps = (rows // tm) * nk
    step = lambda i, k: i * nk + k
    cast_in, cast_out, cast_shape, cast_args = _cast_specs(casts, nsteps, step)
    if ada is not None:
        cond8, ada_w, ada_b, la = ada
        n = ada_w.shape[2]
        nblk = max(b for b in range(1, n // LANES + 1) if (n // LANES) % b == 0 and b <= nsteps)
        tn = n // nblk
        ablk = lambda i, k: jnp.minimum(step(i, k), nblk - 1)
        cast_in += [pl.BlockSpec((SUBLANES, d), lambda i, k: (0, 0)),
                    pl.BlockSpec((1, d, tn), lambda i, k: (la, 0, ablk(i, k))),
                    pl.BlockSpec((1, 1, tn), lambda i, k: (la, 0, ablk(i, k)))]
        cast_out.append(pl.BlockSpec((1, SUBLANES, tn), lambda i, k: (0, 0, ablk(i, k))))
        cast_shape.append(jax.ShapeDtypeStruct((1, SUBLANES, n), F32))
        cast_args += [cond8, ada_w, ada_b.reshape(ada_w.shape[0], 1, n)]
    outs = pl.pallas_call(
        functools.partial(_mlp_kernel, nk=nk, final_norm=final_norm, ncast=len(casts), with_ada=ada is not None),
        grid=(rows // tm, nk),
        in_specs=[pl.BlockSpec((tm, d), lambda i, k: (i, 0)),
                  pl.BlockSpec((1, d), lambda i, k: (0, 0)),
                  pl.BlockSpec((1, 6, d), lambda i, k: (jnp.minimum((i * tm) // seq, nbatch), 0, 0)),
                  pl.BlockSpec((1, d, tk), lambda i, k: (layer, 0, k)),
                  pl.BlockSpec((1, tk, d), lambda i, k: (layer, k, 0)),
                  pl.BlockSpec((1, d), lambda i, k: (0, 0))] + cast_in,
        out_specs=[pl.BlockSpec((tm, d), lambda i, k: (i, 0))] + cast_out,
        out_shape=[jax.ShapeDtypeStruct((rows, d), F32)] + cast_shape,
        scratch_shapes=[pltpu.VMEM((tm, d), BF16), pltpu.VMEM((tm, d), F32)],
        compiler_params=_cparams(("arbitrary", "arbitrary"), 56),
        name="mlp",
    )(h, g.reshape(1, d), mods, w1, w2, final_g.reshape(1, d), *cast_args)
    ncast = len(casts)
    return outs[0], list(outs[1:1 + ncast]), (outs[1 + ncast] if ada is not None else None)


def _outproj_kernel(*refs, split_at):
    ny = 1 if split_at is None else 2
    w_ref, h_ref, mod_ref, o_ref = refs[ny:]
    y = refs[0][...] if split_at is None else _pick_part(pl.program_id(0), split_at, refs[0], refs[1])
    acc = jnp.dot(y, w_ref[0], preferred_element_type=F32)
    o_ref[...] = h_ref[...] + mod_ref[0][2:3] * acc


def _outproj(y, w, layer, h, mods, rows, seq, nbatch, tm=512):
    d = w.shape[2]
    tn = d
    if isinstance(y, tuple):
        kdim = y[0].shape[1]
        y_specs, split_at = _split_specs(y, tm, kdim, lambda i, j: 0)
        y_args = list(y)
    else:
        kdim = y.shape[1]
        y_specs, split_at, y_args = [pl.BlockSpec((tm, kdim), lambda i, j: (i, 0))], None, [y]
    return pl.pallas_call(
        functools.partial(_outproj_kernel, split_at=split_at),
        grid=(rows // tm, d // tn),
        in_specs=y_specs + [
                  pl.BlockSpec((1, kdim, tn), lambda i, j: (layer, 0, j)),
                  pl.BlockSpec((tm, tn), lambda i, j: (i, j)),
                  pl.BlockSpec((1, 6, tn), lambda i, j: (jnp.minimum((i * tm) // seq, nbatch), 0, j))],
        out_specs=pl.BlockSpec((tm, tn), lambda i, j: (i, j)),
        out_shape=jax.ShapeDtypeStruct((rows, d), F32),
        compiler_params=_cparams(("arbitrary", "arbitrary"), 48),
        name="out_proj",
    )(*y_args, w, h, mods)


def _rope_tables(n_tokens, head_dim):
    rows = n_tokens // GRID_W
    row = jnp.repeat(jnp.arange(rows, dtype=F32), GRID_W)
    col = jnp.tile(jnp.arange(GRID_W, dtype=F32), rows)
    n_freq = head_dim // 4
    inv_freq = ROPE_THETA ** (-jnp.arange(n_freq, dtype=F32) / n_freq)
    ang = jnp.concatenate([row[:, None] * inv_freq, col[:, None] * inv_freq], axis=-1)
    cos, sin = jnp.cos(ang), jnp.sin(ang)
    return jnp.concatenate([cos, cos], axis=-1), jnp.concatenate([-sin, sin], axis=-1)


def _rope(x, cos2, sin_s):
    return x * cos2 + pltpu.roll(x, x.shape[-1] // 2, 1) * sin_s


def _gqa_head(h, qs, segs, sink_ref, s_sc, p_sc, rep):
    blk = WINDOW
    for (c0, w, k_, _, _) in segs:
        s_sc[h, :, c0:c0 + w] = _nt_dot(qs, k_)
    linv = []
    for r in range(rep):
        rows = slice(r * blk, (r + 1) * blk)
        sink = sink_ref[h * rep + r] * LOG2E
        m = jnp.full((blk, 1), sink, F32)
        svals = []
        for (c0, w, _, _, msk) in segs:
            s = s_sc[h, rows, c0:c0 + w]
            if msk is not None:
                s = jnp.where(msk, s, -jnp.inf)
            svals.append(s)
            m = jnp.maximum(m, jnp.max(s, axis=-1, keepdims=True))
        l = jnp.exp2(sink - m)
        for (c0, w, _, _, _), s in zip(segs, svals):
            p = jnp.exp2(s - m)
            l = l + jnp.sum(p, axis=-1, keepdims=True)
            p_sc[h, rows, c0:c0 + w] = p.astype(BF16)
        linv.append(1.0 / l)
    acc = None
    for (c0, w, _, v_, _) in segs:
        pv = jnp.dot(p_sc[h, :, c0:c0 + w], v_, preferred_element_type=F32)
        acc = pv if acc is None else acc + pv
    return acc * jnp.concatenate(linv, axis=0)


def _gqa_kernel(sink_ref, q_ref, kp_ref, ko_ref, kn_ref, vp_ref, vo_ref, vn_ref, kc_ref, vc_ref,
                cos_ref, sin_ref, o_ref, s_sc, p_sc, *, nb, nctxb, rep):
    t = pl.program_id(1)
    hd = GQA_HEAD_DIM
    blk = WINDOW
    nctx = kc_ref.shape[0]
    qscale = hd ** -0.5 * LOG2E

    def write(h, o):
        for r in range(rep):
            c0 = (h * rep + r) * hd
            o_ref[:, c0:c0 + hd] = o[r * blk:(r + 1) * blk].astype(o_ref.dtype)

    @pl.when(t < nctxb)
    def _ctx():
        for h in range(GQA_KV_HEADS):
            qs = jnp.concatenate([q_ref[:, (h * rep + r) * hd:(h * rep + r + 1) * hd] for r in range(rep)], axis=0)
            qs = (qs.astype(F32) * qscale).astype(BF16)
            kc = kc_ref[:, h * hd:(h + 1) * hd]
            vc = vc_ref[:, h * hd:(h + 1) * hd]
            write(h, _gqa_head(h, qs, [(0, nctx, kc, vc, None)], sink_ref, s_sc, p_sc, rep))

    @pl.when(t >= nctxb)
    def _lat():
        n = t - nctxb
        pq = pl.multiple_of(n * blk, blk)
        pp = pl.multiple_of(jnp.maximum(n - 1, 0) * blk, blk)
        pn = pl.multiple_of(jnp.minimum(n + 1, nb - 1) * blk, blk)
        cq, sq = cos_ref[pl.ds(pq, blk), :], sin_ref[pl.ds(pq, blk), :]
        cp, sp = cos_ref[pl.ds(pp, blk), :], sin_ref[pl.ds(pp, blk), :]
        cn, sn = cos_ref[pl.ds(pn, blk), :], sin_ref[pl.ds(pn, blk), :]
        i = lax.broadcasted_iota(jnp.int32, (blk, 3 * blk), 0)
        j = lax.broadcasted_iota(jnp.int32, (blk, 3 * blk), 1)
        rel = j - blk - i
        valid = (rel <= WINDOW) & (rel >= -WINDOW)
        valid = valid & ((j >= blk) | (n >= 1)) & ((j < 2 * blk) | (n <= nb - 2))
        for h in range(GQA_KV_HEADS):
            qs = jnp.concatenate(
                [_rope(q_ref[:, (h * rep + r) * hd:(h * rep + r + 1) * hd].astype(F32), cq, sq)
                 for r in range(rep)], axis=0)
            sl = slice(h * hd, (h + 1) * hd)
            kw = jnp.concatenate([_rope(kp_ref[:, sl].astype(F32), cp, sp), _rope(ko_ref[:, sl].astype(F32), cq, sq),
                                  _rope(kn_ref[:, sl].astype(F32), cn, sn)], axis=0).astype(BF16)
            vw = jnp.concatenate([vp_ref[:, sl], vo_ref[:, sl], vn_ref[:, sl]], axis=0)
            kc = kc_ref[:, sl]
            vc = vc_ref[:, sl]
            segs = [(0, 3 * blk, kw, vw, valid), (3 * blk, nctx, kc, vc, None)]
            write(h, _gqa_head(h, (qs * qscale).astype(BF16), segs, sink_ref, s_sc, p_sc, rep))


def _gqa(p, sink, cos2, sin_s, nbatch, seq, nctx):
    t = p.shape[0]
    hd = GQA_HEAD_DIM
    nkv = GQA_KV_HEADS
    dq = p.shape[1] - 2 * nkv * hd
    rep = dq // hd // nkv
    blk = WINDOW
    nb = seq // blk
    nctxb = nctx // blk
    latb = nbatch * seq // blk
    kvw = nkv * hd
    kcol = dq // kvw
    vcol = kcol + 1

    def qrow(b, s):
        return jnp.where(s < nctxb, latb + b * nctxb + s, b * nb + s - nctxb)

    def nidx(s):
        return jnp.maximum(s - nctxb, 0)

    def kvspec(off, col):
        return pl.BlockSpec((blk, kvw), lambda b, s: (b * nb + jnp.clip(nidx(s) + off, 0, nb - 1), col))

    ctx_blk = nbatch * seq // nctx
    return pl.pallas_call(
        functools.partial(_gqa_kernel, nb=nb, nctxb=nctxb, rep=rep),
        grid=(nbatch, nctxb + nb),
        in_specs=[pl.BlockSpec(memory_space=pltpu.SMEM),
                  pl.BlockSpec((blk, dq), lambda b, s: (qrow(b, s), 0)),
                  kvspec(-1, kcol), kvspec(0, kcol), kvspec(1, kcol),
                  kvspec(-1, vcol), kvspec(0, vcol), kvspec(1, vcol),
                  pl.BlockSpec((nctx, kvw), lambda b, s: (ctx_blk + b, kcol)),
                  pl.BlockSpec((nctx, kvw), lambda b, s: (ctx_blk + b, vcol)),
                  pl.BlockSpec((seq, hd), lambda b, s: (0, 0)),
                  pl.BlockSpec((seq, hd), lambda b, s: (0, 0))],
        out_specs=pl.BlockSpec((blk, dq), lambda b, s: (qrow(b, s), 0)),
        out_shape=jax.ShapeDtypeStruct((t, dq), BF16),
        scratch_shapes=[pltpu.VMEM((nkv, rep * blk, 3 * blk + nctx), F32),
                        pltpu.VMEM((nkv, rep * blk, 3 * blk + nctx), BF16)],
        compiler_params=_cparams(("arbitrary", "arbitrary"), 40),
        name="gqa_window_attn",
    )(sink, p, p, p, p, p, p, p, p, p, cos2, sin_s)


def _diff_kernel(lq1_ref, lk1_ref, lq2_ref, lk2_ref, g_ref, ql_ref, qc_ref, kl_ref, kc_ref, vl_ref, vc_ref,
                 cos_ref, sin_ref, ol_ref, oc_ref, k_sc, s_sc, e_sc, *, nctx, seq, tq, kchunk, lambda_init):
    t = pl.program_id(2)
    hd = DIFF_HEAD_DIM
    qscale = hd ** -0.5 * LOG2E

    @pl.when(t == 0)
    def _prep():
        for r0 in range(0, seq, kchunk):
            c, s = cos_ref[r0:r0 + kchunk, :], sin_ref[r0:r0 + kchunk, :]
            for tt in range(2):
                k_sc[r0:r0 + kchunk, tt * hd:(tt + 1) * hd] = _rope(
                    kl_ref[r0:r0 + kchunk, tt * hd:(tt + 1) * hd].astype(F32), c, s).astype(BF16)

    lam = (jnp.exp(jnp.sum(lq1_ref[...] * lk1_ref[...], axis=-1, keepdims=True))
           - jnp.exp(jnp.sum(lq2_ref[...] * lk2_ref[...], axis=-1, keepdims=True)) + lambda_init)

    ctx_chunks = [(c0, min(kchunk, nctx - c0), kc_ref, c0) for c0 in range(0, nctx, kchunk)]
    lat_chunks = [(nctx + c0, kchunk, k_sc, c0) for c0 in range(0, seq, kchunk)]

    def run(qs, chunks, nrows, o_ref, with_lat):
        for tt in range(2):
            q = (qs[tt] * qscale).astype(BF16)
            for (c0, cs, kr, r0) in chunks:
                s_sc[tt, 0:nrows, c0:c0 + cs] = _nt_dot(q, kr[r0:r0 + cs, tt * hd:(tt + 1) * hd])
        outs = []
        for tt in range(2):
            linv = []
            for r in range(0, nrows, SM_ROWS):
                rows = slice(r, r + SM_ROWS)
                m = jnp.full((SM_ROWS, 1), -jnp.inf, F32)
                for (c0, cs, _, _) in chunks:
                    m = jnp.maximum(m, jnp.max(s_sc[tt, rows, c0:c0 + cs], axis=-1, keepdims=True))
                l = jnp.zeros((SM_ROWS, 1), F32)
                for (c0, cs, _, _) in chunks:
                    e = jnp.exp2(s_sc[tt, rows, c0:c0 + cs] - m)
                    l = l + jnp.sum(e, axis=-1, keepdims=True)
                    e_sc[tt, rows, c0:c0 + cs] = e.astype(BF16)
                linv.append(1.0 / l)
            acc = jnp.dot(e_sc[tt, 0:nrows, 0:nctx], vc_ref[...], preferred_element_type=F32)
            if with_lat:
                acc = acc + jnp.dot(e_sc[tt, 0:nrows, nctx:nctx + seq], vl_ref[...], preferred_element_type=F32)
            outs.append(acc * jnp.concatenate(linv, axis=0))
        o = outs[0] - lam * outs[1]
        ms = jnp.mean(o * o, axis=-1, keepdims=True)
        o = o * lax.rsqrt(ms + EPS) * g_ref[...] * (1.0 - lambda_init)
        o_ref[...] = o.astype(o_ref.dtype)

    @pl.when(t == 0)
    def _ctx():
        run([qc_ref[:, 0:hd].astype(F32), qc_ref[:, hd:2 * hd].astype(F32)], ctx_chunks, nctx, oc_ref, False)

    @pl.when(t > 0)
    def _lat():
        p0 = pl.multiple_of((t - 1) * tq, tq)
        c, s = cos_ref[pl.ds(p0, tq), :], sin_ref[pl.ds(p0, tq), :]
        run([_rope(ql_ref[:, 0:hd].astype(F32), c, s), _rope(ql_ref[:, hd:2 * hd].astype(F32), c, s)],
            ctx_chunks + lat_chunks, tq, ol_ref, True)


def _diff_attn(p, lq1, lk1, lq2, lk2, subln_g, cos2, sin_s, nbatch, seq, nctx, lambda_init):
    hd = DIFF_HEAD_DIM
    vd = 2 * hd
    nh = p.shape[1] // (3 * vd)
    tq = 512
    nq = seq // tq
    ctx_blk = nbatch * seq // nctx
    kchunk = 512

    def lrow(b, s):
        return b * nq + jnp.maximum(s - 1, 0)

    vec = lambda a: a.reshape(1, -1)
    o_lat, o_ctx = pl.pallas_call(
        functools.partial(_diff_kernel, nctx=nctx, seq=seq, tq=tq, kchunk=kchunk, lambda_init=lambda_init),
        grid=(nbatch, nh, nq + 1),
        in_specs=[pl.BlockSpec((1, hd), lambda b, h, s: (0, 0))] * 4
        + [pl.BlockSpec((1, vd), lambda b, h, s: (0, 0)),
           pl.BlockSpec((tq, vd), lambda b, h, s: (lrow(b, s), h)),
           pl.BlockSpec((nctx, vd), lambda b, h, s: (ctx_blk + b, h)),
           pl.BlockSpec((seq, vd), lambda b, h, s: (b, nh + h)),
           pl.BlockSpec((nctx, vd), lambda b, h, s: (ctx_blk + b, nh + h)),
           pl.BlockSpec((seq, vd), lambda b, h, s: (b, 2 * nh + h)),
           pl.BlockSpec((nctx, vd), lambda b, h, s: (ctx_blk + b, 2 * nh + h)),
           pl.BlockSpec((seq, hd), lambda b, h, s: (0, 0)),
           pl.BlockSpec((seq, hd), lambda b, h, s: (0, 0))],
        out_specs=[pl.BlockSpec((tq, vd), lambda b, h, s: (lrow(b, s), h)),
                   pl.BlockSpec((nctx, vd), lambda b, h, s: (b, h))],
        out_shape=[jax.ShapeDtypeStruct((nbatch * seq, nh * vd), BF16),
                   jax.ShapeDtypeStruct((nbatch * nctx, nh * vd), BF16)],
        scratch_shapes=[pltpu.VMEM((seq, vd), BF16), pltpu.VMEM((2, tq, nctx + seq), F32),
                        pltpu.VMEM((2, tq, nctx + seq), BF16)],
        compiler_params=_cparams(("arbitrary", "arbitrary", "arbitrary"), 56),
        name="diff_attn",
    )(vec(lq1), vec(lk1), vec(lq2), vec(lk2), vec(subln_g), p, p, p, p, p, p, cos2, sin_s)
    return o_lat, o_ctx


def _ssd_conv_kernel(xm_ref, xp_ref, xn_ref, bcm_ref, bcp_ref, bcn_ref, dt_ref, wx_ref, wbc_ref, bx_ref,
                     bbc_ref, dtb_ref, sh_ref, xs_ref, bm_ref, cm_ref, dto_ref, *, rows, seq, nctx, nlat_rows,
                     slab):
    r = pl.program_id(0)
    row0 = r * rows
    in_lat = row0 < nlat_rows
    seg = jnp.where(in_lat, seq, nctx)
    off = jnp.where(in_lat, row0, row0 - nlat_rows)
    is_first = (off % seg) == 0
    is_last = ((off + rows) % seg) == 0
    halo = CONV_HALO
    pad = SSD_CONV // 2
    ngrp = SSD_GROUPS
    nst = SSD_STATE
    sub = lax.broadcasted_iota(jnp.int32, (SUBLANES, slab), 0)

    def conv_slab(m_ref, p_ref, n_ref, w_ref, b_ref, c0):
        main = m_ref[:, c0:c0 + slab]
        prev = jnp.where(is_first, 0.0, p_ref[:, c0:c0 + slab].astype(F32))
        nxt = jnp.where(is_last, 0.0, n_ref[:, c0:c0 + slab].astype(F32))
        acc = b_ref[:, c0:c0 + slab] + w_ref[pad:pad + 1, c0:c0 + slab] * main.astype(F32)
        for k in range(SSD_CONV):
            dlt = k - pad
            if dlt == 0:
                continue
            sh = jnp.dot(sh_ref[k], main, preferred_element_type=F32)
            if dlt < 0:
                top = sh[0:SUBLANES]
                for q in range(-dlt):
                    top = jnp.where(sub == q, prev[halo + dlt + q:halo + dlt + q + 1, :], top)
                sh = jnp.concatenate([top, sh[SUBLANES:]], axis=0)
            else:
                bot = sh[rows - SUBLANES:]
                for q in range(dlt):
                    bot = jnp.where(sub == SUBLANES - dlt + q, nxt[q:q + 1, :], bot)
                sh = jnp.concatenate([sh[:rows - SUBLANES], bot], axis=0)
            acc = acc + w_ref[k:k + 1, c0:c0 + slab] * sh
        return acc / (1.0 + jnp.exp(-acc))

    for g in range(ngrp):
        xs_ref[g] = conv_slab(xm_ref, xp_ref, xn_ref, wx_ref, bx_ref, g * slab).astype(xs_ref.dtype)
    nbc = 2 * ngrp * nst // slab
    for sidx in range(nbc):
        y = conv_slab(bcm_ref, bcp_ref, bcn_ref, wbc_ref, bbc_ref, sidx * slab).astype(bm_ref.dtype)
        for q in range(slab // nst):
            gi = sidx * (slab // nst) + q
            if gi < ngrp:
                bm_ref[gi] = y[:, q * nst:(q + 1) * nst]
            else:
                cm_ref[gi - ngrp] = y[:, q * nst:(q + 1) * nst]
    v = dt_ref[...] + dtb_ref[...]
    dto_ref[...] = jnp.maximum(v, 0.0) + jnp.log1p(jnp.exp(-jnp.abs(v)))


def _ssd_conv(zx, dt_raw, conv_w, conv_b, dt_bias, d_inner, nbatch, seq, nctx):
    t = zx.shape[0]
    rows = 256
    slab = 512
    ngrp, nst = SSD_GROUPS, SSD_STATE
    bcw = 2 * ngrp * nst
    nh2 = dt_bias.size
    halo = CONV_HALO
    rb = rows // halo
    nhalo = t // halo
    xcol = 1
    bccol = 2 * d_inner // bcw
    wx, wbc = conv_w[:, :d_inner], conv_w[:, d_inner:]
    bx, bbc = conv_b[:d_inner].reshape(1, -1), conv_b[d_inner:].reshape(1, -1)
    shifts = jnp.stack([jnp.eye(rows, k=k - SSD_CONV // 2, dtype=BF16) for k in range(SSD_CONV)])
    prev = lambda r: jnp.maximum(r * rb - 1, 0)
    nxt = lambda r: jnp.minimum(r * rb + rb, nhalo - 1)
    return pl.pallas_call(
        functools.partial(_ssd_conv_kernel, rows=rows, seq=seq, nctx=nctx, nlat_rows=nbatch * seq, slab=slab),
        grid=(t // rows,),
        in_specs=[pl.BlockSpec((rows, d_inner), lambda r: (r, xcol)),
                  pl.BlockSpec((halo, d_inner), lambda r: (prev(r), xcol)),
                  pl.BlockSpec((halo, d_inner), lambda r: (nxt(r), xcol)),
                  pl.BlockSpec((rows, bcw), lambda r: (r, bccol)),
                  pl.BlockSpec((halo, bcw), lambda r: (prev(r), bccol)),
                  pl.BlockSpec((halo, bcw), lambda r: (nxt(r), bccol)),
                  pl.BlockSpec((rows, nh2), lambda r: (r, 0)),
                  pl.BlockSpec((SSD_CONV, d_inner), lambda r: (0, 0)),
                  pl.BlockSpec((SSD_CONV, bcw), lambda r: (0, 0)),
                  pl.BlockSpec((1, d_inner), lambda r: (0, 0)),
                  pl.BlockSpec((1, bcw), lambda r: (0, 0)),
                  pl.BlockSpec((1, nh2), lambda r: (0, 0)),
                  pl.BlockSpec((SSD_CONV, rows, rows), lambda r: (0, 0, 0))],
        out_specs=[pl.BlockSpec((ngrp, rows, d_inner // ngrp), lambda r: (0, r, 0)),
                   pl.BlockSpec((ngrp, rows, nst), lambda r: (0, r, 0)),
                   pl.BlockSpec((ngrp, rows, nst), lambda r: (0, r, 0)),
                   pl.BlockSpec((rows, nh2), lambda r: (r, 0))],
        out_shape=[jax.ShapeDtypeStruct((ngrp, t, d_inner // ngrp), BF16),
                   jax.ShapeDtypeStruct((ngrp, t, nst), BF16),
                   jax.ShapeDtypeStruct((ngrp, t, nst), BF16),
                   jax.ShapeDtypeStruct((t, nh2), F32)],
        compiler_params=_cparams(("arbitrary",), 40),
        name="ssd_conv",
    )(zx, zx, zx, zx, zx, zx, dt_raw, wx, wbc, bx, bbc, dt_bias.reshape(1, nh2), shifts)


def _ssd_scan_kernel(alog_ref, tri_ref, xf_ref, bf_ref, cf_ref, dtf_ref, xb_ref, bb_ref, cb_ref, dtb_ref,
                     yf_ref, yb_ref, st_sc, e_sc, et_sc, wt_sc, ext_sc, *, nheads):
    q = SSD_CHUNK
    ngrp = SSD_GROUPS
    hpg = nheads // ngrp
    p = SSD_HEAD_DIM

    @pl.when(pl.program_id(1) == 0)
    def _():
        st_sc[...] = jnp.zeros_like(st_sc)

    a = -jnp.exp(alog_ref[...]) * LOG2E
    tri = tri_ref[...]
    es, dts, ws = [], [], []
    for d, dt_ref in enumerate((dtf_ref, dtb_ref)):
        dt = dt_ref[:, d * nheads:(d + 1) * nheads]
        la = dt * a[:, d * nheads:(d + 1) * nheads]
        acum = jnp.dot(tri, la, preferred_element_type=F32, precision=lax.Precision.HIGHEST)
        tot = acum[q - 1:q, :]
        e = acum if d == 0 else tot - acum + la
        es.append(e)
        dts.append(dt)
        ws.append(dt * jnp.exp2(tot - e))
    e2 = jnp.concatenate(es, axis=1)
    e_sc[...] = e2
    e2t = e2.T
    et_sc[...] = (e2t - jnp.log2(jnp.concatenate(dts, axis=1).T)).reshape(2 * ngrp, hpg, q)
    wt_sc[...] = jnp.concatenate(ws, axis=1).T.reshape(2 * ngrp, hpg, q)
    tot_col = jnp.concatenate([e2t[0:nheads, q - 1:q], e2t[nheads:2 * nheads, 0:1]], axis=0)
    ext_sc[...] = jnp.broadcast_to(jnp.exp2(tot_col), (2 * nheads, q)).reshape(2 * ngrp, hpg, q)

    lane = lax.broadcasted_iota(jnp.int32, (q, 2 * nheads), 1)
    li = lax.broadcasted_iota(jnp.int32, (q, q), 0)
    si = lax.broadcasted_iota(jnp.int32, (q, q), 1)
    lo = lax.broadcasted_iota(jnp.int32, (q, 2 * p), 1) < p
    causal = (si <= li, si >= li)

    dirs = ((xf_ref, bf_ref, cf_ref, yf_ref), (xb_ref, bb_ref, cb_ref, yb_ref))

    def group(g, carry):
        pre = []
        for d, (x_ref, b_ref, c_ref, y_ref) in enumerate(dirs):
            bg = b_ref[g]
            cg = c_ref[g]
            st = st_sc[d, g]
            pre.append((_nt_dot(cg, bg),
                        bg.astype(F32).T,
                        st,
                        jnp.dot(cg, st.astype(BF16), preferred_element_type=F32),
                        et_sc[d * ngrp + g], wt_sc[d * ngrp + g], ext_sc[d * ngrp + g]))
        e_all = e_sc[...]
        for d, (x_ref, b_ref, c_ref, y_ref) in enumerate(dirs):
            cbm, bgt, st, z, et, wt, ext = pre[d]
            for jp in range(hpg // 2):
                ms, decc, bws, exts = [], [], [], []
                for j in (2 * jp, 2 * jp + 1):
                    col = d * nheads + g * hpg + j
                    sel = lane == col
                    ecol = jnp.sum(jnp.where(sel, e_all, 0.0), axis=1, keepdims=True)
                    decc.append(jnp.exp2(ecol))
                    lm = jnp.exp2(jnp.where(causal[d], ecol - et[j:j + 1, :], -jnp.inf))
                    ms.append((cbm * lm).astype(BF16))
                    bws.append((bgt * wt[j:j + 1, :]).astype(BF16))
                    exts.append(ext[j:j + 1, :])
                cols = slice(jp * 2 * p, (jp + 1) * 2 * p)
                xp = x_ref[g, :, cols]
                zero = jnp.zeros_like(xp)
                rhs = jnp.concatenate([jnp.where(lo, xp, zero), jnp.where(lo, zero, xp)], axis=0)
                y = jnp.dot(jnp.concatenate(ms, axis=1), rhs, preferred_element_type=F32)
                y = y + jnp.where(lo, decc[0], decc[1]) * z[:, cols]
                y_ref[g, :, cols] = y.astype(y_ref.dtype)
                contrib = jnp.dot(jnp.concatenate(bws, axis=1), rhs, preferred_element_type=F32)
                decay = jnp.where(lo[0:1, :], exts[0], exts[1])
                st_sc[d, g, :, cols] = st[:, cols] * decay + contrib
        return carry

    lax.fori_loop(0, ngrp, group, 0, unroll=2)


def _ssd_scan(xs_g, bm_g, cm_g, dt, a_log, nbatch, seq, nctx):
    ngrp, t, gw = xs_g.shape
    nst = bm_g.shape[2]
    q = SSD_CHUNK
    nheads = a_log.shape[1]
    hpg = nheads // ngrp
    nlat = seq // q
    nctxc = nctx // q
    latc = nbatch * nlat

    def fidx(b, s):
        return jnp.where(s < nctxc, latc + b * nctxc + s, b * nlat + s - nctxc)

    def bidx(b, s):
        return jnp.where(s < nctxc, latc + b * nctxc + nctxc - 1 - s, b * nlat + nlat - 1 - (s - nctxc))

    def specs(idx):
        return [pl.BlockSpec((ngrp, q, gw), lambda b, s: (0, idx(b, s), 0)),
                pl.BlockSpec((ngrp, q, nst), lambda b, s: (0, idx(b, s), 0)),
                pl.BlockSpec((ngrp, q, nst), lambda b, s: (0, idx(b, s), 0)),
                pl.BlockSpec((q, 2 * nheads), lambda b, s: (idx(b, s), 0))]

    tri = jnp.tril(jnp.ones((q, q), F32))
    yshape = jax.ShapeDtypeStruct((ngrp, t, gw), BF16)
    small = pltpu.VMEM((2 * ngrp, hpg, q), F32)
    return pl.pallas_call(
        functools.partial(_ssd_scan_kernel, nheads=nheads),
        grid=(nbatch, nctxc + nlat),
        in_specs=[pl.BlockSpec((1, 2 * nheads), lambda b, s: (0, 0)),
                  pl.BlockSpec((q, q), lambda b, s: (0, 0))] + specs(fidx) + specs(bidx),
        out_specs=[pl.BlockSpec((ngrp, q, gw), lambda b, s: (0, fidx(b, s), 0)),
                   pl.BlockSpec((ngrp, q, gw), lambda b, s: (0, bidx(b, s), 0))],
        out_shape=[yshape, yshape],
        scratch_shapes=[pltpu.VMEM((2, ngrp, nst, gw), F32), pltpu.VMEM((q, 2 * nheads), F32),
                        small, small, small],
        compiler_params=_cparams(("arbitrary", "arbitrary"), 48),
        name="ssd_scan",
    )(a_log.reshape(1, 2 * nheads), tri, xs_g, bm_g, cm_g, dt, xs_g, bm_g, cm_g, dt)


def _ssd_out_kernel(yf_ref, yb_ref, xs_ref, z_ref, dx_ref, ng_ref, w_ref, mod_ref, *rest, nsteps, gps, gw, split_at):
    o_ref, acc_sc = rest[-2:]
    s = pl.program_id(1)

    @pl.when(s == 0)
    def _():
        acc_sc[...] = jnp.zeros_like(acc_sc)

    acc = acc_sc[...]
    for g in range(gps):
        sl = slice(g * gw, (g + 1) * gw)
        z = z_ref[:, sl].astype(F32)
        y = yf_ref[g].astype(F32) + yb_ref[g].astype(F32) + xs_ref[g].astype(F32) * dx_ref[:, sl]
        y = y * (z / (1.0 + jnp.exp(-z)))
        ms = jnp.mean(y * y, axis=-1, keepdims=True)
        yn = (y * lax.rsqrt(ms + EPS) * ng_ref[:, sl]).astype(BF16)
        k0 = pl.multiple_of(s * (gps * gw) + g * gw, gw)
        acc = acc + jnp.dot(yn, w_ref[0, pl.ds(k0, gw), :], preferred_element_type=F32)

    @pl.when(s < nsteps - 1)
    def _():
        acc_sc[...] = acc

    @pl.when(s == nsteps - 1)
    def _():
        hres = rest[0][...] if split_at is None else _pick_part(pl.program_id(0), split_at, rest[0], rest[1])
        o_ref[...] = hres + mod_ref[0][2:3] * acc


def _ssd_out(yf, yb, xs_g, zx, d_skip, norm_g, w_out, layer, h, mods, rows, seq, nbatch, tm=512, gps=2):
    ngrp, _, gw = xs_g.shape
    d_inner = ngrp * gw
    d = w_out.shape[2]
    nsteps = ngrp // gps
    dx = jnp.repeat(d_skip, SSD_HEAD_DIM).reshape(1, d_inner)
    gspec = pl.BlockSpec((gps, tm, gw), lambda i, g: (g, i, 0))
    if isinstance(h, tuple):
        h_specs, split_at = _split_specs(h, tm, d, lambda i, g: 0)
        h_args = list(h)
    else:
        h_specs, split_at, h_args = [pl.BlockSpec((tm, d), lambda i, g: (i, 0))], None, [h]
    return pl.pallas_call(
        functools.partial(_ssd_out_kernel, nsteps=nsteps, gps=gps, gw=gw, split_at=split_at),
        grid=(rows // tm, nsteps),
        in_specs=[gspec, gspec, gspec,
                  pl.BlockSpec((tm, gps * gw), lambda i, g: (i, g)),
                  pl.BlockSpec((1, gps * gw), lambda i, g: (0, g)),
                  pl.BlockSpec((1, gps * gw), lambda i, g: (0, g)),
                  pl.BlockSpec((1, d_inner, d), lambda i, g: (layer, 0, 0), pipeline_mode=pl.Buffered(1)),
                  pl.BlockSpec((1, 6, d), lambda i, g: (jnp.minimum((i * tm) // seq, nbatch), 0, 0))] + h_specs,
        out_specs=pl.BlockSpec((tm, d), lambda i, g: (i, 0)),
        out_shape=jax.ShapeDtypeStruct((rows, d), F32),
        scratch_shapes=[pltpu.VMEM((tm, d), F32)],
        compiler_params=_cparams(("arbitrary", "arbitrary"), 56),
        name="ssd_gate_norm_out_proj",
    )(yf, yb, xs_g, zx, dx, norm_g.reshape(1, d_inner), w_out, mods, *h_args)


def kernel(x, c, ctx, c_ctx, ada_w, ada_b, norm_mix_g, norm_mlp_g, mlp_w1, mlp_w2, ssd_w_in, ssd_conv_w, ssd_conv_b, ssd_dt_bias, ssd_a_log, ssd_d, ssd_norm_g, ssd_w_out, gqa_w_qkv, gqa_sink, gqa_w_out, diff_w_qkv, diff_lam_q1, diff_lam_k1, diff_lam_q2, diff_lam_k2, diff_subln_g, diff_w_out, final_norm_g):
    nbatch, seq, d = x.shape
    nctx = ctx.shape[1]
    depth = ada_w.shape[0]
    nlat_rows = nbatch * seq
    t = nlat_rows + nbatch * nctx

    cond8 = jnp.zeros((SUBLANES, d), F32).at[:nbatch].set(c).at[nbatch].set(c_ctx)
    as_mods = lambda m: m[0, :nbatch + 1].reshape(nbatch + 1, 6, d)
    mods = as_mods(_ada_layers(cond8, ada_w, ada_b, 1))
    cos2, sin_s = _rope_tables(seq, GQA_HEAD_DIM)

    h = (x.reshape(nlat_rows, d), ctx.reshape(nbatch * nctx, d))

    def layer_weights(i):
        mix = ((ssd_w_in, ssd_w_out), (gqa_w_qkv, gqa_w_out), (diff_w_qkv, diff_w_out))[i % N_MIXERS]
        return [(mlp_w1, i), (mlp_w2, i), (mix[0], i // N_MIXERS), (mix[1], i // N_MIXERS)]

    w0 = layer_weights(0)
    wcur = [None, None, w0[2][0][w0[2][1]:w0[2][1] + 1].astype(BF16), None]
    for i in range(depth):
        kind, j = i % N_MIXERS, i // N_MIXERS
        need_ctx = i < depth - 1
        rows = t if need_ctx else nlat_rows
        w1_b, w2_b, win_b, wout_b = wcur
        assert kind == 0 or not isinstance(h, tuple)
        if kind == 0:
            nh2 = 2 * ssd_a_log.shape[2]
            first = [w0[0], w0[1], w0[3]] if i == 0 else []
            zx, dt_raw, casted = _proj(h, norm_mix_g[i], mods, win_b, 0, ssd_w_in.shape[2] - nh2, nh2, seq, nbatch,
                                       casts=first)
            if first:
                w1_b, w2_b, wout_b = casted
            xs_g, bm_g, cm_g, dt = _ssd_conv(zx, dt_raw, ssd_conv_w[j], ssd_conv_b[j], ssd_dt_bias[j],
                                             ssd_w_out.shape[1], nbatch, seq, nctx)
            yf, yb = _ssd_scan(xs_g, bm_g, cm_g, dt, ssd_a_log[j], nbatch, seq, nctx)
            h = _ssd_out(yf, yb, xs_g, zx, ssd_d[j], ssd_norm_g[j], wout_b, 0, h, mods, rows, seq, nbatch)
        elif kind == 1:
            p = _proj(h, norm_mix_g[i], mods, win_b, 0, gqa_w_qkv.shape[2], 0, seq, nbatch)[0]
            o = _gqa(p, gqa_sink[j], cos2, sin_s, nbatch, seq, nctx)
            h = _outproj(o, wout_b, 0, h, mods, rows, seq, nbatch)
        else:
            lambda_init = 0.8 - 0.6 * math.exp(-0.3 * i)
            p = _proj(h, norm_mix_g[i], mods, win_b, 0, diff_w_qkv.shape[2], 0, seq, nbatch)[0]
            o = _diff_attn(p, diff_lam_q1[j], diff_lam_k1[j], diff_lam_q2[j], diff_lam_k2[j], diff_subln_g[j],
                           cos2, sin_s, nbatch, seq, nctx, lambda_init)
            h = _outproj(o, wout_b, 0, h, mods, rows, seq, nbatch)
        h, wcur, mods_next = _mlp(h, norm_mlp_g[i], mods, w1_b, w2_b, 0, final_norm_g, rows, seq, nbatch,
                                  final_norm=not need_ctx, casts=layer_weights(i + 1) if need_ctx else (),
                                  ada=(cond8, ada_w, ada_b, i + 1) if need_ctx else None)
        if need_ctx:
            mods = as_mods(mods_next)
    return h[:nlat_rows].reshape(nbatch, seq, d)
```

```python
import functools
import math

import jax
import jax.numpy as jnp
from jax import lax
from jax.experimental import pallas as pl
from jax.experimental.pallas import tpu as pltpu

F32 = jnp.float32
BF16 = jnp.bfloat16

EPS = 1e-6
LOG2E = math.log2(math.e)
GRID_W = 64
ROPE_THETA = 10000.0
N_MIXERS = 3

SSD_HEAD_DIM = 64
SSD_GROUPS = 8
SSD_STATE = 128
SSD_CONV = 5
SSD_CHUNK = 128

GQA_HEAD_DIM = 128
GQA_KV_HEADS = 4
WINDOW = 128

DIFF_HEAD_DIM = 128
SM_ROWS = 128

LANES = 128
SUBLANES = 8
CONV_HALO = 16
CAST_ROW_ALIGN = 16
VMEM_CAP = 56 * 1024 * 1024


def _cparams(sem, vmem_mb):
    return pltpu.CompilerParams(dimension_semantics=sem,
                                vmem_limit_bytes=min(int(vmem_mb * 1024 * 1024), VMEM_CAP))


def _nt_dot(a, b):
    return lax.dot_general(a, b, (((1,), (1,)), ((), ())), preferred_element_type=F32)


def _pick_tile(n, cap):
    best = LANES
    t = LANES
    while t <= cap:
        if n % t == 0:
            best = t
        t += LANES
    return best


def _split_specs(parts, tm, width, col):
    lat, ctx = parts
    nl = lat.shape[0] // tm
    assert lat.shape[0] % tm == 0 and ctx.shape[0] % tm == 0
    return ([pl.BlockSpec((tm, width), lambda i, j: (jnp.minimum(i, nl - 1), col(i, j))),
             pl.BlockSpec((tm, width), lambda i, j: (jnp.maximum(i - nl, 0), col(i, j)),
                          pipeline_mode=pl.Buffered(1))], nl)


def _cast_specs(casts, nsteps, step):
    ins, outs, shapes = [], [], []
    for arr, idx in casts:
        _, r, c = arr.shape
        nblk = 1 << (nsteps.bit_length() - 1)
        while r % (nblk * CAST_ROW_ALIGN):
            nblk //= 2
        blk = lambda i, j, nblk=nblk: jnp.minimum(step(i, j), nblk - 1)
        ins.append(pl.BlockSpec((1, r // nblk, c), lambda i, j, idx=idx, blk=blk: (idx, blk(i, j), 0)))
        outs.append(pl.BlockSpec((1, r // nblk, c), lambda i, j, blk=blk: (0, blk(i, j), 0)))
        shapes.append(jax.ShapeDtypeStruct((1, r, c), BF16))
    return ins, outs, shapes, [a for a, _ in casts]


def _pick_part(i, nl, lat_ref, ctx_ref):
    return jnp.where(i >= nl, ctx_ref[...], lat_ref[...])


def _norm_mod(x, g, shift, scale):
    ms = jnp.mean(x * x, axis=-1, keepdims=True)
    y = x * lax.rsqrt(ms + EPS) * g
    return y * (1.0 + scale) + shift


def _ada_block(cond_ref, w_ref, b_ref, o_ref):
    a = cond_ref[...]
    s = a / (1.0 + jnp.exp(-a))
    o_ref[0] = jnp.dot(s.astype(BF16), w_ref[0].astype(BF16), preferred_element_type=F32) + b_ref[0]


def _ada_layers(cond8, ada_w, ada_b, depth):
    _, d, n = ada_w.shape
    tn = _pick_tile(n, 1024)
    return pl.pallas_call(
        _ada_block,
        grid=(depth, n // tn),
        in_specs=[pl.BlockSpec((SUBLANES, d), lambda l, j: (0, 0)),
                  pl.BlockSpec((1, d, tn), lambda l, j: (l, 0, j)),
                  pl.BlockSpec((1, 1, tn), lambda l, j: (l, 0, j))],
        out_specs=pl.BlockSpec((1, SUBLANES, tn), lambda l, j: (l, 0, j)),
        out_shape=jax.ShapeDtypeStruct((depth, SUBLANES, n), F32),
        compiler_params=_cparams(("arbitrary", "arbitrary"), 40),
        name="ada_mod",
    )(cond8, ada_w, ada_b.reshape(ada_w.shape[0], 1, n))


def _proj_kernel(*refs, n_tail, split_at, ncast):
    nh = 1 if split_at is None else 2
    nt = 1 if n_tail else 0
    g_ref, mod_ref, w_ref = refs[nh:nh + 3]
    rest = refs[nh + 3:]
    wt_ref = rest[0] if n_tail else None
    cast_src = rest[nt:nt + ncast]
    o_ref = rest[nt + ncast]
    ot_ref = rest[nt + ncast + 1] if n_tail else None
    cast_dst = rest[2 * nt + ncast + 1:2 * nt + 2 * ncast + 1]
    u_sc = rest[-1]

    @pl.when(pl.program_id(1) == 0)
    def _():
        m = mod_ref[0]
        x = refs[0][...] if split_at is None else _pick_part(pl.program_id(0), split_at, refs[0], refs[1])
        u = _norm_mod(x, g_ref[...], m[0:1], m[1:2]).astype(BF16)
        u_sc[...] = u
        if n_tail:
            ot_ref[...] = jnp.dot(u, wt_ref[0], preferred_element_type=F32)

    o_ref[...] = jnp.dot(u_sc[...], w_ref[0], preferred_element_type=F32).astype(o_ref.dtype)
    for src, dst in zip(cast_src, cast_dst):
        dst[...] = src[...].astype(BF16)


def _proj(h, g, mods, w, layer, ncols, n_tail, seq, nbatch, casts=(), tm=512):
    if isinstance(h, tuple):
        t, d = h[0].shape[0] + h[1].shape[0], h[0].shape[1]
        h_specs, split_at = _split_specs(h, tm, d, lambda i, j: 0)
        h_args = list(h)
    else:
        t, d = h.shape
        h_specs, split_at, h_args = [pl.BlockSpec((tm, d), lambda i, j: (i, 0))], None, [h]
    tn = _pick_tile(ncols, 3072)
    in_specs = h_specs + [pl.BlockSpec((1, d), lambda i, j: (0, 0)),
                          pl.BlockSpec((1, 6, d), lambda i, j: (jnp.minimum((i * tm) // seq, nbatch), 0, 0)),
                          pl.BlockSpec((1, d, tn), lambda i, j: (layer, 0, j))]
    out_specs = [pl.BlockSpec((tm, tn), lambda i, j: (i, j))]
    out_shape = [jax.ShapeDtypeStruct((t, ncols), BF16)]
    args = h_args + [g.reshape(1, d), mods, w]
    if n_tail:
        assert ncols % n_tail == 0
        tail_blk = ncols // n_tail
        in_specs.append(pl.BlockSpec((1, d, n_tail), lambda i, j: (layer, 0, tail_blk)))
        out_specs.append(pl.BlockSpec((tm, n_tail), lambda i, j: (i, 0)))
        out_shape.append(jax.ShapeDtypeStruct((t, n_tail), F32))
        args.append(w)
    nj = ncols // tn
    c_in, c_out, c_shape, c_args = _cast_specs(casts, (t // tm) * nj, lambda i, j: i * nj + j)
    outs = pl.pallas_call(
        functools.partial(_proj_kernel, n_tail=n_tail, split_at=split_at, ncast=len(casts)),
        grid=(t // tm, nj),
        in_specs=in_specs + c_in,
        out_specs=out_specs + c_out,
        out_shape=out_shape + c_shape,
        scratch_shapes=[pltpu.VMEM((tm, d), BF16)],
        compiler_params=_cparams(("arbitrary", "arbitrary"), 56 if casts else 48),
        name="norm_mod_proj",
    )(*args, *c_args)
    nmain = 2 if n_tail else 1
    return outs[0], (outs[1] if n_tail else None), list(outs[nmain:])


def _mlp_kernel(h_ref, g_ref, mod_ref, w1_ref, w2_ref, fg_ref, *rest, nk, final_norm, ncast, with_ada):
    na = 3 if with_ada else 0
    cast_src, ada_in = rest[:ncast], rest[ncast:ncast + na]
    o_ref = rest[ncast + na]
    cast_dst = rest[ncast + na + 1:2 * ncast + na + 1]
    u_sc, acc_sc = rest[-2:]
    k = pl.program_id(1)

    @pl.when(k == 0)
    def _():
        m = mod_ref[0]
        u = _norm_mod(h_ref[...], g_ref[...], m[3:4], m[4:5])
        u_sc[...] = u.astype(BF16)
        acc_sc[...] = jnp.zeros_like(acc_sc)

    hk = jnp.dot(u_sc[...], w1_ref[0], preferred_element_type=F32)
    hk = jnp.square(jnp.maximum(hk, 0.0)).astype(BF16)
    acc_sc[...] += jnp.dot(hk, w2_ref[0], preferred_element_type=F32)
    for src, dst in zip(cast_src, cast_dst):
        dst[...] = src[...].astype(BF16)
    if with_ada:
        _ada_block(*ada_in, rest[2 * ncast + na + 1])

    @pl.when(k == nk - 1)
    def _():
        m = mod_ref[0]
        out = h_ref[...] + m[5:6] * acc_sc[...]
        if final_norm:
            ms = jnp.mean(out * out, axis=-1, keepdims=True)
            out = out * lax.rsqrt(ms + EPS) * fg_ref[...]
        o_ref[...] = out


def _mlp(h, g, mods, w1, w2, layer, final_g, rows, seq, nbatch, final_norm, casts=(), ada=None, tm=512, tk=1024):
    d = h.shape[1]
    hid = w1.shape[2]
    nk = hid // tk
    nsteps = (rows // tm) * nk
    step = lambda i, k: i * nk + k
    cast_in, cast_out, cast_shape, cast_args = _cast_specs(casts, nsteps, step)
    if ada is not None:
        cond8, ada_w, ada_b, la = ada
        n = ada_w.shape[2]
        nblk = max(b for b in range(1, n // LANES + 1) if (n // LANES) % b == 0 and b <= nsteps)
        tn = n // nblk
        ablk = lambda i, k: jnp.minimum(step(i, k), nblk - 1)
        cast_in += [pl.BlockSpec((SUBLANES, d), lambda i, k: (0, 0)),
                    pl.BlockSpec((1, d, tn), lambda i, k: (la, 0, ablk(i, k))),
                    pl.BlockSpec((1, 1, tn), lambda i, k: (la, 0, ablk(i, k)))]
        cast_out.append(pl.BlockSpec((1, SUBLANES, tn), lambda i, k: (0, 0, ablk(i, k))))
        cast_shape.append(jax.ShapeDtypeStruct((1, SUBLANES, n), F32))
        cast_args += [cond8, ada_w, ada_b.reshape(ada_w.shape[0], 1, n)]
    outs = pl.pallas_call(
        functools.partial(_mlp_kernel, nk=nk, final_norm=final_norm, ncast=len(casts), with_ada=ada is not None),
        grid=(rows // tm, nk),
        in_specs=[pl.BlockSpec((tm, d), lambda i, k: (i, 0)),
                  pl.BlockSpec((1, d), lambda i, k: (0, 0)),
                  pl.BlockSpec((1, 6, d), lambda i, k: (jnp.minimum((i * tm) // seq, nbatch), 0, 0)),
                  pl.BlockSpec((1, d, tk), lambda i, k: (layer, 0, k)),
                  pl.BlockSpec((1, tk, d), lambda i, k: (layer, k, 0)),
                  pl.BlockSpec((1, d), lambda i, k: (0, 0))] + cast_in,
        out_specs=[pl.BlockSpec((tm, d), lambda i, k: (i, 0))] + cast_out,
        out_shape=[jax.ShapeDtypeStruct((rows, d), F32)] + cast_shape,
        scratch_shapes=[pltpu.VMEM((tm, d), BF16), pltpu.VMEM((tm, d), F32)],
        compiler_params=_cparams(("arbitrary", "arbitrary"), 56),
        name="mlp",
    )(h, g.reshape(1, d), mods, w1, w2, final_g.reshape(1, d), *cast_args)
    ncast = len(casts)
    return outs[0], list(outs[1:1 + ncast]), (outs[1 + ncast] if ada is not None else None)


def _outproj_kernel(*refs, split_at):
    ny = 1 if split_at is None else 2
    w_ref, h_ref, mod_ref, o_ref = refs[ny:]
    y = refs[0][...] if split_at is None else _pick_part(pl.program_id(0), split_at, refs[0], refs[1])
    acc = jnp.dot(y, w_ref[0], preferred_element_type=F32)
    o_ref[...] = h_ref[...] + mod_ref[0][2:3] * acc


def _outproj(y, w, layer, h, mods, rows, seq, nbatch, tm=512):
    d = w.shape[2]
    tn = d
    if isinstance(y, tuple):
        kdim = y[0].shape[1]
        y_specs, split_at = _split_specs(y, tm, kdim, lambda i, j: 0)
        y_args = list(y)
    else:
        kdim = y.shape[1]
        y_specs, split_at, y_args = [pl.BlockSpec((tm, kdim), lambda i, j: (i, 0))], None, [y]
    return pl.pallas_call(
        functools.partial(_outproj_kernel, split_at=split_at),
        grid=(rows // tm, d // tn),
        in_specs=y_specs + [
                  pl.BlockSpec((1, kdim, tn), lambda i, j: (layer, 0, j)),
                  pl.BlockSpec((tm, tn), lambda i, j: (i, j)),
                  pl.BlockSpec((1, 6, tn), lambda i, j: (jnp.minimum((i * tm) // seq, nbatch), 0, j))],
        out_specs=pl.BlockSpec((tm, tn), lambda i, j: (i, j)),
        out_shape=jax.ShapeDtypeStruct((rows, d), F32),
        compiler_params=_cparams(("arbitrary", "arbitrary"), 48),
        name="out_proj",
    )(*y_args, w, h, mods)


def _rope_tables(n_tokens, head_dim):
    rows = n_tokens // GRID_W
    row = jnp.repeat(jnp.arange(rows, dtype=F32), GRID_W)
    col = jnp.tile(jnp.arange(GRID_W, dtype=F32), rows)
    n_freq = head_dim // 4
    inv_freq = ROPE_THETA ** (-jnp.arange(n_freq, dtype=F32) / n_freq)
    ang = jnp.concatenate([row[:, None] * inv_freq, col[:, None] * inv_freq], axis=-1)
    cos, sin = jnp.cos(ang), jnp.sin(ang)
    return jnp.concatenate([cos, cos], axis=-1), jnp.concatenate([-sin, sin], axis=-1)


def _rope(x, cos2, sin_s):
    return x * cos2 + pltpu.roll(x, x.shape[-1] // 2, 1) * sin_s


def _gqa_head(h, qs, segs, sink_ref, s_sc, p_sc, rep):
    blk = WINDOW
    for (c0, w, k_, _, _) in segs:
        s_sc[h, :, c0:c0 + w] = _nt_dot(qs, k_)
    linv = []
    for r in range(rep):
        rows = slice(r * blk, (r + 1) * blk)
        sink = sink_ref[h * rep + r] * LOG2E
        m = jnp.full((blk, 1), sink, F32)
        svals = []
        for (c0, w, _, _, msk) in segs:
            s = s_sc[h, rows, c0:c0 + w]
            if msk is not None:
                s = jnp.where(msk, s, -jnp.inf)
            svals.append(s)
            m = jnp.maximum(m, jnp.max(s, axis=-1, keepdims=True))
        l = jnp.exp2(sink - m)
        for (c0, w, _, _, _), s in zip(segs, svals):
            p = jnp.exp2(s - m)
            l = l + jnp.sum(p, axis=-1, keepdims=True)
            p_sc[h, rows, c0:c0 + w] = p.astype(BF16)
        linv.append(1.0 / l)
    acc = None
    for (c0, w, _, v_, _) in segs:
        pv = jnp.dot(p_sc[h, :, c0:c0 + w], v_, preferred_element_type=F32)
        acc = pv if acc is None else acc + pv
    return acc * jnp.concatenate(linv, axis=0)


def _gqa_kernel(sink_ref, q_ref, kp_ref, ko_ref, kn_ref, vp_ref, vo_ref, vn_ref, kc_ref, vc_ref,
                cos_ref, sin_ref, o_ref, s_sc, p_sc, *, nb, nctxb, rep):
    t = pl.program_id(1)
    hd = GQA_HEAD_DIM
    blk = WINDOW
    nctx = kc_ref.shape[0]
    qscale = hd ** -0.5 * LOG2E

    def write(h, o):
        for r in range(rep):
            c0 = (h * rep + r) * hd
            o_ref[:, c0:c0 + hd] = o[r * blk:(r + 1) * blk].astype(o_ref.dtype)

    @pl.when(t < nctxb)
    def _ctx():
        for h in range(GQA_KV_HEADS):
            qs = jnp.concatenate([q_ref[:, (h * rep + r) * hd:(h * rep + r + 1) * hd] for r in range(rep)], axis=0)
            qs = (qs.astype(F32) * qscale).astype(BF16)
            kc = kc_ref[:, h * hd:(h + 1) * hd]
            vc = vc_ref[:, h * hd:(h + 1) * hd]
            write(h, _gqa_head(h, qs, [(0, nctx, kc, vc, None)], sink_ref, s_sc, p_sc, rep))

    @pl.when(t >= nctxb)
    def _lat():
        n = t - nctxb
        pq = pl.multiple_of(n * blk, blk)
        pp = pl.multiple_of(jnp.maximum(n - 1, 0) * blk, blk)
        pn = pl.multiple_of(jnp.minimum(n + 1, nb - 1) * blk, blk)
        cq, sq = cos_ref[pl.ds(pq, blk), :], sin_ref[pl.ds(pq, blk), :]
        cp, sp = cos_ref[pl.ds(pp, blk), :], sin_ref[pl.ds(pp, blk), :]
        cn, sn = cos_ref[pl.ds(pn, blk), :], sin_ref[pl.ds(pn, blk), :]
        i = lax.broadcasted_iota(jnp.int32, (blk, 3 * blk), 0)
        j = lax.broadcasted_iota(jnp.int32, (blk, 3 * blk), 1)
        rel = j - blk - i
        valid = (rel <= WINDOW) & (rel >= -WINDOW)
        valid = valid & ((j >= blk) | (n >= 1)) & ((j < 2 * blk) | (n <= nb - 2))
        for h in range(GQA_KV_HEADS):
            qs = jnp.concatenate(
                [_rope(q_ref[:, (h * rep + r) * hd:(h * rep + r + 1) * hd].astype(F32), cq, sq)
                 for r in range(rep)], axis=0)
            sl = slice(h * hd, (h + 1) * hd)
            kw = jnp.concatenate([_rope(kp_ref[:, sl].astype(F32), cp, sp), _rope(ko_ref[:, sl].astype(F32), cq, sq),
                                  _rope(kn_ref[:, sl].astype(F32), cn, sn)], axis=0).astype(BF16)
            vw = jnp.concatenate([vp_ref[:, sl], vo_ref[:, sl], vn_ref[:, sl]], axis=0)
            kc = kc_ref[:, sl]
            vc = vc_ref[:, sl]
            segs = [(0, 3 * blk, kw, vw, valid), (3 * blk, nctx, kc, vc, None)]
            write(h, _gqa_head(h, (qs * qscale).astype(BF16), segs, sink_ref, s_sc, p_sc, rep))


def _gqa(p, sink, cos2, sin_s, nbatch, seq, nctx):
    t = p.shape[0]
    hd = GQA_HEAD_DIM
    nkv = GQA_KV_HEADS
    dq = p.shape[1] - 2 * nkv * hd
    rep = dq // hd // nkv
    blk = WINDOW
    nb = seq // blk
    nctxb = nctx // blk
    latb = nbatch * seq // blk
    kvw = nkv * hd
    kcol = dq // kvw
    vcol = kcol + 1

    def qrow(b, s):
        return jnp.where(s < nctxb, latb + b * nctxb + s, b * nb + s - nctxb)

    def nidx(s):
        return jnp.maximum(s - nctxb, 0)

    def kvspec(off, col):
        return pl.BlockSpec((blk, kvw), lambda b, s: (b * nb + jnp.clip(nidx(s) + off, 0, nb - 1), col))

    ctx_blk = nbatch * seq // nctx
    return pl.pallas_call(
        functools.partial(_gqa_kernel, nb=nb, nctxb=nctxb, rep=rep),
        grid=(nbatch, nctxb + nb),
        in_specs=[pl.BlockSpec(memory_space=pltpu.SMEM),
                  pl.BlockSpec((blk, dq), lambda b, s: (qrow(b, s), 0)),
                  kvspec(-1, kcol), kvspec(0, kcol), kvspec(1, kcol),
                  kvspec(-1, vcol), kvspec(0, vcol), kvspec(1, vcol),
                  pl.BlockSpec((nctx, kvw), lambda b, s: (ctx_blk + b, kcol)),
                  pl.BlockSpec((nctx, kvw), lambda b, s: (ctx_blk + b, vcol)),
                  pl.BlockSpec((seq, hd), lambda b, s: (0, 0)),
                  pl.BlockSpec((seq, hd), lambda b, s: (0, 0))],
        out_specs=pl.BlockSpec((blk, dq), lambda b, s: (qrow(b, s), 0)),
        out_shape=jax.ShapeDtypeStruct((t, dq), BF16),
        scratch_shapes=[pltpu.VMEM((nkv, rep * blk, 3 * blk + nctx), F32),
                        pltpu.VMEM((nkv, rep * blk, 3 * blk + nctx), BF16)],
        compiler_params=_cparams(("arbitrary", "arbitrary"), 40),
        name="gqa_window_attn",
    )(sink, p, p, p, p, p, p, p, p, p, cos2, sin_s)


def _diff_kernel(lq1_ref, lk1_ref, lq2_ref, lk2_ref, g_ref, ql_ref, qc_ref, kl_ref, kc_ref, vl_ref, vc_ref,
                 cos_ref, sin_ref, ol_ref, oc_ref, k_sc, s_sc, e_sc, *, nctx, seq, tq, kchunk, lambda_init):
    t = pl.program_id(2)
    hd = DIFF_HEAD_DIM
    qscale = hd ** -0.5 * LOG2E

    @pl.when(t == 0)
    def _prep():
        for r0 in range(0, seq, kchunk):
            c, s = cos_ref[r0:r0 + kchunk, :], sin_ref[r0:r0 + kchunk, :]
            for tt in range(2):
                k_sc[r0:r0 + kchunk, tt * hd:(tt + 1) * hd] = _rope(
                    kl_ref[r0:r0 + kchunk, tt * hd:(tt + 1) * hd].astype(F32), c, s).astype(BF16)

    lam = (jnp.exp(jnp.sum(lq1_ref[...] * lk1_ref[...], axis=-1, keepdims=True))
           - jnp.exp(jnp.sum(lq2_ref[...] * lk2_ref[...], axis=-1, keepdims=True)) + lambda_init)

    ctx_chunks = [(c0, min(kchunk, nctx - c0), kc_ref, c0) for c0 in range(0, nctx, kchunk)]
    lat_chunks = [(nctx + c0, kchunk, k_sc, c0) for c0 in range(0, seq, kchunk)]

    def run(qs, chunks, nrows, o_ref, with_lat):
        for tt in range(2):
            q = (qs[tt] * qscale).astype(BF16)
            for (c0, cs, kr, r0) in chunks:
                s_sc[tt, 0:nrows, c0:c0 + cs] = _nt_dot(q, kr[r0:r0 + cs, tt * hd:(tt + 1) * hd])
        outs = []
        for tt in range(2):
            linv = []
            for r in range(0, nrows, SM_ROWS):
                rows = slice(r, r + SM_ROWS)
                m = jnp.full((SM_ROWS, 1), -jnp.inf, F32)
                for (c0, cs, _, _) in chunks:
                    m = jnp.maximum(m, jnp.max(s_sc[tt, rows, c0:c0 + cs], axis=-1, keepdims=True))
                l = jnp.zeros((SM_ROWS, 1), F32)
                for (c0, cs, _, _) in chunks:
                    e = jnp.exp2(s_sc[tt, rows, c0:c0 + cs] - m)
                    l = l + jnp.sum(e, axis=-1, keepdims=True)
                    e_sc[tt, rows, c0:c0 + cs] = e.astype(BF16)
                linv.append(1.0 / l)
            acc = jnp.dot(e_sc[tt, 0:nrows, 0:nctx], vc_ref[...], preferred_element_type=F32)
            if with_lat:
                acc = acc + jnp.dot(e_sc[tt, 0:nrows, nctx:nctx + seq], vl_ref[...], preferred_element_type=F32)
            outs.append(acc * jnp.concatenate(linv, axis=0))
        o = outs[0] - lam * outs[1]
        ms = jnp.mean(o * o, axis=-1, keepdims=True)
        o = o * lax.rsqrt(ms + EPS) * g_ref[...] * (1.0 - lambda_init)
        o_ref[...] = o.astype(o_ref.dtype)

    @pl.when(t == 0)
    def _ctx():
        run([qc_ref[:, 0:hd].astype(F32), qc_ref[:, hd:2 * hd].astype(F32)], ctx_chunks, nctx, oc_ref, False)

    @pl.when(t > 0)
    def _lat():
        p0 = pl.multiple_of((t - 1) * tq, tq)
        c, s = cos_ref[pl.ds(p0, tq), :], sin_ref[pl.ds(p0, tq), :]
        run([_rope(ql_ref[:, 0:hd].astype(F32), c, s), _rope(ql_ref[:, hd:2 * hd].astype(F32), c, s)],
            ctx_chunks + lat_chunks, tq, ol_ref, True)


def _diff_attn(p, lq1, lk1, lq2, lk2, subln_g, cos2, sin_s, nbatch, seq, nctx, lambda_init):
    hd = DIFF_HEAD_DIM
    vd = 2 * hd
    nh = p.shape[1] // (3 * vd)
    tq = 512
    nq = seq // tq
    ctx_blk = nbatch * seq // nctx
    kchunk = 512

    def lrow(b, s):
        return b * nq + jnp.maximum(s - 1, 0)

    vec = lambda a: a.reshape(1, -1)
    o_lat, o_ctx = pl.pallas_call(
        functools.partial(_diff_kernel, nctx=nctx, seq=seq, tq=tq, kchunk=kchunk, lambda_init=lambda_init),
        grid=(nbatch, nh, nq + 1),
        in_specs=[pl.BlockSpec((1, hd), lambda b, h, s: (0, 0))] * 4
        + [pl.BlockSpec((1, vd), lambda b, h, s: (0, 0)),
           pl.BlockSpec((tq, vd), lambda b, h, s: (lrow(b, s), h)),
           pl.BlockSpec((nctx, vd), lambda b, h, s: (ctx_blk + b, h)),
           pl.BlockSpec((seq, vd), lambda b, h, s: (b, nh + h)),
           pl.BlockSpec((nctx, vd), lambda b, h, s: (ctx_blk + b, nh + h)),
           pl.BlockSpec((seq, vd), lambda b, h, s: (b, 2 * nh + h)),
           pl.BlockSpec((nctx, vd), lambda b, h, s: (ctx_blk + b, 2 * nh + h)),
           pl.BlockSpec((seq, hd), lambda b, h, s: (0, 0)),
           pl.BlockSpec((seq, hd), lambda b, h, s: (0, 0))],
        out_specs=[pl.BlockSpec((tq, vd), lambda b, h, s: (lrow(b, s), h)),
                   pl.BlockSpec((nctx, vd), lambda b, h, s: (b, h))],
        out_shape=[jax.ShapeDtypeStruct((nbatch * seq, nh * vd), BF16),
                   jax.ShapeDtypeStruct((nbatch * nctx, nh * vd), BF16)],
        scratch_shapes=[pltpu.VMEM((seq, vd), BF16), pltpu.VMEM((2, tq, nctx + seq), F32),
                        pltpu.VMEM((2, tq, nctx + seq), BF16)],
        compiler_params=_cparams(("arbitrary", "arbitrary", "arbitrary"), 56),
        name="diff_attn",
    )(vec(lq1), vec(lk1), vec(lq2), vec(lk2), vec(subln_g), p, p, p, p, p, p, cos2, sin_s)
    return o_lat, o_ctx


def _ssd_conv_kernel(xm_ref, xp_ref, xn_ref, bcm_ref, bcp_ref, bcn_ref, dt_ref, wx_ref, wbc_ref, bx_ref,
                     bbc_ref, dtb_ref, sh_ref, xs_ref, bm_ref, cm_ref, dto_ref, *, rows, seq, nctx, nlat_rows,
                     slab):
    r = pl.program_id(0)
    row0 = r * rows
    in_lat = row0 < nlat_rows
    seg = jnp.where(in_lat, seq, nctx)
    off = jnp.where(in_lat, row0, row0 - nlat_rows)
    is_first = (off % seg) == 0
    is_last = ((off + rows) % seg) == 0
    halo = CONV_HALO
    pad = SSD_CONV // 2
    ngrp = SSD_GROUPS
    nst = SSD_STATE
    sub = lax.broadcasted_iota(jnp.int32, (SUBLANES, slab), 0)

    def conv_slab(m_ref, p_ref, n_ref, w_ref, b_ref, c0):
        main = m_ref[:, c0:c0 + slab]
        prev = jnp.where(is_first, 0.0, p_ref[:, c0:c0 + slab].astype(F32))
        nxt = jnp.where(is_last, 0.0, n_ref[:, c0:c0 + slab].astype(F32))
        acc = b_ref[:, c0:c0 + slab] + w_ref[pad:pad + 1, c0:c0 + slab] * main.astype(F32)
        for k in range(SSD_CONV):
            dlt = k - pad
            if dlt == 0:
                continue
            sh = jnp.dot(sh_ref[k], main, preferred_element_type=F32)
            if dlt < 0:
                top = sh[0:SUBLANES]
                for q in range(-dlt):
                    top = jnp.where(sub == q, prev[halo + dlt + q:halo + dlt + q + 1, :], top)
                sh = jnp.concatenate([top, sh[SUBLANES:]], axis=0)
            else:
                bot = sh[rows - SUBLANES:]
                for q in range(dlt):
                    bot = jnp.where(sub == SUBLANES - dlt + q, nxt[q:q + 1, :], bot)
                sh = jnp.concatenate([sh[:rows - SUBLANES], bot], axis=0)
            acc = acc + w_ref[k:k + 1, c0:c0 + slab] * sh
        return acc / (1.0 + jnp.exp(-acc))

    gw = xs_ref.shape[2]
    for c0 in range(0, ngrp * gw, slab):
        g, off = divmod(c0, gw)
        xs_ref[g, :, off:off + slab] = conv_slab(xm_ref, xp_ref, xn_ref, wx_ref, bx_ref, c0).astype(xs_ref.dtype)
    nbc = 2 * ngrp * nst // slab
    for sidx in range(nbc):
        y = conv_slab(bcm_ref, bcp_ref, bcn_ref, wbc_ref, bbc_ref, sidx * slab).astype(bm_ref.dtype)
        for q in range(slab // nst):
            gi = sidx * (slab // nst) + q
            if gi < ngrp:
                bm_ref[gi] = y[:, q * nst:(q + 1) * nst]
            else:
                cm_ref[gi - ngrp] = y[:, q * nst:(q + 1) * nst]
    v = dt_ref[...] + dtb_ref[...]
    dto_ref[...] = jnp.maximum(v, 0.0) + jnp.log1p(jnp.exp(-jnp.abs(v)))


def _ssd_conv(zx, dt_raw, conv_w, conv_b, dt_bias, d_inner, nbatch, seq, nctx):
    t = zx.shape[0]
    rows = 256
    slab = 256
    ngrp, nst = SSD_GROUPS, SSD_STATE
    bcw = 2 * ngrp * nst
    nh2 = dt_bias.size
    halo = CONV_HALO
    rb = rows // halo
    nhalo = t // halo
    xcol = 1
    bccol = 2 * d_inner // bcw
    wx, wbc = conv_w[:, :d_inner], conv_w[:, d_inner:]
    bx, bbc = conv_b[:d_inner].reshape(1, -1), conv_b[d_inner:].reshape(1, -1)
    shifts = jnp.stack([jnp.eye(rows, k=k - SSD_CONV // 2, dtype=BF16) for k in range(SSD_CONV)])
    prev = lambda r: jnp.maximum(r * rb - 1, 0)
    nxt = lambda r: jnp.minimum(r * rb + rb, nhalo - 1)
    return pl.pallas_call(
        functools.partial(_ssd_conv_kernel, rows=rows, seq=seq, nctx=nctx, nlat_rows=nbatch * seq, slab=slab),
        grid=(t // rows,),
        in_specs=[pl.BlockSpec((rows, d_inner), lambda r: (r, xcol)),
                  pl.BlockSpec((halo, d_inner), lambda r: (prev(r), xcol)),
                  pl.BlockSpec((halo, d_inner), lambda r: (nxt(r), xcol)),
                  pl.BlockSpec((rows, bcw), lambda r: (r, bccol)),
                  pl.BlockSpec((halo, bcw), lambda r: (prev(r), bccol)),
                  pl.BlockSpec((halo, bcw), lambda r: (nxt(r), bccol)),
                  pl.BlockSpec((rows, nh2), lambda r: (r, 0)),
                  pl.BlockSpec((SSD_CONV, d_inner), lambda r: (0, 0)),
                  pl.BlockSpec((SSD_CONV, bcw), lambda r: (0, 0)),
                  pl.BlockSpec((1, d_inner), lambda r: (0, 0)),
                  pl.BlockSpec((1, bcw), lambda r: (0, 0)),
                  pl.BlockSpec((1, nh2), lambda r: (0, 0)),
                  pl.BlockSpec((SSD_CONV, rows, rows), lambda r: (0, 0, 0))],
        out_specs=[pl.BlockSpec((ngrp, rows, d_inner // ngrp), lambda r: (0, r, 0)),
                   pl.BlockSpec((ngrp, rows, nst), lambda r: (0, r, 0)),
                   pl.BlockSpec((ngrp, rows, nst), lambda r: (0, r, 0)),
                   pl.BlockSpec((rows, nh2), lambda r: (r, 0))],
        out_shape=[jax.ShapeDtypeStruct((ngrp, t, d_inner // ngrp), BF16),
                   jax.ShapeDtypeStruct((ngrp, t, nst), BF16),
                   jax.ShapeDtypeStruct((ngrp, t, nst), BF16),
                   jax.ShapeDtypeStruct((t, nh2), F32)],
        compiler_params=_cparams(("arbitrary",), 40),
        name="ssd_conv",
    )(zx, zx, zx, zx, zx, zx, dt_raw, wx, wbc, bx, bbc, dt_bias.reshape(1, nh2), shifts)


def _ssd_scan_kernel(alog_ref, tri_ref, xf_ref, bf_ref, cf_ref, dtf_ref, xb_ref, bb_ref, cb_ref, dtb_ref,
                     yf_ref, yb_ref, st_sc, e_sc, et_sc, wt_sc, ext_sc, *, nheads):
    q = SSD_CHUNK
    ngrp = SSD_GROUPS
    hpg = nheads // ngrp
    p = SSD_HEAD_DIM

    @pl.when(pl.program_id(1) == 0)
    def _():
        st_sc[...] = jnp.zeros_like(st_sc)

    a = -jnp.exp(alog_ref[...]) * LOG2E
    tri = tri_ref[...]
    es, dts, ws = [], [], []
    for d, dt_ref in enumerate((dtf_ref, dtb_ref)):
        dt = dt_ref[:, d * nheads:(d + 1) * nheads]
        la = dt * a[:, d * nheads:(d + 1) * nheads]
        acum = jnp.dot(tri, la, preferred_element_type=F32, precision=lax.Precision.HIGHEST)
        tot = acum[q - 1:q, :]
        e = acum if d == 0 else tot - acum + la
        es.append(e)
        dts.append(dt)
        ws.append(dt * jnp.exp2(tot - e))
    e2 = jnp.concatenate(es, axis=1)
    e_sc[...] = e2
    e2t = e2.T
    et_sc[...] = (e2t - jnp.log2(jnp.concatenate(dts, axis=1).T)).reshape(2 * ngrp, hpg, q)
    wt_sc[...] = jnp.concatenate(ws, axis=1).T.reshape(2 * ngrp, hpg, q)
    tot_col = jnp.concatenate([e2t[0:nheads, q - 1:q], e2t[nheads:2 * nheads, 0:1]], axis=0)
    ext_sc[...] = jnp.broadcast_to(jnp.exp2(tot_col), (2 * nheads, q)).reshape(2 * ngrp, hpg, q)

    lane = lax.broadcasted_iota(jnp.int32, (q, 2 * nheads), 1)
    li = lax.broadcasted_iota(jnp.int32, (q, q), 0)
    si = lax.broadcasted_iota(jnp.int32, (q, q), 1)
    lo = lax.broadcasted_iota(jnp.int32, (q, 2 * p), 1) < p
    causal = (si <= li, si >= li)

    dirs = ((xf_ref, bf_ref, cf_ref, yf_ref), (xb_ref, bb_ref, cb_ref, yb_ref))

    def group(g, carry):
        pre = []
        for d, (x_ref, b_ref, c_ref, y_ref) in enumerate(dirs):
            bg = b_ref[g]
            cg = c_ref[g]
            st = st_sc[d, g]
            pre.append((_nt_dot(cg, bg),
                        bg.astype(F32).T,
                        st,
                        jnp.dot(cg, st.astype(BF16), preferred_element_type=F32),
                        et_sc[d * ngrp + g], wt_sc[d * ngrp + g], ext_sc[d * ngrp + g]))
        e_all = e_sc[...]
        for d, (x_ref, b_ref, c_ref, y_ref) in enumerate(dirs):
            cbm, bgt, st, z, et, wt, ext = pre[d]
            for jp in range(hpg // 2):
                ms, decc, bws, exts = [], [], [], []
                for j in (2 * jp, 2 * jp + 1):
                    col = d * nheads + g * hpg + j
                    sel = lane == col
                    ecol = jnp.sum(jnp.where(sel, e_all, 0.0), axis=1, keepdims=True)
                    decc.append(jnp.exp2(ecol))
                    lm = jnp.exp2(jnp.where(causal[d], ecol - et[j:j + 1, :], -jnp.inf))
                    ms.append((cbm * lm).astype(BF16))
                    bws.append((bgt * wt[j:j + 1, :]).astype(BF16))
                    exts.append(ext[j:j + 1, :])
                cols = slice(jp * 2 * p, (jp + 1) * 2 * p)
                xp = x_ref[g, :, cols]
                zero = jnp.zeros_like(xp)
                rhs = jnp.concatenate([jnp.where(lo, xp, zero), jnp.where(lo, zero, xp)], axis=0)
                y = jnp.dot(jnp.concatenate(ms, axis=1), rhs, preferred_element_type=F32)
                y = y + jnp.where(lo, decc[0], decc[1]) * z[:, cols]
                y_ref[g, :, cols] = y.astype(y_ref.dtype)
                contrib = jnp.dot(jnp.concatenate(bws, axis=1), rhs, preferred_element_type=F32)
                decay = jnp.where(lo[0:1, :], exts[0], exts[1])
                st_sc[d, g, :, cols] = st[:, cols] * decay + contrib
        return carry

    lax.fori_loop(0, ngrp, group, 0, unroll=4)


def _ssd_scan(xs_g, bm_g, cm_g, dt, a_log, nbatch, seq, nctx):
    ngrp, t, gw = xs_g.shape
    nst = bm_g.shape[2]
    q = SSD_CHUNK
    nheads = a_log.shape[1]
    hpg = nheads // ngrp
    nlat = seq // q
    nctxc = nctx // q
    latc = nbatch * nlat

    def fidx(b, s):
        return jnp.where(s < nctxc, latc + b * nctxc + s, b * nlat + s - nctxc)

    def bidx(b, s):
        return jnp.where(s < nctxc, latc + b * nctxc + nctxc - 1 - s, b * nlat + nlat - 1 - (s - nctxc))

    def specs(idx):
        return [pl.BlockSpec((ngrp, q, gw), lambda b, s: (0, idx(b, s), 0)),
                pl.BlockSpec((ngrp, q, nst), lambda b, s: (0, idx(b, s), 0)),
                pl.BlockSpec((ngrp, q, nst), lambda b, s: (0, idx(b, s), 0)),
                pl.BlockSpec((q, 2 * nheads), lambda b, s: (idx(b, s), 0))]

    tri = jnp.tril(jnp.ones((q, q), F32))
    yshape = jax.ShapeDtypeStruct((ngrp, t, gw), BF16)
    small = pltpu.VMEM((2 * ngrp, hpg, q), F32)
    return pl.pallas_call(
        functools.partial(_ssd_scan_kernel, nheads=nheads),
        grid=(nbatch, nctxc + nlat),
        in_specs=[pl.BlockSpec((1, 2 * nheads), lambda b, s: (0, 0)),
                  pl.BlockSpec((q, q), lambda b, s: (0, 0))] + specs(fidx) + specs(bidx),
        out_specs=[pl.BlockSpec((ngrp, q, gw), lambda b, s: (0, fidx(b, s), 0)),
                   pl.BlockSpec((ngrp, q, gw), lambda b, s: (0, bidx(b, s), 0))],
        out_shape=[yshape, yshape],
        scratch_shapes=[pltpu.VMEM((2, ngrp, nst, gw), F32), pltpu.VMEM((q, 2 * nheads), F32),
                        small, small, small],
        compiler_params=_cparams(("arbitrary", "arbitrary"), 48),
        name="ssd_scan",
    )(a_log.reshape(1, 2 * nheads), tri, xs_g, bm_g, cm_g, dt, xs_g, bm_g, cm_g, dt)


def _ssd_out_kernel(yf_ref, yb_ref, xs_ref, z_ref, dx_ref, ng_ref, w_ref, mod_ref, *rest, nsteps, gps, gw, split_at):
    o_ref, acc_sc = rest[-2:]
    s = pl.program_id(1)

    @pl.when(s == 0)
    def _():
        acc_sc[...] = jnp.zeros_like(acc_sc)

    acc = acc_sc[...]
    for g in range(gps):
        sl = slice(g * gw, (g + 1) * gw)
        z = z_ref[:, sl].astype(F32)
        y = yf_ref[g].astype(F32) + yb_ref[g].astype(F32) + xs_ref[g].astype(F32) * dx_ref[:, sl]
        y = y * (z / (1.0 + jnp.exp(-z)))
        ms = jnp.mean(y * y, axis=-1, keepdims=True)
        yn = (y * lax.rsqrt(ms + EPS) * ng_ref[:, sl]).astype(BF16)
        k0 = pl.multiple_of(s * (gps * gw) + g * gw, gw)
        acc = acc + jnp.dot(yn, w_ref[0, pl.ds(k0, gw), :], preferred_element_type=F32)

    @pl.when(s < nsteps - 1)
    def _():
        acc_sc[...] = acc

    @pl.when(s == nsteps - 1)
    def _():
        hres = rest[0][...] if split_at is None else _pick_part(pl.program_id(0), split_at, rest[0], rest[1])
        o_ref[...] = hres + mod_ref[0][2:3] * acc


def _ssd_out(yf, yb, xs_g, zx, d_skip, norm_g, w_out, layer, h, mods, rows, seq, nbatch, tm=512, gps=2):
    ngrp, _, gw = xs_g.shape
    d_inner = ngrp * gw
    d = w_out.shape[2]
    nsteps = ngrp // gps
    dx = jnp.repeat(d_skip, SSD_HEAD_DIM).reshape(1, d_inner)
    gspec = pl.BlockSpec((gps, tm, gw), lambda i, g: (g, i, 0))
    if isinstance(h, tuple):
        h_specs, split_at = _split_specs(h, tm, d, lambda i, g: 0)
        h_args = list(h)
    else:
        h_specs, split_at, h_args = [pl.BlockSpec((tm, d), lambda i, g: (i, 0))], None, [h]
    return pl.pallas_call(
        functools.partial(_ssd_out_kernel, nsteps=nsteps, gps=gps, gw=gw, split_at=split_at),
        grid=(rows // tm, nsteps),
        in_specs=[gspec, gspec, gspec,
                  pl.BlockSpec((tm, gps * gw), lambda i, g: (i, g)),
                  pl.BlockSpec((1, gps * gw), lambda i, g: (0, g)),
                  pl.BlockSpec((1, gps * gw), lambda i, g: (0, g)),
                  pl.BlockSpec((1, d_inner, d), lambda i, g: (layer, 0, 0), pipeline_mode=pl.Buffered(1)),
                  pl.BlockSpec((1, 6, d), lambda i, g: (jnp.minimum((i * tm) // seq, nbatch), 0, 0))] + h_specs,
        out_specs=pl.BlockSpec((tm, d), lambda i, g: (i, 0)),
        out_shape=jax.ShapeDtypeStruct((rows, d), F32),
        scratch_shapes=[pltpu.VMEM((tm, d), F32)],
        compiler_params=_cparams(("arbitrary", "arbitrary"), 56),
        name="ssd_gate_norm_out_proj",
    )(yf, yb, xs_g, zx, dx, norm_g.reshape(1, d_inner), w_out, mods, *h_args)


def kernel(x, c, ctx, c_ctx, ada_w, ada_b, norm_mix_g, norm_mlp_g, mlp_w1, mlp_w2, ssd_w_in, ssd_conv_w, ssd_conv_b, ssd_dt_bias, ssd_a_log, ssd_d, ssd_norm_g, ssd_w_out, gqa_w_qkv, gqa_sink, gqa_w_out, diff_w_qkv, diff_lam_q1, diff_lam_k1, diff_lam_q2, diff_lam_k2, diff_subln_g, diff_w_out, final_norm_g):
    nbatch, seq, d = x.shape
    nctx = ctx.shape[1]
    depth = ada_w.shape[0]
    nlat_rows = nbatch * seq
    t = nlat_rows + nbatch * nctx

    cond8 = jnp.zeros((SUBLANES, d), F32).at[:nbatch].set(c).at[nbatch].set(c_ctx)
    as_mods = lambda m: m[0, :nbatch + 1].reshape(nbatch + 1, 6, d)
    mods = as_mods(_ada_layers(cond8, ada_w, ada_b, 1))
    cos2, sin_s = _rope_tables(seq, GQA_HEAD_DIM)

    h = (x.reshape(nlat_rows, d), ctx.reshape(nbatch * nctx, d))

    def layer_weights(i):
        mix = ((ssd_w_in, ssd_w_out), (gqa_w_qkv, gqa_w_out), (diff_w_qkv, diff_w_out))[i % N_MIXERS]
        return [(mlp_w1, i), (mlp_w2, i), (mix[0], i // N_MIXERS), (mix[1], i // N_MIXERS)]

    w0 = layer_weights(0)
    wcur = [None, None, w0[2][0][w0[2][1]:w0[2][1] + 1].astype(BF16), None]
    for i in range(depth):
        kind, j = i % N_MIXERS, i // N_MIXERS
        need_ctx = i < depth - 1
        rows = t if need_ctx else nlat_rows
        w1_b, w2_b, win_b, wout_b = wcur
        assert kind == 0 or not isinstance(h, tuple)
        if kind == 0:
            nh2 = 2 * ssd_a_log.shape[2]
            first = [w0[0], w0[1], w0[3]] if i == 0 else []
            zx, dt_raw, casted = _proj(h, norm_mix_g[i], mods, win_b, 0, ssd_w_in.shape[2] - nh2, nh2, seq, nbatch,
                                       casts=first)
            if first:
                w1_b, w2_b, wout_b = casted
            xs_g, bm_g, cm_g, dt = _ssd_conv(zx, dt_raw, ssd_conv_w[j], ssd_conv_b[j], ssd_dt_bias[j],
                                             ssd_w_out.shape[1], nbatch, seq, nctx)
            yf, yb = _ssd_scan(xs_g, bm_g, cm_g, dt, ssd_a_log[j], nbatch, seq, nctx)
            h = _ssd_out(yf, yb, xs_g, zx, ssd_d[j], ssd_norm_g[j], wout_b, 0, h, mods, rows, seq, nbatch)
        elif kind == 1:
            p = _proj(h, norm_mix_g[i], mods, win_b, 0, gqa_w_qkv.shape[2], 0, seq, nbatch)[0]
            o = _gqa(p, gqa_sink[j], cos2, sin_s, nbatch, seq, nctx)
            h = _outproj(o, wout_b, 0, h, mods, rows, seq, nbatch)
        else:
            lambda_init = 0.8 - 0.6 * math.exp(-0.3 * i)
            p = _proj(h, norm_mix_g[i], mods, win_b, 0, diff_w_qkv.shape[2], 0, seq, nbatch)[0]
            o = _diff_attn(p, diff_lam_q1[j], diff_lam_k1[j], diff_lam_q2[j], diff_lam_k2[j], diff_subln_g[j],
                           cos2, sin_s, nbatch, seq, nctx, lambda_init)
            h = _outproj(o, wout_b, 0, h, mods, rows, seq, nbatch)
        h, wcur, mods_next = _mlp(h, norm_mlp_g[i], mods, w1_b, w2_b, 0, final_norm_g, rows, seq, nbatch,
                                  final_norm=not need_ctx, casts=layer_weights(i + 1) if need_ctx else (),
                                  ada=(cond8, ada_w, ada_b, i + 1) if need_ctx else None)
        if need_ctx:
            mods = as_mods(mods_next)
    return h[:nlat_rows].reshape(nbatch, seq, d)
```

```python
import functools
import math

import jax
import jax.numpy as jnp
from jax import lax
from jax.experimental import pallas as pl
from jax.experimental.pallas import tpu as pltpu

F32 = jnp.float32
BF16 = jnp.bfloat16

EPS = 1e-6
LOG2E = math.log2(math.e)
GRID_W = 64
ROPE_THETA = 10000.0
N_MIXERS = 3

SSD_HEAD_DIM = 64
SSD_GROUPS = 8
SSD_STATE = 128
SSD_CONV = 5
SSD_CHUNK = 128

GQA_HEAD_DIM = 128
GQA_KV_HEADS = 4
WINDOW = 128

DIFF_HEAD_DIM = 128
SM_ROWS = 128

LANES = 128
SUBLANES = 8
CONV_HALO = 16
CAST_ROW_ALIGN = 16
VMEM_CAP = 56 * 1024 * 1024


def _cparams(sem, vmem_mb):
    return pltpu.CompilerParams(dimension_semantics=sem,
                                vmem_limit_bytes=min(int(vmem_mb * 1024 * 1024), VMEM_CAP))


def _nt_dot(a, b):
    return lax.dot_general(a, b, (((1,), (1,)), ((), ())), preferred_element_type=F32)


def _pick_tile(n, cap):
    best = LANES
    t = LANES
    while t <= cap:
        if n % t == 0:
            best = t
        t += LANES
    return best


def _split_specs(parts, tm, width, col):
    lat, ctx = parts
    nl = lat.shape[0] // tm
    assert lat.shape[0] % tm == 0 and ctx.shape[0] % tm == 0
    return ([pl.BlockSpec((tm, width), lambda i, j: (jnp.minimum(i, nl - 1), col(i, j))),
             pl.BlockSpec((tm, width), lambda i, j: (jnp.maximum(i - nl, 0), col(i, j)),
                          pipeline_mode=pl.Buffered(1))], nl)


def _cast_specs(casts, nsteps, step):
    ins, outs, shapes = [], [], []
    for arr, idx in casts:
        _, r, c = arr.shape
        nblk = 1 << (nsteps.bit_length() - 1)
        while r % (nblk * CAST_ROW_ALIGN):
            nblk //= 2
        blk = lambda i, j, nblk=nblk: jnp.minimum(step(i, j), nblk - 1)
        ins.append(pl.BlockSpec((1, r // nblk, c), lambda i, j, idx=idx, blk=blk: (idx, blk(i, j), 0)))
        outs.append(pl.BlockSpec((1, r // nblk, c), lambda i, j, blk=blk: (0, blk(i, j), 0)))
        shapes.append(jax.ShapeDtypeStruct((1, r, c), BF16))
    return ins, outs, shapes, [a for a, _ in casts]


def _pick_part(i, nl, lat_ref, ctx_ref):
    return jnp.where(i >= nl, ctx_ref[...], lat_ref[...])


def _norm_mod(x, g, shift, scale):
    ms = jnp.mean(x * x, axis=-1, keepdims=True)
    y = x * lax.rsqrt(ms + EPS) * g
    return y * (1.0 + scale) + shift


def _ada_block(cond_ref, w_ref, b_ref, o_ref):
    a = cond_ref[...]
    s = a / (1.0 + jnp.exp(-a))
    o_ref[0] = jnp.dot(s.astype(BF16), w_ref[0].astype(BF16), preferred_element_type=F32) + b_ref[0]


def _ada_layers(cond8, ada_w, ada_b, depth):
    _, d, n = ada_w.shape
    tn = _pick_tile(n, 1024)
    return pl.pallas_call(
        _ada_block,
        grid=(depth, n // tn),
        in_specs=[pl.BlockSpec((SUBLANES, d), lambda l, j: (0, 0)),
                  pl.BlockSpec((1, d, tn), lambda l, j: (l, 0, j)),
                  pl.BlockSpec((1, 1, tn), lambda l, j: (l, 0, j))],
        out_specs=pl.BlockSpec((1, SUBLANES, tn), lambda l, j: (l, 0, j)),
        out_shape=jax.ShapeDtypeStruct((depth, SUBLANES, n), F32),
        compiler_params=_cparams(("arbitrary", "arbitrary"), 40),
        name="ada_mod",
    )(cond8, ada_w, ada_b.reshape(ada_w.shape[0], 1, n))


def _proj_kernel(*refs, n_tail, split_at, ncast):
    nh = 1 if split_at is None else 2
    nt = 1 if n_tail else 0
    g_ref, mod_ref, w_ref = refs[nh:nh + 3]
    rest = refs[nh + 3:]
    wt_ref = rest[0] if n_tail else None
    cast_src = rest[nt:nt + ncast]
    o_ref = rest[nt + ncast]
    ot_ref = rest[nt + ncast + 1] if n_tail else None
    cast_dst = rest[2 * nt + ncast + 1:2 * nt + 2 * ncast + 1]
    u_sc = rest[-1]

    @pl.when(pl.program_id(1) == 0)
    def _():
        m = mod_ref[0]
        x = refs[0][...] if split_at is None else _pick_part(pl.program_id(0), split_at, refs[0], refs[1])
        u = _norm_mod(x, g_ref[...], m[0:1], m[1:2]).astype(BF16)
        u_sc[...] = u
        if n_tail:
            ot_ref[...] = jnp.dot(u, wt_ref[0], preferred_element_type=F32)

    o_ref[...] = jnp.dot(u_sc[...], w_ref[0], preferred_element_type=F32).astype(o_ref.dtype)
    for src, dst in zip(cast_src, cast_dst):
        dst[...] = src[...].astype(BF16)


def _proj(h, g, mods, w, layer, ncols, n_tail, seq, nbatch, casts=(), tm=512):
    if isinstance(h, tuple):
        t, d = h[0].shape[0] + h[1].shape[0], h[0].shape[1]
        h_specs, split_at = _split_specs(h, tm, d, lambda i, j: 0)
        h_args = list(h)
    else:
        t, d = h.shape
        h_specs, split_at, h_args = [pl.BlockSpec((tm, d), lambda i, j: (i, 0))], None, [h]
    tn = _pick_tile(ncols, 3072)
    in_specs = h_specs + [pl.BlockSpec((1, d), lambda i, j: (0, 0)),
                          pl.BlockSpec((1, 6, d), lambda i, j: (jnp.minimum((i * tm) // seq, nbatch), 0, 0)),
                          pl.BlockSpec((1, d, tn), lambda i, j: (layer, 0, j))]
    out_specs = [pl.BlockSpec((tm, tn), lambda i, j: (i, j))]
    out_shape = [jax.ShapeDtypeStruct((t, ncols), BF16)]
    args = h_args + [g.reshape(1, d), mods, w]
    if n_tail:
        assert ncols % n_tail == 0
        tail_blk = ncols // n_tail
        in_specs.append(pl.BlockSpec((1, d, n_tail), lambda i, j: (layer, 0, tail_blk)))
        out_specs.append(pl.BlockSpec((tm, n_tail), lambda i, j: (i, 0)))
        out_shape.append(jax.ShapeDtypeStruct((t, n_tail), F32))
        args.append(w)
    nj = ncols // tn
    c_in, c_out, c_shape, c_args = _cast_specs(casts, (t // tm) * nj, lambda i, j: i * nj + j)
    outs = pl.pallas_call(
        functools.partial(_proj_kernel, n_tail=n_tail, split_at=split_at, ncast=len(casts)),
        grid=(t // tm, nj),
        in_specs=in_specs + c_in,
        out_specs=out_specs + c_out,
        out_shape=out_shape + c_shape,
        scratch_shapes=[pltpu.VMEM((tm, d), BF16)],
        compiler_params=_cparams(("arbitrary", "arbitrary"), 56 if casts else 48),
        name="norm_mod_proj",
    )(*args, *c_args)
    nmain = 2 if n_tail else 1
    return outs[0], (outs[1] if n_tail else None), list(outs[nmain:])


def _mlp_kernel(h_ref, g_ref, mod_ref, w1_ref, w2_ref, fg_ref, *rest, nk, final_norm, ncast, with_ada):
    na = 3 if with_ada else 0
    cast_src, ada_in = rest[:ncast], rest[ncast:ncast + na]
    o_ref = rest[ncast + na]
    cast_dst = rest[ncast + na + 1:2 * ncast + na + 1]
    u_sc, acc_sc = rest[-2:]
    k = pl.program_id(1)

    @pl.when(k == 0)
    def _():
        m = mod_ref[0]
        u = _norm_mod(h_ref[...], g_ref[...], m[3:4], m[4:5])
        u_sc[...] = u.astype(BF16)
        acc_sc[...] = jnp.zeros_like(acc_sc)

    hk = jnp.dot(u_sc[...], w1_ref[0], preferred_element_type=F32)
    hk = jnp.square(jnp.maximum(hk, 0.0)).astype(BF16)
    acc_sc[...] += jnp.dot(hk, w2_ref[0], preferred_element_type=F32)
    for src, dst in zip(cast_src, cast_dst):
        dst[...] = src[...].astype(BF16)
    if with_ada:
        _ada_block(*ada_in, rest[2 * ncast + na + 1])

    @pl.when(k == nk - 1)
    def _():
        m = mod_ref[0]
        out = h_ref[...] + m[5:6] * acc_sc[...]
        if final_norm:
            ms = jnp.mean(out * out, axis=-1, keepdims=True)
            out = out * lax.rsqrt(ms + EPS) * fg_ref[...]
        o_ref[...] = out


def _mlp(h, g, mods, w1, w2, layer, final_g, rows, seq, nbatch, final_norm, casts=(), ada=None, tm=512, tk=1024):
    d = h.shape[1]
    hid = w1.shape[2]
    nk = hid // tk
    nsteps = (rows // tm) * nk
    step = lambda i, k: i * nk + k
    cast_in, cast_out, cast_shape, cast_args = _cast_specs(casts, nsteps, step)
    if ada is not None:
        cond8, ada_w, ada_b, la = ada
        n = ada_w.shape[2]
        nblk = max(b for b in range(1, n // LANES + 1) if (n // LANES) % b == 0 and b <= nsteps)
        tn = n // nblk
        ablk = lambda i, k: jnp.minimum(step(i, k), nblk - 1)
        cast_in += [pl.BlockSpec((SUBLANES, d), lambda i, k: (0, 0)),
                    pl.BlockSpec((1, d, tn), lambda i, k: (la, 0, ablk(i, k))),
                    pl.BlockSpec((1, 1, tn), lambda i, k: (la, 0, ablk(i, k)))]
        cast_out.append(pl.BlockSpec((1, SUBLANES, tn), lambda i, k: (0, 0, ablk(i, k))))
        cast_shape.append(jax.ShapeDtypeStruct((1, SUBLANES, n), F32))
        cast_args += [cond8, ada_w, ada_b.reshape(ada_w.shape[0], 1, n)]
    outs = pl.pallas_call(
        functools.partial(_mlp_kernel, nk=nk, final_norm=final_norm, ncast=len(casts), with_ada=ada is not None),
        grid=(rows // tm, nk),
        in_specs=[pl.BlockSpec((tm, d), lambda i, k: (i, 0)),
                  pl.BlockSpec((1, d), lambda i, k: (0, 0)),
                  pl.BlockSpec((1, 6, d), lambda i, k: (jnp.minimum((i * tm) // seq, nbatch), 0, 0)),
                  pl.BlockSpec((1, d, tk), lambda i, k: (layer, 0, k)),
                  pl.BlockSpec((1, tk, d), lambda i, k: (layer, k, 0)),
                  pl.BlockSpec((1, d), lambda i, k: (0, 0))] + cast_in,
        out_specs=[pl.BlockSpec((tm, d), lambda i, k: (i, 0))] + cast_out,
        out_shape=[jax.ShapeDtypeStruct((rows, d), F32)] + cast_shape,
        scratch_shapes=[pltpu.VMEM((tm, d), BF16), pltpu.VMEM((tm, d), F32)],
        compiler_params=_cparams(("arbitrary", "arbitrary"), 56),
        name="mlp",
    )(h, g.reshape(1, d), mods, w1, w2, final_g.reshape(1, d), *cast_args)
    ncast = len(casts)
    return outs[0], list(outs[1:1 + ncast]), (outs[1 + ncast] if ada is not None else None)


def _outproj_kernel(*refs, split_at):
    ny = 1 if split_at is None else 2
    w_ref, h_ref, mod_ref, o_ref = refs[ny:]
    y = refs[0][...] if split_at is None else _pick_part(pl.program_id(0), split_at, refs[0], refs[1])
    acc = jnp.dot(y, w_ref[0], preferred_element_type=F32)
    o_ref[...] = h_ref[...] + mod_ref[0][2:3] * acc


def _outproj(y, w, layer, h, mods, rows, seq, nbatch, tm=512):
    d = w.shape[2]
    tn = d
    if isinstance(y, tuple):
        kdim = y[0].shape[1]
        y_specs, split_at = _split_specs(y, tm, kdim, lambda i, j: 0)
        y_args = list(y)
    else:
        kdim = y.shape[1]
        y_specs, split_at, y_args = [pl.BlockSpec((tm, kdim), lambda i, j: (i, 0))], None, [y]
    return pl.pallas_call(
        functools.partial(_outproj_kernel, split_at=split_at),
        grid=(rows // tm, d // tn),
        in_specs=y_specs + [
                  pl.BlockSpec((1, kdim, tn), lambda i, j: (layer, 0, j)),
                  pl.BlockSpec((tm, tn), lambda i, j: (i, j)),
                  pl.BlockSpec((1, 6, tn), lambda i, j: (jnp.minimum((i * tm) // seq, nbatch), 0, j))],
        out_specs=pl.BlockSpec((tm, tn), lambda i, j: (i, j)),
        out_shape=jax.ShapeDtypeStruct((rows, d), F32),
        compiler_params=_cparams(("arbitrary", "arbitrary"), 48),
        name="out_proj",
    )(*y_args, w, h, mods)


def _rope_tables(n_tokens, head_dim):
    rows = n_tokens // GRID_W
    row = jnp.repeat(jnp.arange(rows, dtype=F32), GRID_W)
    col = jnp.tile(jnp.arange(GRID_W, dtype=F32), rows)
    n_freq = head_dim // 4
    inv_freq = ROPE_THETA ** (-jnp.arange(n_freq, dtype=F32) / n_freq)
    ang = jnp.concatenate([row[:, None] * inv_freq, col[:, None] * inv_freq], axis=-1)
    cos, sin = jnp.cos(ang), jnp.sin(ang)
    return jnp.concatenate([cos, cos], axis=-1), jnp.concatenate([-sin, sin], axis=-1)


def _rope(x, cos2, sin_s):
    return x * cos2 + pltpu.roll(x, x.shape[-1] // 2, 1) * sin_s


def _gqa_head(h, qs, segs, sink_ref, s_sc, p_sc, rep):
    blk = WINDOW
    for (c0, w, k_, _, _) in segs:
        s_sc[h, :, c0:c0 + w] = _nt_dot(qs, k_)
    linv = []
    for r in range(rep):
        rows = slice(r * blk, (r + 1) * blk)
        sink = sink_ref[h * rep + r] * LOG2E
        m = jnp.full((blk, 1), sink, F32)
        svals = []
        for (c0, w, _, _, msk) in segs:
            s = s_sc[h, rows, c0:c0 + w]
            if msk is not None:
                s = jnp.where(msk, s, -jnp.inf)
            svals.append(s)
            m = jnp.maximum(m, jnp.max(s, axis=-1, keepdims=True))
        l = jnp.exp2(sink - m)
        for (c0, w, _, _, _), s in zip(segs, svals):
            p = jnp.exp2(s - m)
            l = l + jnp.sum(p, axis=-1, keepdims=True)
            p_sc[h, rows, c0:c0 + w] = p.astype(BF16)
        linv.append(1.0 / l)
    acc = None
    for (c0, w, _, v_, _) in segs:
        pv = jnp.dot(p_sc[h, :, c0:c0 + w], v_, preferred_element_type=F32)
        acc = pv if acc is None else acc + pv
    return acc * jnp.concatenate(linv, axis=0)


def _gqa_kernel(sink_ref, q_ref, kp_ref, ko_ref, kn_ref, vp_ref, vo_ref, vn_ref, kc_ref, vc_ref,
                cos_ref, sin_ref, o_ref, s_sc, p_sc, *, nb, nctxb, rep):
    t = pl.program_id(1)
    hd = GQA_HEAD_DIM
    blk = WINDOW
    nctx = kc_ref.shape[0]
    qscale = hd ** -0.5 * LOG2E

    def write(h, o):
        for r in range(rep):
            c0 = (h * rep + r) * hd
            o_ref[:, c0:c0 + hd] = o[r * blk:(r + 1) * blk].astype(o_ref.dtype)

    @pl.when(t < nctxb)
    def _ctx():
        for h in range(GQA_KV_HEADS):
            qs = jnp.concatenate([q_ref[:, (h * rep + r) * hd:(h * rep + r + 1) * hd] for r in range(rep)], axis=0)
            qs = (qs.astype(F32) * qscale).astype(BF16)
            kc = kc_ref[:, h * hd:(h + 1) * hd]
            vc = vc_ref[:, h * hd:(h + 1) * hd]
            write(h, _gqa_head(h, qs, [(0, nctx, kc, vc, None)], sink_ref, s_sc, p_sc, rep))

    @pl.when(t >= nctxb)
    def _lat():
        n = t - nctxb
        pq = pl.multiple_of(n * blk, blk)
        pp = pl.multiple_of(jnp.maximum(n - 1, 0) * blk, blk)
        pn = pl.multiple_of(jnp.minimum(n + 1, nb - 1) * blk, blk)
        cq, sq = cos_ref[pl.ds(pq, blk), :], sin_ref[pl.ds(pq, blk), :]
        cp, sp = cos_ref[pl.ds(pp, blk), :], sin_ref[pl.ds(pp, blk), :]
        cn, sn = cos_ref[pl.ds(pn, blk), :], sin_ref[pl.ds(pn, blk), :]
        i = lax.broadcasted_iota(jnp.int32, (blk, 3 * blk), 0)
        j = lax.broadcasted_iota(jnp.int32, (blk, 3 * blk), 1)
        rel = j - blk - i
        valid = (rel <= WINDOW) & (rel >= -WINDOW)
        valid = valid & ((j >= blk) | (n >= 1)) & ((j < 2 * blk) | (n <= nb - 2))
        for h in range(GQA_KV_HEADS):
            qs = jnp.concatenate(
                [_rope(q_ref[:, (h * rep + r) * hd:(h * rep + r + 1) * hd].astype(F32), cq, sq)
                 for r in range(rep)], axis=0)
            sl = slice(h * hd, (h + 1) * hd)
            kw = jnp.concatenate([_rope(kp_ref[:, sl].astype(F32), cp, sp), _rope(ko_ref[:, sl].astype(F32), cq, sq),
                                  _rope(kn_ref[:, sl].astype(F32), cn, sn)], axis=0).astype(BF16)
            vw = jnp.concatenate([vp_ref[:, sl], vo_ref[:, sl], vn_ref[:, sl]], axis=0)
            kc = kc_ref[:, sl]
            vc = vc_ref[:, sl]
            segs = [(0, 3 * blk, kw, vw, valid), (3 * blk, nctx, kc, vc, None)]
            write(h, _gqa_head(h, (qs * qscale).astype(BF16), segs, sink_ref, s_sc, p_sc, rep))


def _gqa(p, sink, cos2, sin_s, nbatch, seq, nctx):
    t = p.shape[0]
    hd = GQA_HEAD_DIM
    nkv = GQA_KV_HEADS
    dq = p.shape[1] - 2 * nkv * hd
    rep = dq // hd // nkv
    blk = WINDOW
    nb = seq // blk
    nctxb = nctx // blk
    latb = nbatch * seq // blk
    kvw = nkv * hd
    kcol = dq // kvw
    vcol = kcol + 1

    def qrow(b, s):
        return jnp.where(s < nctxb, latb + b * nctxb + s, b * nb + s - nctxb)

    def nidx(s):
        return jnp.maximum(s - nctxb, 0)

    def kvspec(off, col):
        return pl.BlockSpec((blk, kvw), lambda b, s: (b * nb + jnp.clip(nidx(s) + off, 0, nb - 1), col))

    ctx_blk = nbatch * seq // nctx
    return pl.pallas_call(
        functools.partial(_gqa_kernel, nb=nb, nctxb=nctxb, rep=rep),
        grid=(nbatch, nctxb + nb),
        in_specs=[pl.BlockSpec(memory_space=pltpu.SMEM),
                  pl.BlockSpec((blk, dq), lambda b, s: (qrow(b, s), 0)),
                  kvspec(-1, kcol), kvspec(0, kcol), kvspec(1, kcol),
                  kvspec(-1, vcol), kvspec(0, vcol), kvspec(1, vcol),
                  pl.BlockSpec((nctx, kvw), lambda b, s: (ctx_blk + b, kcol)),
                  pl.BlockSpec((nctx, kvw), lambda b, s: (ctx_blk + b, vcol)),
                  pl.BlockSpec((seq, hd), lambda b, s: (0, 0)),
                  pl.BlockSpec((seq, hd), lambda b, s: (0, 0))],
        out_specs=pl.BlockSpec((blk, dq), lambda b, s: (qrow(b, s), 0)),
        out_shape=jax.ShapeDtypeStruct((t, dq), BF16),
        scratch_shapes=[pltpu.VMEM((nkv, rep * blk, 3 * blk + nctx), F32),
                        pltpu.VMEM((nkv, rep * blk, 3 * blk + nctx), BF16)],
        compiler_params=_cparams(("arbitrary", "arbitrary"), 40),
        name="gqa_window_attn",
    )(sink, p, p, p, p, p, p, p, p, p, cos2, sin_s)


def _diff_kernel(lq1_ref, lk1_ref, lq2_ref, lk2_ref, g_ref, ql_ref, qc_ref, kl_ref, kc_ref, vl_ref, vc_ref,
                 cos_ref, sin_ref, ol_ref, oc_ref, k_sc, s_sc, e_sc, *, nctx, seq, tq, kchunk, lambda_init):
    t = pl.program_id(2)
    hd = DIFF_HEAD_DIM
    qscale = hd ** -0.5 * LOG2E

    @pl.when(t == 0)
    def _prep():
        for r0 in range(0, seq, kchunk):
            c, s = cos_ref[r0:r0 + kchunk, :], sin_ref[r0:r0 + kchunk, :]
            for tt in range(2):
                k_sc[r0:r0 + kchunk, tt * hd:(tt + 1) * hd] = _rope(
                    kl_ref[r0:r0 + kchunk, tt * hd:(tt + 1) * hd].astype(F32), c, s).astype(BF16)

    lam = (jnp.exp(jnp.sum(lq1_ref[...] * lk1_ref[...], axis=-1, keepdims=True))
           - jnp.exp(jnp.sum(lq2_ref[...] * lk2_ref[...], axis=-1, keepdims=True)) + lambda_init)

    ctx_chunks = [(c0, min(kchunk, nctx - c0), kc_ref, c0) for c0 in range(0, nctx, kchunk)]
    lat_chunks = [(nctx + c0, kchunk, k_sc, c0) for c0 in range(0, seq, kchunk)]

    def run(qs, chunks, nrows, o_ref, with_lat):
        for tt in range(2):
            q = (qs[tt] * qscale).astype(BF16)
            for (c0, cs, kr, r0) in chunks:
                s_sc[tt, 0:nrows, c0:c0 + cs] = _nt_dot(q, kr[r0:r0 + cs, tt * hd:(tt + 1) * hd])
        outs = []
        for tt in range(2):
            linv = []
            for r in range(0, nrows, SM_ROWS):
                rows = slice(r, r + SM_ROWS)
                m = jnp.full((SM_ROWS, 1), -jnp.inf, F32)
                for (c0, cs, _, _) in chunks:
                    m = jnp.maximum(m, jnp.max(s_sc[tt, rows, c0:c0 + cs], axis=-1, keepdims=True))
                l = jnp.zeros((SM_ROWS, 1), F32)
                for (c0, cs, _, _) in chunks:
                    e = jnp.exp2(s_sc[tt, rows, c0:c0 + cs] - m)
                    l = l + jnp.sum(e, axis=-1, keepdims=True)
                    e_sc[tt, rows, c0:c0 + cs] = e.astype(BF16)
                linv.append(1.0 / l)
            acc = jnp.dot(e_sc[tt, 0:nrows, 0:nctx], vc_ref[...], preferred_element_type=F32)
            if with_lat:
                acc = acc + jnp.dot(e_sc[tt, 0:nrows, nctx:nctx + seq], vl_ref[...], preferred_element_type=F32)
            outs.append(acc * jnp.concatenate(linv, axis=0))
        o = outs[0] - lam * outs[1]
        ms = jnp.mean(o * o, axis=-1, keepdims=True)
        o = o * lax.rsqrt(ms + EPS) * g_ref[...] * (1.0 - lambda_init)
        o_ref[...] = o.astype(o_ref.dtype)

    @pl.when(t == 0)
    def _ctx():
        run([qc_ref[:, 0:hd].astype(F32), qc_ref[:, hd:2 * hd].astype(F32)], ctx_chunks, nctx, oc_ref, False)

    @pl.when(t > 0)
    def _lat():
        p0 = pl.multiple_of((t - 1) * tq, tq)
        c, s = cos_ref[pl.ds(p0, tq), :], sin_ref[pl.ds(p0, tq), :]
        run([_rope(ql_ref[:, 0:hd].astype(F32), c, s), _rope(ql_ref[:, hd:2 * hd].astype(F32), c, s)],
            ctx_chunks + lat_chunks, tq, ol_ref, True)


def _diff_attn(p, lq1, lk1, lq2, lk2, subln_g, cos2, sin_s, nbatch, seq, nctx, lambda_init):
    hd = DIFF_HEAD_DIM
    vd = 2 * hd
    nh = p.shape[1] // (3 * vd)
    tq = 512
    nq = seq // tq
    ctx_blk = nbatch * seq // nctx
    kchunk = 512

    def lrow(b, s):
        return b * nq + jnp.maximum(s - 1, 0)

    vec = lambda a: a.reshape(1, -1)
    o_lat, o_ctx = pl.pallas_call(
        functools.partial(_diff_kernel, nctx=nctx, seq=seq, tq=tq, kchunk=kchunk, lambda_init=lambda_init),
        grid=(nbatch, nh, nq + 1),
        in_specs=[pl.BlockSpec((1, hd), lambda b, h, s: (0, 0))] * 4
        + [pl.BlockSpec((1, vd), lambda b, h, s: (0, 0)),
           pl.BlockSpec((tq, vd), lambda b, h, s: (lrow(b, s), h)),
           pl.BlockSpec((nctx, vd), lambda b, h, s: (ctx_blk + b, h)),
           pl.BlockSpec((seq, vd), lambda b, h, s: (b, nh + h)),
           pl.BlockSpec((nctx, vd), lambda b, h, s: (ctx_blk + b, nh + h)),
           pl.BlockSpec((seq, vd), lambda b, h, s: (b, 2 * nh + h)),
           pl.BlockSpec((nctx, vd), lambda b, h, s: (ctx_blk + b, 2 * nh + h)),
           pl.BlockSpec((seq, hd), lambda b, h, s: (0, 0)),
           pl.BlockSpec((seq, hd), lambda b, h, s: (0, 0))],
        out_specs=[pl.BlockSpec((tq, vd), lambda b, h, s: (lrow(b, s), h)),
                   pl.BlockSpec((nctx, vd), lambda b, h, s: (b, h))],
        out_shape=[jax.ShapeDtypeStruct((nbatch * seq, nh * vd), BF16),
                   jax.ShapeDtypeStruct((nbatch * nctx, nh * vd), BF16)],
        scratch_shapes=[pltpu.VMEM((seq, vd), BF16), pltpu.VMEM((2, tq, nctx + seq), F32),
                        pltpu.VMEM((2, tq, nctx + seq), BF16)],
        compiler_params=_cparams(("arbitrary", "arbitrary", "arbitrary"), 56),
        name="diff_attn",
    )(vec(lq1), vec(lk1), vec(lq2), vec(lk2), vec(subln_g), p, p, p, p, p, p, cos2, sin_s)
    return o_lat, o_ctx


def _ssd_conv_kernel(xm_ref, xp_ref, xn_ref, bcm_ref, bcp_ref, bcn_ref, dt_ref, wx_ref, wbc_ref, bx_ref,
                     bbc_ref, dtb_ref, sh_ref, xs_ref, bm_ref, cm_ref, dto_ref, *, rows, seq, nctx, nlat_rows,
                     slab):
    r = pl.program_id(0)
    row0 = r * rows
    in_lat = row0 < nlat_rows
    seg = jnp.where(in_lat, seq, nctx)
    off = jnp.where(in_lat, row0, row0 - nlat_rows)
    is_first = (off % seg) == 0
    is_last = ((off + rows) % seg) == 0
    halo = CONV_HALO
    pad = SSD_CONV // 2
    ngrp = SSD_GROUPS
    nst = SSD_STATE
    sub = lax.broadcasted_iota(jnp.int32, (SUBLANES, slab), 0)

    def conv_slab(m_ref, p_ref, n_ref, w_ref, b_ref, c0):
        main = m_ref[:, c0:c0 + slab]
        prev = jnp.where(is_first, 0.0, p_ref[:, c0:c0 + slab].astype(F32))
        nxt = jnp.where(is_last, 0.0, n_ref[:, c0:c0 + slab].astype(F32))
        acc = b_ref[:, c0:c0 + slab] + w_ref[pad:pad + 1, c0:c0 + slab] * main.astype(F32)
        for k in range(SSD_CONV):
            dlt = k - pad
            if dlt == 0:
                continue
            sh = jnp.dot(sh_ref[k], main, preferred_element_type=F32)
            if dlt < 0:
                top = sh[0:SUBLANES]
                for q in range(-dlt):
                    top = jnp.where(sub == q, prev[halo + dlt + q:halo + dlt + q + 1, :], top)
                sh = jnp.concatenate([top, sh[SUBLANES:]], axis=0)
            else:
                bot = sh[rows - SUBLANES:]
                for q in range(dlt):
                    bot = jnp.where(sub == SUBLANES - dlt + q, nxt[q:q + 1, :], bot)
                sh = jnp.concatenate([sh[:rows - SUBLANES], bot], axis=0)
            acc = acc + w_ref[k:k + 1, c0:c0 + slab] * sh
        return acc / (1.0 + jnp.exp(-acc))

    gw = xs_ref.shape[2]
    for c0 in range(0, ngrp * gw, slab):
        g, off = divmod(c0, gw)
        xs_ref[g, :, off:off + slab] = conv_slab(xm_ref, xp_ref, xn_ref, wx_ref, bx_ref, c0).astype(xs_ref.dtype)
    nbc = 2 * ngrp * nst // slab
    for sidx in range(nbc):
        y = conv_slab(bcm_ref, bcp_ref, bcn_ref, wbc_ref, bbc_ref, sidx * slab).astype(bm_ref.dtype)
        for q in range(slab // nst):
            gi = sidx * (slab // nst) + q
            if gi < ngrp:
                bm_ref[gi] = y[:, q * nst:(q + 1) * nst]
            else:
                cm_ref[gi - ngrp] = y[:, q * nst:(q + 1) * nst]
    v = dt_ref[...] + dtb_ref[...]
    dto_ref[...] = jnp.maximum(v, 0.0) + jnp.log1p(jnp.exp(-jnp.abs(v)))


def _ssd_conv(zx, dt_raw, conv_w, conv_b, dt_bias, d_inner, nbatch, seq, nctx):
    t = zx.shape[0]
    rows = 256
    slab = 256
    ngrp, nst = SSD_GROUPS, SSD_STATE
    bcw = 2 * ngrp * nst
    nh2 = dt_bias.size
    halo = CONV_HALO
    rb = rows // halo
    nhalo = t // halo
    xcol = 1
    bccol = 2 * d_inner // bcw
    wx, wbc = conv_w[:, :d_inner], conv_w[:, d_inner:]
    bx, bbc = conv_b[:d_inner].reshape(1, -1), conv_b[d_inner:].reshape(1, -1)
    shifts = jnp.stack([jnp.eye(rows, k=k - SSD_CONV // 2, dtype=BF16) for k in range(SSD_CONV)])
    prev = lambda r: jnp.maximum(r * rb - 1, 0)
    nxt = lambda r: jnp.minimum(r * rb + rb, nhalo - 1)
    return pl.pallas_call(
        functools.partial(_ssd_conv_kernel, rows=rows, seq=seq, nctx=nctx, nlat_rows=nbatch * seq, slab=slab),
        grid=(t // rows,),
        in_specs=[pl.BlockSpec((rows, d_inner), lambda r: (r, xcol)),
                  pl.BlockSpec((halo, d_inner), lambda r: (prev(r), xcol)),
                  pl.BlockSpec((halo, d_inner), lambda r: (nxt(r), xcol)),
                  pl.BlockSpec((rows, bcw), lambda r: (r, bccol)),
                  pl.BlockSpec((halo, bcw), lambda r: (prev(r), bccol)),
                  pl.BlockSpec((halo, bcw), lambda r: (nxt(r), bccol)),
                  pl.BlockSpec((rows, nh2), lambda r: (r, 0)),
                  pl.BlockSpec((SSD_CONV, d_inner), lambda r: (0, 0)),
                  pl.BlockSpec((SSD_CONV, bcw), lambda r: (0, 0)),
                  pl.BlockSpec((1, d_inner), lambda r: (0, 0)),
                  pl.BlockSpec((1, bcw), lambda r: (0, 0)),
                  pl.BlockSpec((1, nh2), lambda r: (0, 0)),
                  pl.BlockSpec((SSD_CONV, rows, rows), lambda r: (0, 0, 0))],
        out_specs=[pl.BlockSpec((ngrp, rows, d_inner // ngrp), lambda r: (0, r, 0)),
                   pl.BlockSpec((ngrp, rows, nst), lambda r: (0, r, 0)),
                   pl.BlockSpec((ngrp, rows, nst), lambda r: (0, r, 0)),
                   pl.BlockSpec((rows, nh2), lambda r: (r, 0))],
        out_shape=[jax.ShapeDtypeStruct((ngrp, t, d_inner // ngrp), BF16),
                   jax.ShapeDtypeStruct((ngrp, t, nst), BF16),
                   jax.ShapeDtypeStruct((ngrp, t, nst), BF16),
                   jax.ShapeDtypeStruct((t, nh2), F32)],
        compiler_params=_cparams(("arbitrary",), 40),
        name="ssd_conv",
    )(zx, zx, zx, zx, zx, zx, dt_raw, wx, wbc, bx, bbc, dt_bias.reshape(1, nh2), shifts)


def _ssd_scan_kernel(alog_ref, tri_ref, xf_ref, bf_ref, cf_ref, dtf_ref, xb_ref, bb_ref, cb_ref, dtb_ref,
                     yf_ref, yb_ref, st_sc, e_sc, et_sc, wt_sc, ext_sc, *, nheads):
    q = SSD_CHUNK
    ngrp = SSD_GROUPS
    hpg = nheads // ngrp
    p = SSD_HEAD_DIM

    @pl.when(pl.program_id(1) == 0)
    def _():
        st_sc[...] = jnp.zeros_like(st_sc)

    a = -jnp.exp(alog_ref[...]) * LOG2E
    tri = tri_ref[...]
    es, dts, ws = [], [], []
    for d, dt_ref in enumerate((dtf_ref, dtb_ref)):
        dt = dt_ref[:, d * nheads:(d + 1) * nheads]
        la = dt * a[:, d * nheads:(d + 1) * nheads]
        p1 = la.astype(BF16)
        r1 = la - p1.astype(F32)
        p2 = r1.astype(BF16)
        p3 = (r1 - p2.astype(F32)).astype(BF16)
        acum = (jnp.dot(tri, p1, preferred_element_type=F32) + jnp.dot(tri, p2, preferred_element_type=F32)
                + jnp.dot(tri, p3, preferred_element_type=F32))
        tot = acum[q - 1:q, :]
        e = acum if d == 0 else tot - acum + la
        es.append(e)
        dts.append(dt)
        ws.append(dt * jnp.exp2(tot - e))
    e2 = jnp.concatenate(es, axis=1)
    e_sc[...] = e2
    e2t = e2.T
    et_sc[...] = (e2t - jnp.log2(jnp.concatenate(dts, axis=1).T)).reshape(2 * ngrp, hpg, q)
    wt_sc[...] = jnp.concatenate(ws, axis=1).T.reshape(2 * ngrp, hpg, q)
    tot_col = jnp.concatenate([e2t[0:nheads, q - 1:q], e2t[nheads:2 * nheads, 0:1]], axis=0)
    ext_sc[...] = jnp.broadcast_to(jnp.exp2(tot_col), (2 * nheads, q)).reshape(2 * ngrp, hpg, q)

    lane = lax.broadcasted_iota(jnp.int32, (q, 2 * nheads), 1)
    li = lax.broadcasted_iota(jnp.int32, (q, q), 0)
    si = lax.broadcasted_iota(jnp.int32, (q, q), 1)
    lo = lax.broadcasted_iota(jnp.int32, (q, 2 * p), 1) < p
    causal = (si <= li, si >= li)

    dirs = ((xf_ref, bf_ref, cf_ref, yf_ref), (xb_ref, bb_ref, cb_ref, yb_ref))

    def group(g, carry):
        pre = []
        for d, (x_ref, b_ref, c_ref, y_ref) in enumerate(dirs):
            bg = b_ref[g]
            cg = c_ref[g]
            st = st_sc[d, g]
            pre.append((_nt_dot(cg, bg),
                        bg.astype(F32).T,
                        st,
                        jnp.dot(cg, st.astype(BF16), preferred_element_type=F32),
                        et_sc[d * ngrp + g], wt_sc[d * ngrp + g], ext_sc[d * ngrp + g]))
        e_all = e_sc[...]
        for d, (x_ref, b_ref, c_ref, y_ref) in enumerate(dirs):
            cbm, bgt, st, z, et, wt, ext = pre[d]
            for jp in range(hpg // 2):
                ms, decc, bws, exts = [], [], [], []
                for j in (2 * jp, 2 * jp + 1):
                    col = d * nheads + g * hpg + j
                    sel = lane == col
                    ecol = jnp.sum(jnp.where(sel, e_all, 0.0), axis=1, keepdims=True)
                    decc.append(jnp.exp2(ecol))
                    lm = jnp.exp2(jnp.where(causal[d], ecol - et[j:j + 1, :], -jnp.inf))
                    ms.append((cbm * lm).astype(BF16))
                    bws.append((bgt * wt[j:j + 1, :]).astype(BF16))
                    exts.append(ext[j:j + 1, :])
                cols = slice(jp * 2 * p, (jp + 1) * 2 * p)
                xp = x_ref[g, :, cols]
                zero = jnp.zeros_like(xp)
                rhs = jnp.concatenate([jnp.where(lo, xp, zero), jnp.where(lo, zero, xp)], axis=0)
                y = jnp.dot(jnp.concatenate(ms, axis=1), rhs, preferred_element_type=F32)
                y = y + jnp.where(lo, decc[0], decc[1]) * z[:, cols]
                y_ref[g, :, cols] = y.astype(y_ref.dtype)
                contrib = jnp.dot(jnp.concatenate(bws, axis=1), rhs, preferred_element_type=F32)
                decay = jnp.where(lo[0:1, :], exts[0], exts[1])
                st_sc[d, g, :, cols] = st[:, cols] * decay + contrib
        return carry

    lax.fori_loop(0, ngrp, group, 0, unroll=4)


def _ssd_scan(xs_g, bm_g, cm_g, dt, a_log, nbatch, seq, nctx):
    ngrp, t, gw = xs_g.shape
    nst = bm_g.shape[2]
    q = SSD_CHUNK
    nheads = a_log.shape[1]
    hpg = nheads // ngrp
    nlat = seq // q
    nctxc = nctx // q
    latc = nbatch * nlat

    def fidx(b, s):
        return jnp.where(s < nctxc, latc + b * nctxc + s, b * nlat + s - nctxc)

    def bidx(b, s):
        return jnp.where(s < nctxc, latc + b * nctxc + nctxc - 1 - s, b * nlat + nlat - 1 - (s - nctxc))

    def specs(idx):
        return [pl.BlockSpec((ngrp, q, gw), lambda b, s: (0, idx(b, s), 0)),
                pl.BlockSpec((ngrp, q, nst), lambda b, s: (0, idx(b, s), 0)),
                pl.BlockSpec((ngrp, q, nst), lambda b, s: (0, idx(b, s), 0)),
                pl.BlockSpec((q, 2 * nheads), lambda b, s: (idx(b, s), 0))]

    tri = jnp.tril(jnp.ones((q, q), BF16))
    yshape = jax.ShapeDtypeStruct((ngrp, t, gw), BF16)
    small = pltpu.VMEM((2 * ngrp, hpg, q), F32)
    return pl.pallas_call(
        functools.partial(_ssd_scan_kernel, nheads=nheads),
        grid=(nbatch, nctxc + nlat),
        in_specs=[pl.BlockSpec((1, 2 * nheads), lambda b, s: (0, 0)),
                  pl.BlockSpec((q, q), lambda b, s: (0, 0))] + specs(fidx) + specs(bidx),
        out_specs=[pl.BlockSpec((ngrp, q, gw), lambda b, s: (0, fidx(b, s), 0)),
                   pl.BlockSpec((ngrp, q, gw), lambda b, s: (0, bidx(b, s), 0))],
        out_shape=[yshape, yshape],
        scratch_shapes=[pltpu.VMEM((2, ngrp, nst, gw), F32), pltpu.VMEM((q, 2 * nheads), F32),
                        small, small, small],
        compiler_params=_cparams(("arbitrary", "arbitrary"), 48),
        name="ssd_scan",
    )(a_log.reshape(1, 2 * nheads), tri, xs_g, bm_g, cm_g, dt, xs_g, bm_g, cm_g, dt)


def _ssd_out_kernel(yf_ref, yb_ref, xs_ref, z_ref, dx_ref, ng_ref, w_ref, mod_ref, *rest, nsteps, gps, gw, split_at):
    o_ref, acc_sc = rest[-2:]
    s = pl.program_id(1)

    @pl.when(s == 0)
    def _():
        acc_sc[...] = jnp.zeros_like(acc_sc)

    acc = acc_sc[...]
    for g in range(gps):
        sl = slice(g * gw, (g + 1) * gw)
        z = z_ref[:, sl].astype(F32)
        y = yf_ref[g].astype(F32) + yb_ref[g].astype(F32) + xs_ref[g].astype(F32) * dx_ref[:, sl]
        y = y * (z / (1.0 + jnp.exp(-z)))
        ms = jnp.mean(y * y, axis=-1, keepdims=True)
        yn = (y * lax.rsqrt(ms + EPS) * ng_ref[:, sl]).astype(BF16)
        k0 = pl.multiple_of(s * (gps * gw) + g * gw, gw)
        acc = acc + jnp.dot(yn, w_ref[0, pl.ds(k0, gw), :], preferred_element_type=F32)

    @pl.when(s < nsteps - 1)
    def _():
        acc_sc[...] = acc

    @pl.when(s == nsteps - 1)
    def _():
        hres = rest[0][...] if split_at is None else _pick_part(pl.program_id(0), split_at, rest[0], rest[1])
        o_ref[...] = hres + mod_ref[0][2:3] * acc


def _ssd_out(yf, yb, xs_g, zx, d_skip, norm_g, w_out, layer, h, mods, rows, seq, nbatch, tm=512, gps=2):
    ngrp, _, gw = xs_g.shape
    d_inner = ngrp * gw
    d = w_out.shape[2]
    nsteps = ngrp // gps
    dx = jnp.repeat(d_skip, SSD_HEAD_DIM).reshape(1, d_inner)
    gspec = pl.BlockSpec((gps, tm, gw), lambda i, g: (g, i, 0))
    if isinstance(h, tuple):
        h_specs, split_at = _split_specs(h, tm, d, lambda i, g: 0)
        h_args = list(h)
    else:
        h_specs, split_at, h_args = [pl.BlockSpec((tm, d), lambda i, g: (i, 0))], None, [h]
    return pl.pallas_call(
        functools.partial(_ssd_out_kernel, nsteps=nsteps, gps=gps, gw=gw, split_at=split_at),
        grid=(rows // tm, nsteps),
        in_specs=[gspec, gspec, gspec,
                  pl.BlockSpec((tm, gps * gw), lambda i, g: (i, g)),
                  pl.BlockSpec((1, gps * gw), lambda i, g: (0, g)),
                  pl.BlockSpec((1, gps * gw), lambda i, g: (0, g)),
                  pl.BlockSpec((1, d_inner, d), lambda i, g: (layer, 0, 0), pipeline_mode=pl.Buffered(1)),
                  pl.BlockSpec((1, 6, d), lambda i, g: (jnp.minimum((i * tm) // seq, nbatch), 0, 0))] + h_specs,
        out_specs=pl.BlockSpec((tm, d), lambda i, g: (i, 0)),
        out_shape=jax.ShapeDtypeStruct((rows, d), F32),
        scratch_shapes=[pltpu.VMEM((tm, d), F32)],
        compiler_params=_cparams(("arbitrary", "arbitrary"), 56),
        name="ssd_gate_norm_out_proj",
    )(yf, yb, xs_g, zx, dx, norm_g.reshape(1, d_inner), w_out, mods, *h_args)


def kernel(x, c, ctx, c_ctx, ada_w, ada_b, norm_mix_g, norm_mlp_g, mlp_w1, mlp_w2, ssd_w_in, ssd_conv_w, ssd_conv_b, ssd_dt_bias, ssd_a_log, ssd_d, ssd_norm_g, ssd_w_out, gqa_w_qkv, gqa_sink, gqa_w_out, diff_w_qkv, diff_lam_q1, diff_lam_k1, diff_lam_q2, diff_lam_k2, diff_subln_g, diff_w_out, final_norm_g):
    nbatch, seq, d = x.shape
    nctx = ctx.shape[1]
    depth = ada_w.shape[0]
    nlat_rows = nbatch * seq
    t = nlat_rows + nbatch * nctx

    cond8 = jnp.zeros((SUBLANES, d), F32).at[:nbatch].set(c).at[nbatch].set(c_ctx)
    as_mods = lambda m: m[0, :nbatch + 1].reshape(nbatch + 1, 6, d)
    mods = as_mods(_ada_layers(cond8, ada_w, ada_b, 1))
    cos2, sin_s = _rope_tables(seq, GQA_HEAD_DIM)

    h = (x.reshape(nlat_rows, d), ctx.reshape(nbatch * nctx, d))

    def layer_weights(i):
        mix = ((ssd_w_in, ssd_w_out), (gqa_w_qkv, gqa_w_out), (diff_w_qkv, diff_w_out))[i % N_MIXERS]
        return [(mlp_w1, i), (mlp_w2, i), (mix[0], i // N_MIXERS), (mix[1], i // N_MIXERS)]

    w0 = layer_weights(0)
    wcur = [None, None, w0[2][0][w0[2][1]:w0[2][1] + 1].astype(BF16), None]
    for i in range(depth):
        kind, j = i % N_MIXERS, i // N_MIXERS
        need_ctx = i < depth - 1
        rows = t if need_ctx else nlat_rows
        w1_b, w2_b, win_b, wout_b = wcur
        assert kind == 0 or not isinstance(h, tuple)
        if kind == 0:
            nh2 = 2 * ssd_a_log.shape[2]
            first = [w0[0], w0[1], w0[3]] if i == 0 else []
            zx, dt_raw, casted = _proj(h, norm_mix_g[i], mods, win_b, 0, ssd_w_in.shape[2] - nh2, nh2, seq, nbatch,
                                       casts=first)
            if first:
                w1_b, w2_b, wout_b = casted
            xs_g, bm_g, cm_g, dt = _ssd_conv(zx, dt_raw, ssd_conv_w[j], ssd_conv_b[j], ssd_dt_bias[j],
                                             ssd_w_out.shape[1], nbatch, seq, nctx)
            yf, yb = _ssd_scan(xs_g, bm_g, cm_g, dt, ssd_a_log[j], nbatch, seq, nctx)
            h = _ssd_out(yf, yb, xs_g, zx, ssd_d[j], ssd_norm_g[j], wout_b, 0, h, mods, rows, seq, nbatch)
        elif kind == 1:
            p = _proj(h, norm_mix_g[i], mods, win_b, 0, gqa_w_qkv.shape[2], 0, seq, nbatch)[0]
            o = _gqa(p, gqa_sink[j], cos2, sin_s, nbatch, seq, nctx)
            h = _outproj(o, wout_b, 0, h, mods, rows, seq, nbatch)
        else:
            lambda_init = 0.8 - 0.6 * math.exp(-0.3 * i)
            p = _proj(h, norm_mix_g[i], mods, win_b, 0, diff_w_qkv.shape[2], 0, seq, nbatch)[0]
            o = _diff_attn(p, diff_lam_q1[j], diff_lam_k1[j], diff_lam_q2[j], diff_lam_k2[j], diff_subln_g[j],
                           cos2, sin_s, nbatch, seq, nctx, lambda_init)
            h = _outproj(o, wout_b, 0, h, mods, rows, seq, nbatch)
        h, wcur, mods_next = _mlp(h, norm_mlp_g[i], mods, w1_b, w2_b, 0, final_norm_g, rows, seq, nbatch,
                                  final_norm=not need_ctx, casts=layer_weights(i + 1) if need_ctx else (),
                                  ada=(cond8, ada_w, ada_b, i + 1) if need_ctx else None)
        if need_ctx:
            mods = as_mods(mods_next)
    return h[:nlat_rows].reshape(nbatch, seq, d)
```
